```python
import math
import jax, jax.numpy as jnp
from jax import lax
import numpy as np

D_MODEL = 1024
BATCH = 16
SEQ = 256
DEPTH = 4
DEC_BATCH = 4
DEC_SEQ = 1024
PAST_LEN = 512

GRID_W = 64
MIX_W = D_MODEL
N_MIXERS = 3
QBLK = 128
ROPE_BASE = 10000.0
NORM_EPS = 1e-6
HQ_A = 16
HKV_A = 4
G_A = HQ_A // HKV_A
DH_A = MIX_W // HQ_A
WINDOW = 128
WBLK = WINDOW
H_B = 8
DH_B = MIX_W // (2 * H_B)
H_C = 8
DK_C = MIX_W // H_C
DV_C = MIX_W // H_C
CONV_K = 3
CHUNK = 64
IN_A = HQ_A * DH_A + 2 * HKV_A * DH_A + MIX_W
IN_B = 4 * MIX_W
IN_C = 4 * MIX_W + 4 * H_C

kernel_name = "hybrid_diffusion_prefix_trunk_step"

F32 = jnp.float32


def rmsnorm(x, w):
    xf = x.astype(F32)
    y = xf * lax.rsqrt(jnp.mean(xf * xf, axis=-1, keepdims=True) + NORM_EPS)
    return (y * w.astype(F32)).astype(x.dtype)


def l2norm(x):
    xf = x.astype(F32)
    return (xf * lax.rsqrt(jnp.sum(xf * xf, axis=-1, keepdims=True) + 1e-6)).astype(x.dtype)


def ada_modulate(x, norm_w, mod_w, mod_b, cond):
    m = jax.nn.silu(cond) @ mod_w + mod_b
    shift, scale, gate = jnp.split(m, 3, axis=-1)
    h = rmsnorm(x, norm_w) * (1 + scale[:, None, :]) + shift[:, None, :]
    return h, gate[:, None, :]


def axial_angles(n, dh):
    rows = n // GRID_W
    row = jnp.repeat(jnp.arange(rows), GRID_W).astype(F32)
    col = jnp.tile(jnp.arange(GRID_W), rows).astype(F32)
    quarter = dh // 4
    inv = ROPE_BASE ** (-jnp.arange(quarter, dtype=F32) / quarter)
    return row[:, None] * inv, col[:, None] * inv


def rope_1d(x, ang):
    d2 = x.shape[-1] // 2
    cos = jnp.cos(ang)[None, :, None, :].astype(x.dtype)
    sin = jnp.sin(ang)[None, :, None, :].astype(x.dtype)
    x1, x2 = x[..., :d2], x[..., d2:]
    return jnp.concatenate([x1 * cos - x2 * sin, x2 * cos + x1 * sin], axis=-1)


def axial_rope(x, ang_row, ang_col):
    h = x.shape[-1] // 2
    return jnp.concatenate([rope_1d(x[..., :h], ang_row), rope_1d(x[..., h:], ang_col)], axis=-1)


def query_blocks(fn, q):
    b, n = q.shape[:2]
    qb = jnp.moveaxis(q.reshape((b, n // QBLK, QBLK) + q.shape[2:]), 1, 0)
    o = jnp.moveaxis(lax.map(fn, qb), 0, 1)
    return o.reshape((b, n) + o.shape[3:])


def sink_softmax(s, sink):
    m = jnp.maximum(jnp.max(s, axis=-1, keepdims=True), sink)
    e = jnp.exp(s - m)
    return e / (jnp.sum(e, axis=-1, keepdims=True) + jnp.exp(sink - m))


def sink_attention(q, k, v, sink):
    scale = DH_A ** -0.5
    sk = sink.astype(F32).reshape(1, HKV_A, G_A, 1, 1)

    def block(qb):
        b, nq = qb.shape[:2]
        qg = qb.reshape(b, nq, HKV_A, G_A, DH_A)
        s = jnp.einsum('bqhgd,bkhd->bhgqk', qg, k).astype(F32) * scale
        p = sink_softmax(s, sk).astype(v.dtype)
        return jnp.einsum('bhgqk,bkhd->bqhgd', p, v).reshape(b, nq, HQ_A, DH_A)

    return query_blocks(block, q)


def banded_window_attention(q, k, v, kc, vc, sink):
    b, n = q.shape[:2]
    nb = n // WBLK
    scale = DH_A ** -0.5
    qb = q.reshape(b, nb, WBLK, HKV_A, G_A, DH_A)
    pad = ((0, 0), (WBLK, WBLK), (0, 0), (0, 0))
    kp = jnp.pad(k, pad).reshape(b, nb + 2, WBLK, HKV_A, DH_A)
    vp = jnp.pad(v, pad).reshape(b, nb + 2, WBLK, HKV_A, DH_A)
    kw = jnp.concatenate([kp[:, :-2], kp[:, 1:-1], kp[:, 2:]], axis=2)
    vw = jnp.concatenate([vp[:, :-2], vp[:, 1:-1], vp[:, 2:]], axis=2)
    blk = jnp.arange(nb)[:, None, None]
    qi = blk * WBLK + jnp.arange(WBLK)[None, :, None]
    kj = (blk - 1) * WBLK + jnp.arange(3 * WBLK)[None, None, :]
    valid = (jnp.abs(kj - qi) <= WINDOW) & (kj >= 0) & (kj < n)
    s_loc = jnp.einsum('bnqhgd,bnkhd->bnhgqk', qb, kw).astype(F32) * scale
    s_loc = jnp.where(valid[None, :, None, None, :, :], s_loc, -jnp.inf)
    s_ctx = jnp.einsum('bnqhgd,bchd->bnhgqc', qb, kc).astype(F32) * scale
    s = jnp.concatenate([s_loc, s_ctx], axis=-1)
    p = sink_softmax(s, sink.astype(F32).reshape(1, 1, HKV_A, G_A, 1, 1)).astype(v.dtype)
    o = (jnp.einsum('bnhgqk,bnkhd->bnqhgd', p[..., :3 * WBLK], vw)
         + jnp.einsum('bnhgqc,bchd->bnqhgd', p[..., 3 * WBLK:], vc))
    return o.reshape(b, n, HQ_A, DH_A)


def mixer_a_project(h, in_w):
    b, n, _ = h.shape
    pr = h @ in_w
    nq, nkv = HQ_A * DH_A, HKV_A * DH_A
    q = pr[..., :nq].reshape(b, n, HQ_A, DH_A)
    k = pr[..., nq:nq + nkv].reshape(b, n, HKV_A, DH_A)
    v = pr[..., nq + nkv:nq + 2 * nkv].reshape(b, n, HKV_A, DH_A)
    z = pr[..., nq + 2 * nkv:]
    return q, k, v, z


def mixer_a_context(h, p):
    b, n, _ = h.shape
    q, k, v, z = mixer_a_project(h, p["in_w"])
    o = sink_attention(q, k, v, p["sink"])
    out = (o.reshape(b, n, MIX_W) * jax.nn.silu(z)) @ p["out_w"]
    return out, k, v


def mixer_a_latent(h, p, kc, vc):
    b, n, _ = h.shape
    q, k, v, z = mixer_a_project(h, p["in_w"])
    ar, ac = axial_angles(n, DH_A)
    q = axial_rope(q, ar, ac)
    k = axial_rope(k, ar, ac)
    o = banded_window_attention(q, k, v, kc, vc, p["sink"])
    return (o.reshape(b, n, MIX_W) * jax.nn.silu(z)) @ p["out_w"]


def mixer_b_project(h, in_w):
    b, n, _ = h.shape
    pr = h @ in_w
    q = pr[..., :MIX_W].reshape(b, n, H_B, 2, DH_B)
    k = pr[..., MIX_W:2 * MIX_W].reshape(b, n, H_B, 2, DH_B)
    v = pr[..., 2 * MIX_W:3 * MIX_W].reshape(b, n, H_B, 2 * DH_B)
    z = pr[..., 3 * MIX_W:]
    return q, k, v, z


def diff_lambda(p, lam_init):
    dot_exp = lambda a, c: jnp.exp(jnp.sum(a.astype(F32) * c.astype(F32)))
    return dot_exp(p["lq1"], p["lk1"]) - dot_exp(p["lq2"], p["lk2"]) + lam_init


def diff_attention(q, k, v, lam):
    scale = DH_B ** -0.5

    def block(qb):
        s = jnp.einsum('bqhcd,bkhcd->bhcqk', qb, k).astype(F32) * scale
        pr = jax.nn.softmax(s, axis=-1)
        a = (pr[:, :, 0] - lam * pr[:, :, 1]).astype(v.dtype)
        return jnp.einsum('bhqk,bkhe->bqhe', a, v)

    return query_blocks(block, q)


def mixer_b_out(o, z, p, lam_init):
    b, n = o.shape[:2]
    o = rmsnorm(o, p["subln_w"]) * (1.0 - lam_init)
    return (o.reshape(b, n, MIX_W) * jax.nn.silu(z)) @ p["out_w"]


def mixer_b_context(h, p, lam_init):
    q, k, v, z = mixer_b_project(h, p["in_w"])
    o = diff_attention(q, k, v, diff_lambda(p, lam_init))
    return mixer_b_out(o, z, p, lam_init), k, v


def mixer_b_latent(h, p, lam_init, kc, vc):
    b, n, _ = h.shape
    q, k, v, z = mixer_b_project(h, p["in_w"])
    ar, ac = axial_angles(n, DH_B)
    rot = lambda t: axial_rope(t.reshape(b, n, 2 * H_B, DH_B), ar, ac).reshape(b, n, H_B, 2, DH_B)
    q, k = rot(q), rot(k)
    k_all = jnp.concatenate([k, kc], axis=1)
    v_all = jnp.concatenate([v, vc], axis=1)
    o = diff_attention(q, k_all, v_all, diff_lambda(p, lam_init))
    return mixer_b_out(o, z, p, lam_init)


def depthwise_conv_centred(x, w):
    ch = x.shape[-1]
    return lax.conv_general_dilated(
        x, w[:, None, :], window_strides=(1,), padding=((CONV_K // 2, CONV_K // 2),),
        dimension_numbers=('NWC', 'WIO', 'NWC'), feature_group_count=ch)


def gated_delta_chunked(q, k, v, g, beta, s0):
    dt = v.dtype
    q, k, v, g, beta, s0 = (t.astype(F32) for t in (q, k, v, g, beta, s0))
    b, n, h, dk = q.shape
    dv = v.shape[-1]
    nc = n // CHUNK

    def to_chunks(t):
        t = t.reshape((b, nc, CHUNK, h) + t.shape[3:])
        return jnp.moveaxis(jnp.moveaxis(t, 1, 0), 2, 3)

    qc, kc, vc, gc, bc = (to_chunks(t) for t in (q, k, v, g, beta))
    gcum = jnp.cumsum(gc, axis=-1)
    idx = jnp.arange(CHUNK)
    incl = idx[:, None] >= idx[None, :]
    strict = idx[:, None] > idx[None, :]
    decay = jnp.exp(jnp.where(incl, gcum[..., :, None] - gcum[..., None, :], -jnp.inf))
    kb = kc * bc[..., None]
    lmat = jnp.where(strict, jnp.einsum('...id,...jd->...ij', kb, kc) * decay, 0.0)
    eye = jnp.eye(CHUNK, dtype=F32)
    rhs = jnp.concatenate([vc * bc[..., None], kb * jnp.exp(gcum)[..., None]], axis=-1)
    sol = lax.linalg.triangular_solve(eye + lmat, rhs, left_side=True, lower=True)
    u, w = sol[..., :dv], sol[..., dv:]

    def step(state, inp):
        qi, ki, ui, wi, gi, di = inp
        intra = jnp.where(incl, jnp.einsum('bhid,bhjd->bhij', qi, ki) * di, 0.0)
        vnew = ui - jnp.einsum('bhck,bhkv->bhcv', wi, state)
        o = (jnp.einsum('bhck,bhkv->bhcv', qi * jnp.exp(gi)[..., None], state)
             + jnp.einsum('bhij,bhjv->bhiv', intra, vnew))
        glast = gi[..., -1:]
        state = (state * jnp.exp(glast)[..., None]
                 + jnp.einsum('bhck,bhcv->bhkv', ki * jnp.exp(glast - gi)[..., None], vnew))
        return state, o

    s_fin, o = lax.scan(step, s0, (qc, kc, u, w, gcum, decay))
    o = jnp.moveaxis(jnp.moveaxis(o, 3, 2), 0, 1).reshape(b, n, h, dv)
    return o.astype(dt), s_fin.astype(dt)


def mixer_c(h, p, s0_f, s0_b):
    b, n, _ = h.shape
    pr = h @ p["in_w"]
    qkv = jax.nn.silu(depthwise_conv_centred(pr[..., :3 * MIX_W], p["conv_w"]))
    z = pr[..., 3 * MIX_W:4 * MIX_W]
    bgate = pr[..., 4 * MIX_W:4 * MIX_W + 2 * H_C].reshape(b, n, 2, H_C)
    agate = pr[..., 4 * MIX_W + 2 * H_C:].reshape(b, n, 2, H_C)
    q = l2norm(qkv[..., :MIX_W].reshape(b, n, H_C, DK_C)) * (DK_C ** -0.5)
    k = l2norm(qkv[..., MIX_W:2 * MIX_W].reshape(b, n, H_C, DK_C))
    v = qkv[..., 2 * MIX_W:].reshape(b, n, H_C, DV_C)
    beta = jax.nn.sigmoid(bgate.astype(F32))
    g = -jnp.exp(p["a_log"].astype(F32)) * jax.nn.softplus(agate.astype(F32) + p["dt_bias"].astype(F32))
    o_f, s_f = gated_delta_chunked(q, k, v, g[:, :, 0], beta[:, :, 0], s0_f)
    rev = lambda t: jnp.flip(t, axis=1)
    o_b, s_b = gated_delta_chunked(rev(q), rev(k), rev(v), rev(g[:, :, 1]), rev(beta[:, :, 1]), s0_b)
    o = rmsnorm(o_f + rev(o_b), p["onorm_w"])
    out = (o.reshape(b, n, MIX_W) * jax.nn.silu(z)) @ p["out_w"]
    return out, jnp.stack([s_f, s_b], axis=1)


def lambda_init_for(layer):
    return 0.8 - 0.6 * math.exp(-0.3 * layer)


def setup_inputs(seed: int = 0) -> dict:
    key = jax.random.key(seed)
    keys = iter(jax.random.split(key, 64))
    nrm = lambda shape, s: jax.random.normal(next(keys), shape, F32) * s
    gain = lambda m: 1.0 + nrm((m,), 0.02)
    d = D_MODEL
    inp = {}
    inp["x_prompt"] = nrm((BATCH, SEQ, d), 1.0)
    inp["x_sample"] = nrm((DEC_BATCH, DEC_SEQ, d), 1.0)
    inp["cache_l0_k"] = nrm((DEC_BATCH, PAST_LEN, HKV_A, DH_A), 1.0)
    inp["cache_l0_v"] = nrm((DEC_BATCH, PAST_LEN, HKV_A, DH_A), 1.0)
    inp["cache_l1_k"] = nrm((DEC_BATCH, PAST_LEN, H_B, 2, DH_B), 1.0)
    inp["cache_l1_v"] = nrm((DEC_BATCH, PAST_LEN, H_B, 2 * DH_B), 1.0)
    inp["state_l2"] = nrm((DEC_BATCH, 2, H_C, DK_C, DV_C), 0.3)
    inp["cache_l3_k"] = nrm((DEC_BATCH, PAST_LEN, HKV_A, DH_A), 1.0)
    inp["cache_l3_v"] = nrm((DEC_BATCH, PAST_LEN, HKV_A, DH_A), 1.0)
    inp["c"] = nrm((DEC_BATCH, d), 1.0)
    inp["c_ctx"] = nrm((d,), 1.0)

    def common(i, in_dim):
        inp[f"l{i}_norm_w"] = gain(d)
        inp[f"l{i}_mod_w"] = nrm((d, 3 * d), 0.5 * d ** -0.5)
        inp[f"l{i}_mod_b"] = nrm((3 * d,), 0.01)
        inp[f"l{i}_in_w"] = nrm((d, in_dim), d ** -0.5)
        inp[f"l{i}_out_w"] = nrm((MIX_W, d), MIX_W ** -0.5)

    common(0, IN_A)
    inp["l0_sink"] = nrm((HQ_A,), 0.5)
    common(1, IN_B)
    inp["l1_lambda_q1"] = nrm((DH_B,), 0.1)
    inp["l1_lambda_k1"] = nrm((DH_B,), 0.1)
    inp["l1_lambda_q2"] = nrm((DH_B,), 0.1)
    inp["l1_lambda_k2"] = nrm((DH_B,), 0.1)
    inp["l1_subln_w"] = gain(2 * DH_B)
    common(2, IN_C)
    inp["l2_conv_w"] = nrm((CONV_K, 3 * MIX_W), CONV_K ** -0.5)
    inp["l2_a_log"] = jnp.log(jax.random.uniform(next(keys), (2, H_C), F32, 1.0, 16.0))
    dtv = jnp.exp(jax.random.uniform(next(keys), (2, H_C), F32, math.log(1e-3), math.log(1e-1)))
    inp["l2_dt_bias"] = dtv + jnp.log(-jnp.expm1(-dtv))
    inp["l2_onorm_w"] = gain(DV_C)
    common(3, IN_A)
    inp["l3_sink"] = nrm((HQ_A,), 0.5)
    inp["final_norm_w"] = gain(d)
    return inp


def reference(x_prompt, x_sample, cache_l0_k, cache_l0_v, cache_l1_k, cache_l1_v, state_l2,
              cache_l3_k, cache_l3_v, c, c_ctx,
              l0_norm_w, l0_mod_w, l0_mod_b, l0_in_w, l0_out_w, l0_sink,
              l1_norm_w, l1_mod_w, l1_mod_b, l1_in_w, l1_out_w,
              l1_lambda_q1, l1_lambda_k1, l1_lambda_q2, l1_lambda_k2, l1_subln_w,
              l2_norm_w, l2_mod_w, l2_mod_b, l2_in_w, l2_out_w,
              l2_conv_w, l2_a_log, l2_dt_bias, l2_onorm_w,
              l3_norm_w, l3_mod_w, l3_mod_b, l3_in_w, l3_out_w, l3_sink,
              final_norm_w):
    layers = [
        dict(norm_w=l0_norm_w, mod_w=l0_mod_w, mod_b=l0_mod_b, in_w=l0_in_w, out_w=l0_out_w, sink=l0_sink),
        dict(norm_w=l1_norm_w, mod_w=l1_mod_w, mod_b=l1_mod_b, in_w=l1_in_w, out_w=l1_out_w,
             lq1=l1_lambda_q1, lk1=l1_lambda_k1, lq2=l1_lambda_q2, lk2=l1_lambda_k2, subln_w=l1_subln_w),
        dict(norm_w=l2_norm_w, mod_w=l2_mod_w, mod_b=l2_mod_b, in_w=l2_in_w, out_w=l2_out_w,
             conv_w=l2_conv_w, a_log=l2_a_log, dt_bias=l2_dt_bias, onorm_w=l2_onorm_w),
        dict(norm_w=l3_norm_w, mod_w=l3_mod_w, mod_b=l3_mod_b, in_w=l3_in_w, out_w=l3_out_w, sink=l3_sink),
    ]
    caches = [(cache_l0_k, cache_l0_v), (cache_l1_k, cache_l1_v), (state_l2,), (cache_l3_k, cache_l3_v)]

    xc, xl = x_prompt, x_sample
    new_state = []
    for i in range(DEPTH):
        p = layers[i]
        hc, gate_c = ada_modulate(xc, p["norm_w"], p["mod_w"], p["mod_b"], c_ctx[None, :])
        hl, gate_l = ada_modulate(xl, p["norm_w"], p["mod_w"], p["mod_b"], c)
        kind = i % N_MIXERS
        if kind == 0:
            oc, kc_new, vc_new = mixer_a_context(hc, p)
            ol = mixer_a_latent(hl, p, caches[i][0], caches[i][1])
            new_state += [kc_new, vc_new]
        elif kind == 1:
            lam_init = lambda_init_for(i)
            oc, kc_new, vc_new = mixer_b_context(hc, p, lam_init)
            ol = mixer_b_latent(hl, p, lam_init, caches[i][0], caches[i][1])
            new_state += [kc_new, vc_new]
        else:
            zeros = jnp.zeros((xc.shape[0], H_C, DK_C, DV_C), xc.dtype)
            oc, st_new = mixer_c(hc, p, zeros, zeros)
            ol, _ = mixer_c(hl, p, caches[i][0][:, 0], caches[i][0][:, 1])
            new_state.append(st_new)
        xc = xc + gate_c * oc
        xl = xl + gate_l * ol

    y_prompt = rmsnorm(xc, final_norm_w)
    y_sample = rmsnorm(xl, final_norm_w)
    new_l0_k, new_l0_v, new_l1_k, new_l1_v, new_l2_state, new_l3_k, new_l3_v = new_state
    return (y_prompt, y_sample, new_l0_k, new_l0_v, new_l1_k, new_l1_v, new_l2_state, new_l3_k, new_l3_v)
```

```python
import functools
import math

import jax
import jax.numpy as jnp
from jax import lax
from jax.experimental import pallas as pl
from jax.experimental.pallas import tpu as pltpu

F32 = jnp.float32
BF16 = jnp.bfloat16

D = 1024
N_CTX_B, N_CTX = 16, 256
N_LAT_B, N_LAT = 4, 1024
PAST = 512
GRID_W = 64
ROPE_BASE = 10000.0
NORM_EPS = 1e-6
DH = 64
HKV_A = 4
WINDOW = 128
H_C = 8
DK_C = 128
CHUNK = 64
LANES = 128
TM = 256
VMEM_LIMIT = 48 * 1024 * 1024

_NT = (((1,), (1,)), ((), ()))


def _sigmoid(x):
    return 1.0 / (1.0 + jnp.exp(-x))


def _silu(x):
    return x * _sigmoid(x)


def _softplus(x):
    return jnp.maximum(x, 0.0) + jnp.log1p(jnp.exp(-jnp.abs(x)))


def _rms(x, w):
    return x * lax.rsqrt(jnp.mean(x * x, axis=-1, keepdims=True) + NORM_EPS) * w


def _params(sem):
    return pltpu.CompilerParams(dimension_semantics=sem, vmem_limit_bytes=VMEM_LIMIT)


def _mod_kernel(cond_ref, w0, w1, w2, w3, b0, b1, b2, b3, o0, o1, o2, o3):
    a = _silu(cond_ref[...]).astype(BF16)
    for w, b, o in ((w0, b0, o0), (w1, b1, o1), (w2, b2, o2), (w3, b3, o3)):
        o[...] = jnp.dot(a, w[...].astype(BF16), preferred_element_type=F32) + b[...]


def _modulation(cond, mod_ws, mod_bs):
    tn = 512
    wspec = pl.BlockSpec((D, tn), lambda j: (0, j))
    bspec = pl.BlockSpec((1, tn), lambda j: (0, j))
    ospec = pl.BlockSpec((8, tn), lambda j: (0, j))
    outs = pl.pallas_call(
        _mod_kernel,
        grid=(3 * D // tn,),
        in_specs=[pl.BlockSpec((8, D), lambda j: (0, 0))] + [wspec] * 4 + [bspec] * 4,
        out_specs=[ospec] * 4,
        out_shape=[jax.ShapeDtypeStruct((8, 3 * D), F32)] * 4,
        compiler_params=_params(("arbitrary",)),
        name="adaln_mod",
    )(cond, *mod_ws, *[b.reshape(1, 3 * D) for b in mod_bs])
    return [o.reshape(8, 1, 3 * D) for o in outs]


def _adaln_h(x_ref, nw_ref, mod_ref):
    y = _rms(x_ref[...], nw_ref[...])
    return (y * (1.0 + mod_ref[:, D:2 * D]) + mod_ref[:, 0:D]).astype(BF16)


def _proj(h, w_ref, lo, hi):
    return jnp.dot(h, w_ref[:, lo:hi], preferred_element_type=F32)


def _rope(x, cos_ref, sin_ref):
    width = x.shape[1]
    lane = lax.broadcasted_iota(jnp.int32, x.shape, 1)
    partner = jnp.where((lane & 31) < 16, pltpu.roll(x, width - 16, 1), pltpu.roll(x, 16, 1))
    reps = width // LANES
    return x * jnp.tile(cos_ref[...], (1, reps)) + partner * jnp.tile(sin_ref[...], (1, reps))


def _inproj_a_kernel(*refs, rope):
    if rope:
        x_ref, nw_ref, mod_ref, w_ref, cos_ref, sin_ref, q_ref, k_ref, v_ref, z_ref = refs
    else:
        x_ref, nw_ref, mod_ref, w_ref, q_ref, k_ref, v_ref, z_ref = refs
    h = _adaln_h(x_ref, nw_ref, mod_ref)
    for j in range(2):
        acc = _proj(h, w_ref, j * 512, (j + 1) * 512)
        if rope:
            acc = _rope(acc, cos_ref, sin_ref)
        q_ref[:, j * 512:(j + 1) * 512] = (acc * (DH ** -0.5)).astype(BF16)
    kv = _proj(h, w_ref, 1024, 1536)
    k = kv[:, :256]
    if rope:
        k = _rope(k, cos_ref, sin_ref)
    k_ref[...] = k
    v_ref[...] = kv[:, 256:]
    for j in range(2):
        z_ref[:, j * 512:(j + 1) * 512] = _proj(h, w_ref, 1536 + j * 512, 2048 + j * 512)


def _inproj_b_kernel(*refs, rope):
    if rope:
        x_ref, nw_ref, mod_ref, w_ref, cos_ref, sin_ref, q_ref, k_ref, v_ref, z_ref = refs
    else:
        x_ref, nw_ref, mod_ref, w_ref, q_ref, k_ref, v_ref, z_ref = refs
    h = _adaln_h(x_ref, nw_ref, mod_ref)
    for j in range(2):
        sl = slice(j * 512, (j + 1) * 512)
        q = _proj(h, w_ref, j * 512, (j + 1) * 512)
        k = _proj(h, w_ref, D + j * 512, D + (j + 1) * 512)
        if rope:
            q = _rope(q, cos_ref, sin_ref)
            k = _rope(k, cos_ref, sin_ref)
        q_ref[:, sl] = (q * (DH ** -0.5)).astype(BF16)
        k_ref[:, sl] = k
        v_ref[:, sl] = _proj(h, w_ref, 2 * D + j * 512, 2 * D + (j + 1) * 512)
        z_ref[:, sl] = _proj(h, w_ref, 3 * D + j * 512, 3 * D + (j + 1) * 512)


def _inproj_c_kernel(x_ref, nw_ref, mod_ref, w_ref, wg_ref, alog_ref, dtb_ref, qkv_ref, z_ref, g_ref):
    h = _adaln_h(x_ref, nw_ref, mod_ref)
    for j in range(6):
        qkv_ref[:, j * 512:(j + 1) * 512] = _proj(h, w_ref, j * 512, (j + 1) * 512)
    for j in range(2):
        z_ref[:, j * 512:(j + 1) * 512] = _proj(h, w_ref, 3 * D + j * 512, 3 * D + (j + 1) * 512)
    acc = jnp.dot(h, wg_ref[...], preferred_element_type=F32)
    lane = lax.broadcasted_iota(jnp.int32, acc.shape, 1)
    g = -jnp.exp(alog_ref[...]) * _softplus(acc + dtb_ref[...])
    g_ref[...] = jnp.where(lane < 2 * H_C, _sigmoid(acc), g)


def _inproj(kind, x, norm_w, mod, w, extra_in, outs, latent, rope_tabs=None):
    rows = x.shape[0]
    per_seq = (N_LAT if latent else N_CTX) // TM
    mod_row = (lambda i: (i // per_seq, 0, 0)) if latent else (lambda i: (4, 0, 0))
    in_specs = [
        pl.BlockSpec((TM, D), lambda i: (i, 0)),
        pl.BlockSpec((1, D), lambda i: (0, 0)),
        pl.BlockSpec((None, 1, 3 * D), mod_row),
        pl.BlockSpec(w.shape, lambda i: (0, 0)),
    ]
    args = [x, norm_w.reshape(1, D), mod, w]
    for e in extra_in:
        in_specs.append(pl.BlockSpec(e.shape, lambda i: (0, 0)))
        args.append(e)
    rope = rope_tabs is not None
    if rope:
        for t in rope_tabs:
            in_specs.append(pl.BlockSpec((TM, LANES), lambda i: (i % per_seq, 0)))
            args.append(t)
    if kind == "a":
        body = functools.partial(_inproj_a_kernel, rope=rope)
    elif kind == "b":
        body = functools.partial(_inproj_b_kernel, rope=rope)
    else:
        body = _inproj_c_kernel
    return pl.pallas_call(
        body,
        grid=(rows // TM,),
        in_specs=in_specs,
        out_specs=[pl.BlockSpec((TM, wd), lambda i: (i, 0)) for wd, _ in outs],
        out_shape=[jax.ShapeDtypeStruct((rows, wd), dt) for wd, dt in outs],
        compiler_params=_params(("parallel",)),
        name=f"inproj_{kind}_{'lat' if latent else 'ctx'}",
    )(*args)


def _outproj_kernel(og_ref, w_ref, x_ref, mod_ref, *rest, final):
    out = jnp.dot(og_ref[...], w_ref[...], preferred_element_type=F32)
    xn = x_ref[...] + mod_ref[:, 2 * D:3 * D] * out
    if final:
        fw_ref, y_ref = rest
        y_ref[...] = _rms(xn, fw_ref[...])
    else:
        (o_ref,) = rest
        o_ref[...] = xn


def _outproj(og, w, x, mod, latent, final_w=None):
    rows = x.shape[0]
    per_seq = (N_LAT if latent else N_CTX) // TM
    mod_row = (lambda i: (i // per_seq, 0, 0)) if latent else (lambda i: (4, 0, 0))
    in_specs = [
        pl.BlockSpec((TM, D), lambda i: (i, 0)),
        pl.BlockSpec((D, D), lambda i: (0, 0)),
        pl.BlockSpec((TM, D), lambda i: (i, 0)),
        pl.BlockSpec((None, 1, 3 * D), mod_row),
    ]
    args = [og, w, x, mod]
    if final_w is not None:
        in_specs.append(pl.BlockSpec((1, D), lambda i: (0, 0)))
        args.append(final_w.reshape(1, D))
    return pl.pallas_call(
        functools.partial(_outproj_kernel, final=final_w is not None),
        grid=(rows // TM,),
        in_specs=in_specs,
        out_specs=pl.BlockSpec((TM, D), lambda i: (i, 0)),
        out_shape=jax.ShapeDtypeStruct((rows, D), F32),
        compiler_params=_params(("parallel",)),
        name=f"outproj_{'lat' if latent else 'ctx'}",
    )(*args)


def _lane_lo(shape):
    return lax.broadcasted_iota(jnp.int32, shape, 1) < DH


def _sink_head(qm, pieces, sink):
    scores = []
    m = None
    for kx, _, valid in pieces:
        s = lax.dot_general(qm, kx, _NT, preferred_element_type=F32)
        if valid is not None:
            s = jnp.where(valid, s, -jnp.inf)
        scores.append(s)
        sm = jnp.max(s, axis=-1, keepdims=True)
        m = sm if m is None else jnp.maximum(m, sm)
    m = jnp.maximum(m, sink)
    denom = jnp.exp(sink - m)
    o = None
    for (_, vx, _), s in zip(pieces, scores):
        e = jnp.exp(s - m)
        denom = denom + jnp.sum(e, axis=-1, keepdims=True)
        po = jnp.dot(e.astype(BF16), vx, preferred_element_type=F32)
        o = po if o is None else o + po
    return o * (1.0 / denom)


def _gqa_block(sink_ref, q_ref, z_ref, o_ref, kv_pieces):
    lo = _lane_lo((1, LANES))
    for j in range(HKV_A // 2):
        variants = []
        for k2, v2, valid in kv_pieces(j):
            variants.append((
                (k2.astype(BF16), pltpu.roll(k2, DH, 1).astype(BF16)),
                (v2.astype(BF16), pltpu.roll(v2, DH, 1).astype(BF16)),
                valid))
        for hh in range(2):
            h = 2 * j + hh
            for pq in range(2):
                blk = 2 * h + pq
                cols = slice(blk * LANES, (blk + 1) * LANES)
                q2 = q_ref[:, cols]
                outs = []
                for a in range(2):
                    sw = 0 if hh == a else 1
                    half = lo if a == 0 else jnp.logical_not(lo)
                    qm = jnp.where(half, q2, jnp.zeros_like(q2))
                    pieces = [(kk[sw], vv[sw], valid) for kk, vv, valid in variants]
                    outs.append(_sink_head(qm, pieces, sink_ref[4 * h + 2 * pq + a]))
                o2 = jnp.where(lo, outs[0], outs[1])
                o_ref[:, cols] = (o2 * _silu(z_ref[:, cols])).astype(BF16)


def _attn_a_ctx_kernel(sink_ref, q_ref, k_ref, v_ref, z_ref, o_ref):
    def kv_pieces(j):
        cols = slice(j * LANES, (j + 1) * LANES)
        return [(k_ref[:, cols], v_ref[:, cols], None)]
    _gqa_block(sink_ref, q_ref, z_ref, o_ref, kv_pieces)


def _attn_a_lat_kernel(sink_ref, q_ref, k_ref, v_ref, kc_ref, vc_ref, z_ref, o_ref):
    n = pl.program_id(1)
    span = 3 * WINDOW
    start = pl.multiple_of(jnp.clip((n - 1) * WINDOW, 0, N_LAT - span), WINDOW)
    qi = n * WINDOW + lax.broadcasted_iota(jnp.int32, (WINDOW, span), 0)
    kj = start + lax.broadcasted_iota(jnp.int32, (WINDOW, span), 1)
    valid = jnp.abs(kj - qi) <= WINDOW

    def kv_pieces(j):
        cols = slice(j * LANES, (j + 1) * LANES)
        return [(k_ref[pl.ds(start, span), cols], v_ref[pl.ds(start, span), cols], valid),
                (kc_ref[:, cols], vc_ref[:, cols], None)]
    _gqa_block(sink_ref, q_ref, z_ref, o_ref, kv_pieces)


def _attn_a(sink, q, k, v, z, cache=None):
    smem = pl.BlockSpec(memory_space=pltpu.SMEM)
    kvw = HKV_A * DH
    if cache is None:
        return pl.pallas_call(
            _attn_a_ctx_kernel,
            grid=(N_CTX_B,),
            in_specs=[smem,
                      pl.BlockSpec((N_CTX, D), lambda b: (b, 0)),
                      pl.BlockSpec((N_CTX, kvw), lambda b: (b, 0)),
                      pl.BlockSpec((N_CTX, kvw), lambda b: (b, 0)),
                      pl.BlockSpec((N_CTX, D), lambda b: (b, 0))],
            out_specs=pl.BlockSpec((N_CTX, D), lambda b: (b, 0)),
            out_shape=jax.ShapeDtypeStruct((N_CTX_B * N_CTX, D), BF16),
            compiler_params=_params(("parallel",)),
            name="attn_a_ctx",
        )(sink, q, k, v, z)
    kc, vc = cache
    nq = N_LAT // WINDOW
    return pl.pallas_call(
        _attn_a_lat_kernel,
        grid=(N_LAT_B, nq),
        in_specs=[smem,
                  pl.BlockSpec((WINDOW, D), lambda b, n: (b * nq + n, 0)),
                  pl.BlockSpec((N_LAT, kvw), lambda b, n: (b, 0)),
                  pl.BlockSpec((N_LAT, kvw), lambda b, n: (b, 0)),
                  pl.BlockSpec((PAST, kvw), lambda b, n: (b, 0)),
                  pl.BlockSpec((PAST, kvw), lambda b, n: (b, 0)),
                  pl.BlockSpec((WINDOW, D), lambda b, n: (b * nq + n, 0))],
        out_specs=pl.BlockSpec((WINDOW, D), lambda b, n: (b * nq + n, 0)),
        out_shape=jax.ShapeDtypeStruct((N_LAT_B * N_LAT, D), BF16),
        compiler_params=_params(("parallel", "arbitrary")),
        name="attn_a_lat",
    )(sink, q, k, v, kc, vc, z)


def _diff_attn_block(lam_refs, subln_ref, q_ref, z_ref, o_ref, kv_pieces, lam_init):
    lq1, lk1, lq2, lk2 = lam_refs
    dot_exp = lambda a, c: jnp.exp(jnp.sum(a[...] * c[...], axis=-1, keepdims=True))
    lam = dot_exp(lq1, lk1) - dot_exp(lq2, lk2) + lam_init
    lo = _lane_lo((1, LANES))
    for h in range(D // LANES):
        cols = slice(h * LANES, (h + 1) * LANES)
        q2 = q_ref[:, cols]
        pieces = kv_pieces(h)
        probs = []
        for c in range(2):
            half = lo if c == 0 else jnp.logical_not(lo)
            qm = jnp.where(half, q2, jnp.zeros_like(q2))
            scores = [lax.dot_general(qm, k2, _NT, preferred_element_type=F32) for k2, _ in pieces]
            m = None
            for s in scores:
                sm = jnp.max(s, axis=-1, keepdims=True)
                m = sm if m is None else jnp.maximum(m, sm)
            es = [jnp.exp(s - m) for s in scores]
            denom = None
            for e in es:
                se = jnp.sum(e, axis=-1, keepdims=True)
                denom = se if denom is None else denom + se
            inv = 1.0 / denom
            probs.append([e * inv for e in es])
        o = None
        for i, (_, vx) in enumerate(pieces):
            a = (probs[0][i] - lam * probs[1][i]).astype(BF16)
            po = jnp.dot(a, vx, preferred_element_type=F32)
            o = po if o is None else o + po
        o = _rms(o, subln_ref[...]) * (1.0 - lam_init)
        o_ref[:, cols] = (o * _silu(z_ref[:, cols])).astype(BF16)


def _attn_b_ctx_kernel(lq1, lk1, lq2, lk2, subln_ref, q_ref, k_ref, v_ref, z_ref, o_ref, *, lam_init):
    def kv_pieces(h):
        cols = slice(h * LANES, (h + 1) * LANES)
        return [(k_ref[:, cols].astype(BF16), v_ref[:, cols].astype(BF16))]
    _diff_attn_block((lq1, lk1, lq2, lk2), subln_ref, q_ref, z_ref, o_ref, kv_pieces, lam_init)


def _attn_b_lat_kernel(lq1, lk1, lq2, lk2, subln_ref, q_ref, k_ref, v_ref, kc_ref, vc_ref, z_ref, o_ref,
                       *, lam_init):
    def kv_pieces(h):
        cols = slice(h * LANES, (h + 1) * LANES)
        return [(k_ref[:, cols].astype(BF16), v_ref[:, cols].astype(BF16)),
                (kc_ref[:, cols].astype(BF16), vc_ref[:, cols].astype(BF16))]
    _diff_attn_block((lq1, lk1, lq2, lk2), subln_ref, q_ref, z_ref, o_ref, kv_pieces, lam_init)


def _attn_b(lams, subln, q, k, v, z, lam_init, cache=None):
    small = [pl.BlockSpec((1, DH), lambda *_: (0, 0))] * 4 + [pl.BlockSpec((1, 2 * DH), lambda *_: (0, 0))]
    small_args = [l.reshape(1, DH) for l in lams] + [subln.reshape(1, 2 * DH)]
    if cache is None:
        blk = pl.BlockSpec((N_CTX, D), lambda b: (b, 0))
        return pl.pallas_call(
            functools.partial(_attn_b_ctx_kernel, lam_init=lam_init),
            grid=(N_CTX_B,),
            in_specs=small + [blk] * 4,
            out_specs=blk,
            out_shape=jax.ShapeDtypeStruct((N_CTX_B * N_CTX, D), BF16),
            compiler_params=_params(("parallel",)),
            name="attn_b_ctx",
        )(*small_args, q, k, v, z)
    kc, vc = cache
    tq = 256
    nq = N_LAT // tq
    qblk = pl.BlockSpec((tq, D), lambda b, n: (b * nq + n, 0))
    return pl.pallas_call(
        functools.partial(_attn_b_lat_kernel, lam_init=lam_init),
        grid=(N_LAT_B, nq),
        in_specs=small + [qblk,
                          pl.BlockSpec((N_LAT, D), lambda b, n: (b, 0)),
                          pl.BlockSpec((N_LAT, D), lambda b, n: (b, 0)),
                          pl.BlockSpec((PAST, D), lambda b, n: (b, 0)),
                          pl.BlockSpec((PAST, D), lambda b, n: (b, 0)),
                          qblk],
        out_specs=qblk,
        out_shape=jax.ShapeDtypeStruct((N_LAT_B * N_LAT, D), BF16),
        compiler_params=_params(("parallel", "arbitrary")),
        name="attn_b_lat",
    )(*small_args, q, k, v, kc, vc, z)


def _mm(a, b):
    return jnp.dot(a.astype(BF16), b.astype(BF16), preferred_element_type=F32)


def _unit_tri_inverse_residual(l, same_blk):
    rd = jnp.where(same_blk, -l, 0.0)
    pk = rd
    for _ in range(3):
        pk = _mm(pk, pk)
        rd = rd + pk + _mm(rd, pk)
    lo = jnp.where(same_blk, 0.0, l)
    m = lo + _mm(rd, lo)
    m2 = _mm(m, m)
    q = m2 - m - _mm(m, m2)
    return q + rd + _mm(q, rd)


def _gdn_kernel(*refs, n, has_s0, want_state):
    it = iter(refs)
    qp_ref, kp_ref, vp_ref, cwq_ref, cwk_ref, cwv_ref, g_ref, z_ref, onw_ref = [next(it) for _ in range(9)]
    s0_ref = next(it) if has_s0 else None
    og_ref = next(it)
    sf_ref = next(it) if want_state else None
    q_s, k_s, v_s, beta_b, gc_b, gtot_b, u_s, w_s, qg_s, kdt_s, intra_s, o_s = it
    nc = n // CHUNK
    h = pl.program_id(1)

    row = lax.broadcasted_iota(jnp.int32, (n, LANES), 0)
    lane = lax.broadcasted_iota(jnp.int32, (n, LANES), 1)

    def conv_silu(p_ref, cw_ref):
        x = p_ref[...]
        xm1 = jnp.where(row == 0, 0.0, pltpu.roll(x, 1, 0))
        xp1 = jnp.where(row == n - 1, 0.0, pltpu.roll(x, n - 1, 0))
        return _silu(cw_ref[0:1, :] * xm1 + cw_ref[1:2, :] * x + cw_ref[2:3, :] * xp1)

    def l2n(x):
        return x * lax.rsqrt(jnp.sum(x * x, axis=-1, keepdims=True) + 1e-6)

    q_s[...] = l2n(conv_silu(qp_ref, cwq_ref)) * (DK_C ** -0.5)
    k_s[...] = l2n(conv_silu(kp_ref, cwk_ref))
    v_s[...] = conv_silu(vp_ref, cwv_ref)

    gates = g_ref[...]
    local = row & (CHUNK - 1)
    pre = gates
    suf = gates
    s = 1
    while s < CHUNK:
        pre = pre + jnp.where(local >= s, pltpu.roll(pre, s, 0), 0.0)
        suf = suf + jnp.where(local < CHUNK - s, pltpu.roll(suf, n - s, 0), 0.0)
        s *= 2
    tot = pre + suf - gates

    def col(x, idx):
        picked = jnp.sum(jnp.where(lane == idx, x, 0.0), axis=1, keepdims=True)
        return jnp.broadcast_to(picked, (n, LANES))

    for d in range(2):
        beta_b[d] = col(gates, d * H_C + h)
        gc_b[d] = col(pre if d == 0 else suf, 2 * H_C + d * H_C + h)
        gtot_b[d] = col(tot, 2 * H_C + d * H_C + h)

    ii = lax.broadcasted_iota(jnp.int32, (CHUNK, CHUNK), 0)
    jj = lax.broadcasted_iota(jnp.int32, (CHUNK, CHUNK), 1)
    same_blk = (ii >> 4) == (jj >> 4)
    eye =(lax.broadcasted_iota(jnp.int32, (LANES, LANES), 0)
           == lax.broadcasted_iota(jnp.int32, (LANES, LANES), 1)).astype(BF16)

    def prep(c, carry):
        sl = pl.ds(pl.multiple_of(c * CHUNK, CHUNK), CHUNK)
        kc = k_s[sl, :]
        qc = q_s[sl, :]
        vc = v_s[sl, :]
        kc_b = kc.astype(BF16)
        for d in range(2):
            beta = beta_b[d, sl, :]
            gc = gc_b[d, sl, :]
            gt = gtot_b[d, sl, :]
            kb = kc * beta
            eg = jnp.exp(gc)
            a = lax.dot_general(jnp.concatenate([kb, qc], axis=0).astype(BF16), kc_b, _NT,
                                preferred_element_type=F32)
            gci = gc[:, :CHUNK]
            gcj = jnp.sum(jnp.where(ii == jj, gci, 0.0), axis=0, keepdims=True)
            incl = (ii >= jj) if d == 0 else (ii <= jj)
            strict = (ii > jj) if d == 0 else (ii < jj)
            decay = jnp.where(incl, jnp.exp(jnp.where(incl, gci - gcj, 0.0)), 0.0)
            r = _unit_tri_inverse_residual(jnp.where(strict, a[:CHUNK] * decay, 0.0), same_blk)
            rhs = jnp.concatenate([vc * beta, kb * eg], axis=1)
            sol = rhs + _mm(r, rhs)
            u_s[d, sl, :] = sol[:, :LANES]
            w_s[d, sl, :] = sol[:, LANES:]
            qg_s[d, sl, :] = qc * eg
            intra_s[d, sl, :] = jnp.where(incl, a[CHUNK:] * decay, 0.0)
            kd = (kc * jnp.exp(gt - gc)).astype(BF16)
            kdt_s[d, pl.ds(pl.multiple_of(c * LANES, LANES), LANES), :] = lax.dot_general(
                eye, kd, _NT, preferred_element_type=F32)
        return carry

    lax.fori_loop(0, nc, prep, 0)

    def advance(d, c, state):
        off = pl.multiple_of(c * CHUNK, CHUNK)
        sl = pl.ds(off, CHUNK)
        st_b = state.astype(BF16)
        ws = jnp.dot(w_s[d, sl, :].astype(BF16), st_b, preferred_element_type=F32)
        vnew = u_s[d, sl, :] - ws
        vn_b = vnew.astype(BF16)
        o_s[d, sl, :] = (jnp.dot(qg_s[d, sl, :].astype(BF16), st_b, preferred_element_type=F32)
                         + jnp.dot(intra_s[d, sl, :].astype(BF16), vn_b, preferred_element_type=F32))
        eg_tot = jnp.exp(gtot_b[d, pl.ds(off, 1), :])
        kdt = kdt_s[d, pl.ds(pl.multiple_of(c * LANES, LANES), LANES), :].astype(BF16)
        return state * eg_tot + jnp.dot(kdt, vn_b, preferred_element_type=F32)

    def step(i, carry):
        sf, sb = carry
        return advance(0, i, sf), advance(1, nc - 1 - i, sb)

    if has_s0:
        init = (s0_ref[0], s0_ref[1])
    else:
        init = (jnp.zeros((DK_C, LANES), F32), jnp.zeros((DK_C, LANES), F32))
    sf, sb = lax.fori_loop(0, nc, step, init)

    o = o_s[0] + o_s[1]
    og_ref[...] = (_rms(o, onw_ref[...]) * _silu(z_ref[...])).astype(BF16)
    if want_state:
        sf_ref[0] = sf
        sf_ref[1] = sb


def _gdn(qkv, conv_w, gates, z, onorm_w, latent, s0=None):
    n = N_LAT if latent else N_CTX
    nb = N_LAT_B if latent else N_CTX_B
    nc = n // CHUNK
    want_state = not latent
    col = lambda off: pl.BlockSpec((n, LANES), lambda b, h, off=off: (b, off + h))
    cw = lambda off: pl.BlockSpec((3, LANES), lambda b, h, off=off: (0, off + h))
    state_spec = pl.BlockSpec((None, 2, None, DK_C, LANES), lambda b, h: (b, 0, h, 0, 0))
    in_specs = [col(0), col(H_C), col(2 * H_C), cw(0), cw(H_C), cw(2 * H_C),
                pl.BlockSpec((n, LANES), lambda b, h: (b, 0)),
                pl.BlockSpec((n, LANES), lambda b, h: (b, h)),
                pl.BlockSpec((1, LANES), lambda b, h: (0, 0))]
    args = [qkv, qkv, qkv, conv_w, conv_w, conv_w, gates, z, onorm_w.reshape(1, LANES)]
    if s0 is not None:
        in_specs.append(state_spec)
        args.append(s0)
    out_specs = [pl.BlockSpec((n, LANES), lambda b, h: (b, h))]
    out_shape = [jax.ShapeDtypeStruct((nb * n, D), BF16)]
    if want_state:
        out_specs.append(state_spec)
        out_shape.append(jax.ShapeDtypeStruct((nb, 2, H_C, DK_C, LANES), F32))
    seq = lambda: pltpu.VMEM((n, LANES), F32)
    both = lambda: pltpu.VMEM((2, n, LANES), F32)
    scratch = [seq(), seq(), seq(), both(), both(), both(), both(), both(), both(),
               pltpu.VMEM((2, nc * LANES, CHUNK), F32), pltpu.VMEM((2, n, CHUNK), F32), both()]
    return pl.pallas_call(
        functools.partial(_gdn_kernel, n=n, has_s0=s0 is not None, want_state=want_state),
        grid=(nb, H_C),
        in_specs=in_specs,
        out_specs=out_specs,
        out_shape=out_shape,
        scratch_shapes=scratch,
        compiler_params=_params(("parallel", "arbitrary")),
        name=f"gdn_{'lat' if latent else 'ctx'}",
    )(*args)


def _rope_tables():
    rows = N_LAT // GRID_W
    row = jnp.repeat(jnp.arange(rows), GRID_W).astype(F32)
    colp = jnp.tile(jnp.arange(GRID_W), rows).astype(F32)
    quarter = DH // 4
    inv = ROPE_BASE ** (-jnp.arange(quarter, dtype=F32) / quarter)
    ar, ac = row[:, None] * inv, colp[:, None] * inv
    cos = jnp.concatenate([jnp.cos(ar)] * 2 + [jnp.cos(ac)] * 2, axis=-1)
    sin = jnp.concatenate([-jnp.sin(ar), jnp.sin(ar), -jnp.sin(ac), jnp.sin(ac)], axis=-1)
    return jnp.tile(cos, (1, 2)), jnp.tile(sin, (1, 2))


def kernel(x_prompt, x_sample, cache_l0_k, cache_l0_v, cache_l1_k, cache_l1_v, state_l2, cache_l3_k, cache_l3_v, c, c_ctx, l0_norm_w, l0_mod_w, l0_mod_b, l0_in_w, l0_out_w, l0_sink, l1_norm_w, l1_mod_w, l1_mod_b, l1_in_w, l1_out_w, l1_lambda_q1, l1_lambda_k1, l1_lambda_q2, l1_lambda_k2, l1_subln_w, l2_norm_w, l2_mod_w, l2_mod_b, l2_in_w, l2_out_w, l2_conv_w, l2_a_log, l2_dt_bias, l2_onorm_w, l3_norm_w, l3_mod_w, l3_mod_b, l3_in_w, l3_out_w, l3_sink, final_norm_w):
    xc = x_prompt.reshape(N_CTX_B * N_CTX, D)
    xl = x_sample.reshape(N_LAT_B * N_LAT, D)
    cond = jnp.concatenate([c, c_ctx[None, :], jnp.zeros((3, D), F32)], axis=0)
    mods = _modulation(cond, (l0_mod_w, l1_mod_w, l2_mod_w, l3_mod_w), (l0_mod_b, l1_mod_b, l2_mod_b, l3_mod_b))
    rope_tabs = _rope_tables()
    kvw = HKV_A * DH
    new_state = []

    def layer_a(xc, xl, mod, norm_w, in_w, out_w, sink, cache_k, cache_v, final_w=None):
        w = in_w.astype(BF16)
        outs = [(D, BF16), (kvw, F32), (kvw, F32), (D, F32)]
        qc, kc, vc, zc = _inproj("a", xc, norm_w, mod, w, [], outs, latent=False)
        ql, kl, vl, zl = _inproj("a", xl, norm_w, mod, w, [], outs, latent=True, rope_tabs=rope_tabs)
        ogc = _attn_a(sink, qc, kc, vc, zc)
        ogl = _attn_a(sink, ql, kl, vl, zl,
                      cache=(cache_k.reshape(N_LAT_B * PAST, kvw), cache_v.reshape(N_LAT_B * PAST, kvw)))
        wo = out_w.astype(BF16)
        xc = _outproj(ogc, wo, xc, mod, latent=False, final_w=final_w)
        xl = _outproj(ogl, wo, xl, mod, latent=True, final_w=final_w)
        new_state.extend([kc.reshape(N_CTX_B, N_CTX, HKV_A, DH), vc.reshape(N_CTX_B, N_CTX, HKV_A, DH)])
        return xc, xl

    xc, xl = layer_a(xc, xl, mods[0], l0_norm_w, l0_in_w, l0_out_w, l0_sink, cache_l0_k, cache_l0_v)

    lam_init = 0.8 - 0.6 * math.exp(-0.3 * 1)
    w = l1_in_w.astype(BF16)
    outs = [(D, BF16), (D, F32), (D, F32), (D, F32)]
    qc, kc, vc, zc = _inproj("b", xc, l1_norm_w, mods[1], w, [], outs, latent=False)
    ql, kl, vl, zl = _inproj("b", xl, l1_norm_w, mods[1], w, [], outs, latent=True, rope_tabs=rope_tabs)
    lams = (l1_lambda_q1, l1_lambda_k1, l1_lambda_q2, l1_lambda_k2)
    ogc = _attn_b(lams, l1_subln_w, qc, kc, vc, zc, lam_init)
    ogl = _attn_b(lams, l1_subln_w, ql, kl, vl, zl, lam_init,
                  cache=(cache_l1_k.reshape(N_LAT_B * PAST, D), cache_l1_v.reshape(N_LAT_B * PAST, D)))
    wo = l1_out_w.astype(BF16)
    xc = _outproj(ogc, wo, xc, mods[1], latent=False)
    xl = _outproj(ogl, wo, xl, mods[1], latent=True)
    new_state.extend([kc.reshape(N_CTX_B, N_CTX, H_C, 2, DH), vc.reshape(N_CTX_B, N_CTX, H_C, 2 * DH)])

    w = l2_in_w[:, :4 * D].astype(BF16)
    wg = jnp.pad(l2_in_w[:, 4 * D:], ((0, 0), (0, LANES - 4 * H_C))).astype(BF16)
    lane_pad = lambda p: jnp.pad(p.reshape(1, 2 * H_C), ((0, 0), (2 * H_C, LANES - 4 * H_C)))
    extra = [wg, lane_pad(l2_a_log), lane_pad(l2_dt_bias)]
    outs = [(3 * D, F32), (D, F32), (LANES, F32)]
    qkvc, zc, gc = _inproj("c", xc, l2_norm_w, mods[2], w, extra, outs, latent=False)
    qkvl, zl, gl = _inproj("c", xl, l2_norm_w, mods[2], w, extra, outs, latent=True)
    ogc, st_new = _gdn(qkvc, l2_conv_w, gc, zc, l2_onorm_w, latent=False)
    (ogl,) = _gdn(qkvl, l2_conv_w, gl, zl, l2_onorm_w, latent=True, s0=state_l2)
    wo = l2_out_w.astype(BF16)
    xc = _outproj(ogc, wo, xc, mods[2], latent=False)
    xl = _outproj(ogl, wo, xl, mods[2], latent=True)
    new_state.append(st_new)

    yc, yl = layer_a(xc, xl, mods[3], l3_norm_w, l3_in_w, l3_out_w, l3_sink, cache_l3_k, cache_l3_v,
                     final_w=final_norm_w)
    return (yc.reshape(N_CTX_B, N_CTX, D), yl.reshape(N_LAT_B, N_LAT, D), *new_state)
```

```python
import functools
import math

import jax
import jax.numpy as jnp
from jax import lax
from jax.experimental import pallas as pl
from jax.experimental.pallas import tpu as pltpu

F32 = jnp.float32
BF16 = jnp.bfloat16

D = 1024
N_CTX_B, N_CTX = 16, 256
N_LAT_B, N_LAT = 4, 1024
PAST = 512
GRID_W = 64
ROPE_BASE = 10000.0
NORM_EPS = 1e-6
DH = 64
HKV_A = 4
WINDOW = 128
H_C = 8
DK_C = 128
CHUNK = 64
LANES = 128
TM = 256
VMEM_LIMIT = 48 * 1024 * 1024

_NT = (((1,), (1,)), ((), ()))


def _sigmoid(x):
    return 1.0 / (1.0 + jnp.exp(-x))


def _silu(x):
    return x * _sigmoid(x)


def _softplus(x):
    return jnp.maximum(x, 0.0) + jnp.log1p(jnp.exp(-jnp.abs(x)))


def _rms(x, w):
    return x * lax.rsqrt(jnp.mean(x * x, axis=-1, keepdims=True) + NORM_EPS) * w


def _params(sem):
    return pltpu.CompilerParams(dimension_semantics=sem, vmem_limit_bytes=VMEM_LIMIT)


def _mod_kernel(cond_ref, w0, w1, w2, w3, b0, b1, b2, b3, o0, o1, o2, o3):
    a = _silu(cond_ref[...]).astype(BF16)
    for w, b, o in ((w0, b0, o0), (w1, b1, o1), (w2, b2, o2), (w3, b3, o3)):
        o[...] = jnp.dot(a, w[...].astype(BF16), preferred_element_type=F32) + b[...]


def _modulation(cond, mod_ws, mod_bs):
    tn = 512
    wspec = pl.BlockSpec((D, tn), lambda j: (0, j))
    bspec = pl.BlockSpec((1, tn), lambda j: (0, j))
    ospec = pl.BlockSpec((8, tn), lambda j: (0, j))
    outs = pl.pallas_call(
        _mod_kernel,
        grid=(3 * D // tn,),
        in_specs=[pl.BlockSpec((8, D), lambda j: (0, 0))] + [wspec] * 4 + [bspec] * 4,
        out_specs=[ospec] * 4,
        out_shape=[jax.ShapeDtypeStruct((8, 3 * D), F32)] * 4,
        compiler_params=_params(("arbitrary",)),
        name="adaln_mod",
    )(cond, *mod_ws, *[b.reshape(1, 3 * D) for b in mod_bs])
    return [o.reshape(8, 1, 3 * D) for o in outs]


def _adaln_h(x_ref, nw_ref, mod_ref):
    y = _rms(x_ref[...], nw_ref[...])
    return (y * (1.0 + mod_ref[:, D:2 * D]) + mod_ref[:, 0:D]).astype(BF16)


def _proj(h, w_ref, lo, hi):
    return jnp.dot(h, w_ref[:, lo:hi], preferred_element_type=F32)


def _rope(x, cos_ref, sin_ref):
    width = x.shape[1]
    lane = lax.broadcasted_iota(jnp.int32, x.shape, 1)
    partner = jnp.where((lane & 31) < 16, pltpu.roll(x, width - 16, 1), pltpu.roll(x, 16, 1))
    reps = width // LANES
    return x * jnp.tile(cos_ref[...], (1, reps)) + partner * jnp.tile(sin_ref[...], (1, reps))


def _inproj_a_kernel(*refs, rope):
    if rope:
        x_ref, nw_ref, mod_ref, w_ref, cos_ref, sin_ref, q_ref, k_ref, v_ref, z_ref = refs
    else:
        x_ref, nw_ref, mod_ref, w_ref, q_ref, k_ref, v_ref, z_ref = refs
    h = _adaln_h(x_ref, nw_ref, mod_ref)
    for j in range(2):
        acc = _proj(h, w_ref, j * 512, (j + 1) * 512)
        if rope:
            acc = _rope(acc, cos_ref, sin_ref)
        q_ref[:, j * 512:(j + 1) * 512] = (acc * (DH ** -0.5)).astype(BF16)
    kv = _proj(h, w_ref, 1024, 1536)
    k = kv[:, :256]
    if rope:
        k = _rope(k, cos_ref, sin_ref)
    k_ref[...] = k
    v_ref[...] = kv[:, 256:]
    for j in range(2):
        z_ref[:, j * 512:(j + 1) * 512] = _proj(h, w_ref, 1536 + j * 512, 2048 + j * 512)


def _inproj_b_kernel(*refs, rope):
    if rope:
        x_ref, nw_ref, mod_ref, w_ref, cos_ref, sin_ref, q_ref, k_ref, v_ref, z_ref = refs
    else:
        x_ref, nw_ref, mod_ref, w_ref, q_ref, k_ref, v_ref, z_ref = refs
    h = _adaln_h(x_ref, nw_ref, mod_ref)
    for j in range(2):
        sl = slice(j * 512, (j + 1) * 512)
        q = _proj(h, w_ref, j * 512, (j + 1) * 512)
        k = _proj(h, w_ref, D + j * 512, D + (j + 1) * 512)
        if rope:
            q = _rope(q, cos_ref, sin_ref)
            k = _rope(k, cos_ref, sin_ref)
        q_ref[:, sl] = (q * (DH ** -0.5)).astype(BF16)
        k_ref[:, sl] = k
        v_ref[:, sl] = _proj(h, w_ref, 2 * D + j * 512, 2 * D + (j + 1) * 512)
        z_ref[:, sl] = _proj(h, w_ref, 3 * D + j * 512, 3 * D + (j + 1) * 512)


def _inproj_c_kernel(x_ref, nw_ref, mod_ref, w_ref, wg_ref, alog_ref, dtb_ref, qkv_ref, z_ref, g_ref):
    h = _adaln_h(x_ref, nw_ref, mod_ref)
    for j in range(6):
        qkv_ref[:, j * 512:(j + 1) * 512] = _proj(h, w_ref, j * 512, (j + 1) * 512)
    for j in range(2):
        z_ref[:, j * 512:(j + 1) * 512] = _proj(h, w_ref, 3 * D + j * 512, 3 * D + (j + 1) * 512)
    acc = jnp.dot(h, wg_ref[...], preferred_element_type=F32)
    lane = lax.broadcasted_iota(jnp.int32, acc.shape, 1)
    g = -jnp.exp(alog_ref[...]) * _softplus(acc + dtb_ref[...])
    g_ref[...] = jnp.where(lane < 2 * H_C, _sigmoid(acc), g)


def _inproj(kind, x, norm_w, mod, w, extra_in, outs, latent, rope_tabs=None):
    rows = x.shape[0]
    per_seq = (N_LAT if latent else N_CTX) // TM
    mod_row = (lambda i: (i // per_seq, 0, 0)) if latent else (lambda i: (4, 0, 0))
    in_specs = [
        pl.BlockSpec((TM, D), lambda i: (i, 0)),
        pl.BlockSpec((1, D), lambda i: (0, 0)),
        pl.BlockSpec((None, 1, 3 * D), mod_row),
        pl.BlockSpec(w.shape, lambda i: (0, 0)),
    ]
    args = [x, norm_w.reshape(1, D), mod, w]
    for e in extra_in:
        in_specs.append(pl.BlockSpec(e.shape, lambda i: (0, 0)))
        args.append(e)
    rope = rope_tabs is not None
    if rope:
        for t in rope_tabs:
            in_specs.append(pl.BlockSpec((TM, LANES), lambda i: (i % per_seq, 0)))
            args.append(t)
    if kind == "a":
        body = functools.partial(_inproj_a_kernel, rope=rope)
    elif kind == "b":
        body = functools.partial(_inproj_b_kernel, rope=rope)
    else:
        body = _inproj_c_kernel
    return pl.pallas_call(
        body,
        grid=(rows // TM,),
        in_specs=in_specs,
        out_specs=[pl.BlockSpec((TM, wd), lambda i: (i, 0)) for wd, _ in outs],
        out_shape=[jax.ShapeDtypeStruct((rows, wd), dt) for wd, dt in outs],
        compiler_params=_params(("parallel",)),
        name=f"inproj_{kind}_{'lat' if latent else 'ctx'}",
    )(*args)


def _outproj_kernel(og_ref, w_ref, x_ref, mod_ref, *rest, final):
    out = jnp.dot(og_ref[...], w_ref[...], preferred_element_type=F32)
    xn = x_ref[...] + mod_ref[:, 2 * D:3 * D] * out
    if final:
        fw_ref, y_ref = rest
        y_ref[...] = _rms(xn, fw_ref[...])
    else:
        (o_ref,) = rest
        o_ref[...] = xn


def _outproj(og, w, x, mod, latent, final_w=None):
    rows = x.shape[0]
    per_seq = (N_LAT if latent else N_CTX) // TM
    mod_row = (lambda i: (i // per_seq, 0, 0)) if latent else (lambda i: (4, 0, 0))
    in_specs = [
        pl.BlockSpec((TM, D), lambda i: (i, 0)),
        pl.BlockSpec((D, D), lambda i: (0, 0)),
        pl.BlockSpec((TM, D), lambda i: (i, 0)),
        pl.BlockSpec((None, 1, 3 * D), mod_row),
    ]
    args = [og, w, x, mod]
    if final_w is not None:
        in_specs.append(pl.BlockSpec((1, D), lambda i: (0, 0)))
        args.append(final_w.reshape(1, D))
    return pl.pallas_call(
        functools.partial(_outproj_kernel, final=final_w is not None),
        grid=(rows // TM,),
        in_specs=in_specs,
        out_specs=pl.BlockSpec((TM, D), lambda i: (i, 0)),
        out_shape=jax.ShapeDtypeStruct((rows, D), F32),
        compiler_params=_params(("parallel",)),
        name=f"outproj_{'lat' if latent else 'ctx'}",
    )(*args)


def _lane_lo(shape):
    return lax.broadcasted_iota(jnp.int32, shape, 1) < DH


def _sink_head(qm, pieces, sink):
    scores = []
    m = None
    for kx, _, valid in pieces:
        s = lax.dot_general(qm, kx, _NT, preferred_element_type=F32)
        if valid is not None:
            s = jnp.where(valid, s, -jnp.inf)
        scores.append(s)
        sm = jnp.max(s, axis=-1, keepdims=True)
        m = sm if m is None else jnp.maximum(m, sm)
    m = jnp.maximum(m, sink)
    denom = jnp.exp(sink - m)
    o = None
    for (_, vx, _), s in zip(pieces, scores):
        e = jnp.exp(s - m)
        denom = denom + jnp.sum(e, axis=-1, keepdims=True)
        po = jnp.dot(e.astype(BF16), vx, preferred_element_type=F32)
        o = po if o is None else o + po
    return o * (1.0 / denom)


def _gqa_block(sink_ref, q_ref, z_ref, o_ref, kv_pieces):
    lo = _lane_lo((1, LANES))
    for j in range(HKV_A // 2):
        variants = []
        for k2, v2, valid in kv_pieces(j):
            variants.append((
                (k2.astype(BF16), pltpu.roll(k2, DH, 1).astype(BF16)),
                (v2.astype(BF16), pltpu.roll(v2, DH, 1).astype(BF16)),
                valid))
        for hh in range(2):
            h = 2 * j + hh
            for pq in range(2):
                blk = 2 * h + pq
                cols = slice(blk * LANES, (blk + 1) * LANES)
                q2 = q_ref[:, cols]
                outs = []
                for a in range(2):
                    sw = 0 if hh == a else 1
                    half = lo if a == 0 else jnp.logical_not(lo)
                    qm = jnp.where(half, q2, jnp.zeros_like(q2))
                    pieces = [(kk[sw], vv[sw], valid) for kk, vv, valid in variants]
                    outs.append(_sink_head(qm, pieces, sink_ref[4 * h + 2 * pq + a]))
                o2 = jnp.where(lo, outs[0], outs[1])
                o_ref[:, cols] = (o2 * _silu(z_ref[:, cols])).astype(BF16)


def _attn_a_ctx_kernel(sink_ref, q_ref, k_ref, v_ref, z_ref, o_ref):
    def kv_pieces(j):
        cols = slice(j * LANES, (j + 1) * LANES)
        return [(k_ref[:, cols], v_ref[:, cols], None)]
    _gqa_block(sink_ref, q_ref, z_ref, o_ref, kv_pieces)


def _attn_a_lat_kernel(sink_ref, q_ref, k_ref, v_ref, kc_ref, vc_ref, z_ref, o_ref):
    n = pl.program_id(1)
    span = 3 * WINDOW
    start = pl.multiple_of(jnp.clip((n - 1) * WINDOW, 0, N_LAT - span), WINDOW)
    qi = n * WINDOW + lax.broadcasted_iota(jnp.int32, (WINDOW, span), 0)
    kj = start + lax.broadcasted_iota(jnp.int32, (WINDOW, span), 1)
    valid = jnp.abs(kj - qi) <= WINDOW

    def kv_pieces(j):
        cols = slice(j * LANES, (j + 1) * LANES)
        return [(k_ref[pl.ds(start, span), cols], v_ref[pl.ds(start, span), cols], valid),
                (kc_ref[:, cols], vc_ref[:, cols], None)]
    _gqa_block(sink_ref, q_ref, z_ref, o_ref, kv_pieces)


def _attn_a(sink, q, k, v, z, cache=None):
    smem = pl.BlockSpec(memory_space=pltpu.SMEM)
    kvw = HKV_A * DH
    if cache is None:
        return pl.pallas_call(
            _attn_a_ctx_kernel,
            grid=(N_CTX_B,),
            in_specs=[smem,
                      pl.BlockSpec((N_CTX, D), lambda b: (b, 0)),
                      pl.BlockSpec((N_CTX, kvw), lambda b: (b, 0)),
                      pl.BlockSpec((N_CTX, kvw), lambda b: (b, 0)),
                      pl.BlockSpec((N_CTX, D), lambda b: (b, 0))],
            out_specs=pl.BlockSpec((N_CTX, D), lambda b: (b, 0)),
            out_shape=jax.ShapeDtypeStruct((N_CTX_B * N_CTX, D), BF16),
            compiler_params=_params(("parallel",)),
            name="attn_a_ctx",
        )(sink, q, k, v, z)
    kc, vc = cache
    nq = N_LAT // WINDOW
    return pl.pallas_call(
        _attn_a_lat_kernel,
        grid=(N_LAT_B, nq),
        in_specs=[smem,
                  pl.BlockSpec((WINDOW, D), lambda b, n: (b * nq + n, 0)),
                  pl.BlockSpec((N_LAT, kvw), lambda b, n: (b, 0)),
                  pl.BlockSpec((N_LAT, kvw), lambda b, n: (b, 0)),
                  pl.BlockSpec((PAST, kvw), lambda b, n: (b, 0)),
                  pl.BlockSpec((PAST, kvw), lambda b, n: (b, 0)),
                  pl.BlockSpec((WINDOW, D), lambda b, n: (b * nq + n, 0))],
        out_specs=pl.BlockSpec((WINDOW, D), lambda b, n: (b * nq + n, 0)),
        out_shape=jax.ShapeDtypeStruct((N_LAT_B * N_LAT, D), BF16),
        compiler_params=_params(("parallel", "arbitrary")),
        name="attn_a_lat",
    )(sink, q, k, v, kc, vc, z)


def _diff_attn_block(lam_refs, subln_ref, q_ref, z_ref, o_ref, kv_pieces, lam_init):
    lq1, lk1, lq2, lk2 = lam_refs
    dot_exp = lambda a, c: jnp.exp(jnp.sum(a[...] * c[...], axis=-1, keepdims=True))
    lam = dot_exp(lq1, lk1) - dot_exp(lq2, lk2) + lam_init
    lo = _lane_lo((1, LANES))
    for h in range(D // LANES):
        cols = slice(h * LANES, (h + 1) * LANES)
        q2 = q_ref[:, cols]
        pieces = kv_pieces(h)
        probs = []
        for c in range(2):
            half = lo if c == 0 else jnp.logical_not(lo)
            qm = jnp.where(half, q2, jnp.zeros_like(q2))
            scores = [lax.dot_general(qm, k2, _NT, preferred_element_type=F32) for k2, _ in pieces]
            m = None
            for s in scores:
                sm = jnp.max(s, axis=-1, keepdims=True)
                m = sm if m is None else jnp.maximum(m, sm)
            es = [jnp.exp(s - m) for s in scores]
            denom = None
            for e in es:
                se = jnp.sum(e, axis=-1, keepdims=True)
                denom = se if denom is None else denom + se
            inv = 1.0 / denom
            probs.append([e * inv for e in es])
        o = None
        for i, (_, vx) in enumerate(pieces):
            a = (probs[0][i] - lam * probs[1][i]).astype(BF16)
            po = jnp.dot(a, vx, preferred_element_type=F32)
            o = po if o is None else o + po
        o = _rms(o, subln_ref[...]) * (1.0 - lam_init)
        o_ref[:, cols] = (o * _silu(z_ref[:, cols])).astype(BF16)


def _attn_b_ctx_kernel(lq1, lk1, lq2, lk2, subln_ref, q_ref, k_ref, v_ref, z_ref, o_ref, *, lam_init):
    def kv_pieces(h):
        cols = slice(h * LANES, (h + 1) * LANES)
        return [(k_ref[:, cols].astype(BF16), v_ref[:, cols].astype(BF16))]
    _diff_attn_block((lq1, lk1, lq2, lk2), subln_ref, q_ref, z_ref, o_ref, kv_pieces, lam_init)


def _attn_b_lat_kernel(lq1, lk1, lq2, lk2, subln_ref, q_ref, k_ref, v_ref, kc_ref, vc_ref, z_ref, o_ref,
                       *, lam_init):
    def kv_pieces(h):
        cols = slice(h * LANES, (h + 1) * LANES)
        return [(k_ref[:, cols].astype(BF16), v_ref[:, cols].astype(BF16)),
                (kc_ref[:, cols].astype(BF16), vc_ref[:, cols].astype(BF16))]
    _diff_attn_block((lq1, lk1, lq2, lk2), subln_ref, q_ref, z_ref, o_ref, kv_pieces, lam_init)


def _attn_b(lams, subln, q, k, v, z, lam_init, cache=None):
    small = [pl.BlockSpec((1, DH), lambda *_: (0, 0))] * 4 + [pl.BlockSpec((1, 2 * DH), lambda *_: (0, 0))]
    small_args = [l.reshape(1, DH) for l in lams] + [subln.reshape(1, 2 * DH)]
    if cache is None:
        blk = pl.BlockSpec((N_CTX, D), lambda b: (b, 0))
        return pl.pallas_call(
            functools.partial(_attn_b_ctx_kernel, lam_init=lam_init),
            grid=(N_CTX_B,),
            in_specs=small + [blk] * 4,
            out_specs=blk,
            out_shape=jax.ShapeDtypeStruct((N_CTX_B * N_CTX, D), BF16),
            compiler_params=_params(("parallel",)),
            name="attn_b_ctx",
        )(*small_args, q, k, v, z)
    kc, vc = cache
    tq = 256
    nq = N_LAT // tq
    qblk = pl.BlockSpec((tq, D), lambda b, n: (b * nq + n, 0))
    return pl.pallas_call(
        functools.partial(_attn_b_lat_kernel, lam_init=lam_init),
        grid=(N_LAT_B, nq),
        in_specs=small + [qblk,
                          pl.BlockSpec((N_LAT, D), lambda b, n: (b, 0)),
                          pl.BlockSpec((N_LAT, D), lambda b, n: (b, 0)),
                          pl.BlockSpec((PAST, D), lambda b, n: (b, 0)),
                          pl.BlockSpec((PAST, D), lambda b, n: (b, 0)),
                          qblk],
        out_specs=qblk,
        out_shape=jax.ShapeDtypeStruct((N_LAT_B * N_LAT, D), BF16),
        compiler_params=_params(("parallel", "arbitrary")),
        name="attn_b_lat",
    )(*small_args, q, k, v, kc, vc, z)


def _mm(a, b):
    return jnp.dot(a.astype(BF16), b.astype(BF16), preferred_element_type=F32)


def _each(fn, *lists):
    return [fn(*xs) for xs in zip(*lists)]


def _unit_tri_inverse_residuals(ls, same_blk):
    rd = [jnp.where(same_blk, -l, 0.0) for l in ls]
    lo = [jnp.where(same_blk, 0.0, l) for l in ls]
    pk = _each(_mm, rd, rd)
    for it in range(3):
        t = _each(_mm, rd, pk)
        nxt = _each(_mm, pk, pk) if it < 2 else None
        rd = _each(lambda r, p, x: r + p + x, rd, pk, t)
        pk = nxt
    m = _each(lambda x, r: x + _mm(r, x), lo, rd)
    m2 = _each(_mm, m, m)
    q = _each(lambda a, a2: a2 - a - _mm(a, a2), m, m2)
    return _each(lambda a, r: a + r + _mm(a, r), q, rd)


def _gdn_kernel(*refs, n, has_s0, want_state):
    it = iter(refs)
    qp_ref, kp_ref, vp_ref, cwq_ref, cwk_ref, cwv_ref, g_ref, z_ref, onw_ref = [next(it) for _ in range(9)]
    s0_ref = next(it) if has_s0 else None
    og_ref = next(it)
    sf_ref = next(it) if want_state else None
    q_s, k_s, v_s, beta_b, gc_b, gtot_b, u_s, wq_s, intra_s, bb_s, ab_s, st_s = it
    nc = n // CHUNK
    h = pl.program_id(1)

    row = lax.broadcasted_iota(jnp.int32, (n, LANES), 0)
    lane = lax.broadcasted_iota(jnp.int32, (n, LANES), 1)

    def conv_silu(p_ref, cw_ref):
        x = p_ref[...]
        xm1 = jnp.where(row == 0, 0.0, pltpu.roll(x, 1, 0))
        xp1 = jnp.where(row == n - 1, 0.0, pltpu.roll(x, n - 1, 0))
        return _silu(cw_ref[0:1, :] * xm1 + cw_ref[1:2, :] * x + cw_ref[2:3, :] * xp1)

    def l2n(x):
        return x * lax.rsqrt(jnp.sum(x * x, axis=-1, keepdims=True) + 1e-6)

    q_s[...] = l2n(conv_silu(qp_ref, cwq_ref)) * (DK_C ** -0.5)
    k_s[...] = l2n(conv_silu(kp_ref, cwk_ref))
    v_s[...] = conv_silu(vp_ref, cwv_ref)

    gates = g_ref[...]
    local = row & (CHUNK - 1)
    pre = gates
    suf = gates
    s = 1
    while s < CHUNK:
        pre = pre + jnp.where(local >= s, pltpu.roll(pre, s, 0), 0.0)
        suf = suf + jnp.where(local < CHUNK - s, pltpu.roll(suf, n - s, 0), 0.0)
        s *= 2
    tot = pre + suf - gates

    def col(x, idx):
        picked = jnp.sum(jnp.where(lane == idx, x, 0.0), axis=1, keepdims=True)
        return jnp.broadcast_to(picked, (n, LANES))

    for d in range(2):
        beta_b[d] = col(gates, d * H_C + h)
        gc_b[d] = col(pre if d == 0 else suf, 2 * H_C + d * H_C + h)
        gtot_b[d] = col(tot, 2 * H_C + d * H_C + h)

    ii = lax.broadcasted_iota(jnp.int32, (CHUNK, CHUNK), 0)
    jj = lax.broadcasted_iota(jnp.int32, (CHUNK, CHUNK), 1)
    same_blk = (ii >> 4) == (jj >> 4)
    eye = (lax.broadcasted_iota(jnp.int32, (LANES, LANES), 0)
           == lax.broadcasted_iota(jnp.int32, (LANES, LANES), 1)).astype(BF16)
    incl = (ii >= jj, ii <= jj)
    strict = (ii > jj, ii < jj)
    chains = [(c, d) for c in range(nc) for d in range(2)]
    rows = lambda c: pl.ds(c * CHUNK, CHUNK)
    rows2 = lambda c: pl.ds(c * LANES, LANES)

    kc = [k_s[rows(c), :] for c in range(nc)]
    qc = [q_s[rows(c), :] for c in range(nc)]
    beta = [beta_b[d, rows(c), :] for c, d in chains]
    gc = [gc_b[d, rows(c), :] for c, d in chains]
    kb = [kc[c] * b for (c, d), b in zip(chains, beta)]
    eg = [jnp.exp(g) for g in gc]
    a = [lax.dot_general(jnp.concatenate([kb[2 * c], kb[2 * c + 1], qc[c]], axis=0).astype(BF16),
                         kc[c].astype(BF16), _NT, preferred_element_type=F32) for c in range(nc)]
    kdt = [lax.dot_general(eye, (kc[c] * jnp.exp(gtot_b[d, rows(c), :] - g)).astype(BF16), _NT,
                           preferred_element_type=F32).astype(BF16)
           for (c, d), g in zip(chains, gc)]
    ls = []
    for i, (c, d) in enumerate(chains):
        gci = gc[i][:, :CHUNK]
        gcj = jnp.sum(jnp.where(ii == jj, gci, 0.0), axis=0, keepdims=True)
        decay = jnp.where(incl[d], jnp.exp(jnp.where(incl[d], gci - gcj, 0.0)), 0.0)
        ls.append(jnp.where(strict[d], a[c][d * CHUNK:(d + 1) * CHUNK] * decay, 0.0))
        intra_s[d, rows(c), :] = jnp.where(incl[d], a[c][2 * CHUNK:] * decay, 0.0)
        wq_s[d, pl.ds(c * LANES + CHUNK, CHUNK), :] = (qc[c] * eg[i]).astype(BF16)
    r = _unit_tri_inverse_residuals(ls, same_blk)
    rhs = [jnp.concatenate([v_s[rows(c), :] * b, x * e], axis=1)
           for (c, d), b, x, e in zip(chains, beta, kb, eg)]
    sol = _each(lambda x, y: y + _mm(x, y), r, rhs)
    for (c, d), x in zip(chains, sol):
        u_s[d, rows(c), :] = x[:, :LANES]
        wq_s[d, pl.ds(c * LANES, CHUNK), :] = x[:, LANES:].astype(BF16)
    ba = _each(lambda t, x: jnp.dot(t, x.astype(BF16), preferred_element_type=F32), kdt, sol)
    for (c, d), x in zip(chains, ba):
        bb_s[d, rows2(c), :] = x[:, :LANES]
        ab_s[d, rows2(c), :] = x[:, LANES:].astype(BF16)

    if has_s0:
        state = [s0_ref[0], s0_ref[1]]
    else:
        state = [jnp.zeros((DK_C, LANES), F32), jnp.zeros((DK_C, LANES), F32)]
    for i in range(nc):
        for d in range(2):
            c = i if d == 0 else nc - 1 - i
            st_b = state[d].astype(BF16)
            st_s[d, rows2(c), :] = st_b
            eg_tot = jnp.exp(gtot_b[d, pl.ds(c * CHUNK, 1), :])
            state[d] = (state[d] * eg_tot + bb_s[d, rows2(c), :]
                        - jnp.dot(ab_s[d, rows2(c), :], st_b, preferred_element_type=F32))

    wo = [jnp.dot(wq_s[d, rows2(c), :], st_s[d, rows2(c), :], preferred_element_type=F32)
          for c, d in chains]
    vnew = [u_s[d, rows(c), :] - x[:CHUNK] for (c, d), x in zip(chains, wo)]
    o = [x[CHUNK:] + _mm(intra_s[d, rows(c), :], vn) for (c, d), x, vn in zip(chains, wo, vnew)]
    for c in range(nc):
        oc = o[2 * c] + o[2 * c + 1]
        og_ref[rows(c), :] = (_rms(oc, onw_ref[...]) * _silu(z_ref[rows(c), :])).astype(BF16)
    if want_state:
        sf_ref[0] = state[0]
        sf_ref[1] = state[1]


def _gdn(qkv, conv_w, gates, z, onorm_w, latent, s0=None):
    n = N_LAT if latent else N_CTX
    nb = N_LAT_B if latent else N_CTX_B
    nc = n // CHUNK
    want_state = not latent
    col = lambda off: pl.BlockSpec((n, LANES), lambda b, h, off=off: (b, off + h))
    cw = lambda off: pl.BlockSpec((3, LANES), lambda b, h, off=off: (0, off + h))
    state_spec = pl.BlockSpec((None, 2, None, DK_C, LANES), lambda b, h: (b, 0, h, 0, 0))
    in_specs = [col(0), col(H_C), col(2 * H_C), cw(0), cw(H_C), cw(2 * H_C),
                pl.BlockSpec((n, LANES), lambda b, h: (b, 0)),
                pl.BlockSpec((n, LANES), lambda b, h: (b, h)),
                pl.BlockSpec((1, LANES), lambda b, h: (0, 0))]
    args = [qkv, qkv, qkv, conv_w, conv_w, conv_w, gates, z, onorm_w.reshape(1, LANES)]
    if s0 is not None:
        in_specs.append(state_spec)
        args.append(s0)
    out_specs = [pl.BlockSpec((n, LANES), lambda b, h: (b, h))]
    out_shape = [jax.ShapeDtypeStruct((nb * n, D), BF16)]
    if want_state:
        out_specs.append(state_spec)
        out_shape.append(jax.ShapeDtypeStruct((nb, 2, H_C, DK_C, LANES), F32))
    seq = lambda: pltpu.VMEM((n, LANES), F32)
    both = lambda: pltpu.VMEM((2, n, LANES), F32)
    per_chunk = lambda dt: pltpu.VMEM((2, nc * LANES, LANES), dt)
    scratch = [seq(), seq(), seq(), both(), both(), both(), both(),
               per_chunk(BF16), pltpu.VMEM((2, n, CHUNK), F32), per_chunk(F32), per_chunk(BF16), per_chunk(BF16)]
    return pl.pallas_call(
        functools.partial(_gdn_kernel, n=n, has_s0=s0 is not None, want_state=want_state),
        grid=(nb, H_C),
        in_specs=in_specs,
        out_specs=out_specs,
        out_shape=out_shape,
        scratch_shapes=scratch,
        compiler_params=_params(("parallel", "arbitrary")),
        name=f"gdn_{'lat' if latent else 'ctx'}",
    )(*args)


def _rope_tables():
    rows = N_LAT // GRID_W
    row = jnp.repeat(jnp.arange(rows), GRID_W).astype(F32)
    colp = jnp.tile(jnp.arange(GRID_W), rows).astype(F32)
    quarter = DH // 4
    inv = ROPE_BASE ** (-jnp.arange(quarter, dtype=F32) / quarter)
    ar, ac = row[:, None] * inv, colp[:, None] * inv
    cos = jnp.concatenate([jnp.cos(ar)] * 2 + [jnp.cos(ac)] * 2, axis=-1)
    sin = jnp.concatenate([-jnp.sin(ar), jnp.sin(ar), -jnp.sin(ac), jnp.sin(ac)], axis=-1)
    return jnp.tile(cos, (1, 2)), jnp.tile(sin, (1, 2))


def kernel(x_prompt, x_sample, cache_l0_k, cache_l0_v, cache_l1_k, cache_l1_v, state_l2, cache_l3_k, cache_l3_v, c, c_ctx, l0_norm_w, l0_mod_w, l0_mod_b, l0_in_w, l0_out_w, l0_sink, l1_norm_w, l1_mod_w, l1_mod_b, l1_in_w, l1_out_w, l1_lambda_q1, l1_lambda_k1, l1_lambda_q2, l1_lambda_k2, l1_subln_w, l2_norm_w, l2_mod_w, l2_mod_b, l2_in_w, l2_out_w, l2_conv_w, l2_a_log, l2_dt_bias, l2_onorm_w, l3_norm_w, l3_mod_w, l3_mod_b, l3_in_w, l3_out_w, l3_sink, final_norm_w):
    xc = x_prompt.reshape(N_CTX_B * N_CTX, D)
    xl = x_sample.reshape(N_LAT_B * N_LAT, D)
    cond = jnp.concatenate([c, c_ctx[None, :], jnp.zeros((3, D), F32)], axis=0)
    mods = _modulation(cond, (l0_mod_w, l1_mod_w, l2_mod_w, l3_mod_w), (l0_mod_b, l1_mod_b, l2_mod_b, l3_mod_b))
    rope_tabs = _rope_tables()
    kvw = HKV_A * DH
    new_state = []

    def layer_a(xc, xl, mod, norm_w, in_w, out_w, sink, cache_k, cache_v, final_w=None):
        w = in_w.astype(BF16)
        outs = [(D, BF16), (kvw, F32), (kvw, F32), (D, F32)]
        qc, kc, vc, zc = _inproj("a", xc, norm_w, mod, w, [], outs, latent=False)
        ql, kl, vl, zl = _inproj("a", xl, norm_w, mod, w, [], outs, latent=True, rope_tabs=rope_tabs)
        ogc = _attn_a(sink, qc, kc, vc, zc)
        ogl = _attn_a(sink, ql, kl, vl, zl,
                      cache=(cache_k.reshape(N_LAT_B * PAST, kvw), cache_v.reshape(N_LAT_B * PAST, kvw)))
        wo = out_w.astype(BF16)
        xc = _outproj(ogc, wo, xc, mod, latent=False, final_w=final_w)
        xl = _outproj(ogl, wo, xl, mod, latent=True, final_w=final_w)
        new_state.extend([kc.reshape(N_CTX_B, N_CTX, HKV_A, DH), vc.reshape(N_CTX_B, N_CTX, HKV_A, DH)])
        return xc, xl

    xc, xl = layer_a(xc, xl, mods[0], l0_norm_w, l0_in_w, l0_out_w, l0_sink, cache_l0_k, cache_l0_v)

    lam_init = 0.8 - 0.6 * math.exp(-0.3 * 1)
    w = l1_in_w.astype(BF16)
    outs = [(D, BF16), (D, F32), (D, F32), (D, F32)]
    qc, kc, vc, zc = _inproj("b", xc, l1_norm_w, mods[1], w, [], outs, latent=False)
    ql, kl, vl, zl = _inproj("b", xl, l1_norm_w, mods[1], w, [], outs, latent=True, rope_tabs=rope_tabs)
    lams = (l1_lambda_q1, l1_lambda_k1, l1_lambda_q2, l1_lambda_k2)
    ogc = _attn_b(lams, l1_subln_w, qc, kc, vc, zc, lam_init)
    ogl = _attn_b(lams, l1_subln_w, ql, kl, vl, zl, lam_init,
                  cache=(cache_l1_k.reshape(N_LAT_B * PAST, D), cache_l1_v.reshape(N_LAT_B * PAST, D)))
    wo = l1_out_w.astype(BF16)
    xc = _outproj(ogc, wo, xc, mods[1], latent=False)
    xl = _outproj(ogl, wo, xl, mods[1], latent=True)
    new_state.extend([kc.reshape(N_CTX_B, N_CTX, H_C, 2, DH), vc.reshape(N_CTX_B, N_CTX, H_C, 2 * DH)])

    w = l2_in_w[:, :4 * D].astype(BF16)
    wg = jnp.pad(l2_in_w[:, 4 * D:], ((0, 0), (0, LANES - 4 * H_C))).astype(BF16)
    lane_pad = lambda p: jnp.pad(p.reshape(1, 2 * H_C), ((0, 0), (2 * H_C, LANES - 4 * H_C)))
    extra = [wg, lane_pad(l2_a_log), lane_pad(l2_dt_bias)]
    outs = [(3 * D, F32), (D, F32), (LANES, F32)]
    qkvc, zc, gc = _inproj("c", xc, l2_norm_w, mods[2], w, extra, outs, latent=False)
    qkvl, zl, gl = _inproj("c", xl, l2_norm_w, mods[2], w, extra, outs, latent=True)
    ogc, st_new = _gdn(qkvc, l2_conv_w, gc, zc, l2_onorm_w, latent=False)
    (ogl,) = _gdn(qkvl, l2_conv_w, gl, zl, l2_onorm_w, latent=True, s0=state_l2)
    wo = l2_out_w.astype(BF16)
    xc = _outproj(ogc, wo, xc, mods[2], latent=False)
    xl = _outproj(ogl, wo, xl, mods[2], latent=True)
    new_state.append(st_new)

    yc, yl = layer_a(xc, xl, mods[3], l3_norm_w, l3_in_w, l3_out_w, l3_sink, cache_l3_k, cache_l3_v,
                     final_w=final_norm_w)
    return (yc.reshape(N_CTX_B, N_CTX, D), yl.reshape(N_LAT_B, N_LAT, D), *new_state)
```

```python
import functools
import math

import jax
import jax.numpy as jnp
from jax import lax
from jax.experimental import pallas as pl
from jax.experimental.pallas import tpu as pltpu

F32 = jnp.float32
BF16 = jnp.bfloat16

D = 1024
N_CTX_B, N_CTX = 16, 256
N_LAT_B, N_LAT = 4, 1024
PAST = 512
GRID_W = 64
ROPE_BASE = 10000.0
NORM_EPS = 1e-6
DH = 64
HKV_A = 4
WINDOW = 128
H_C = 8
DK_C = 128
CHUNK = 64
LANES = 128
TM = 256
VMEM_LIMIT = 48 * 1024 * 1024

_NT = (((1,), (1,)), ((), ()))


def _sigmoid(x):
    return 1.0 / (1.0 + jnp.exp(-x))


def _silu(x):
    return x * _sigmoid(x)


def _softplus(x):
    return jnp.maximum(x, 0.0) + jnp.log1p(jnp.exp(-jnp.abs(x)))


def _rms(x, w):
    return x * lax.rsqrt(jnp.mean(x * x, axis=-1, keepdims=True) + NORM_EPS) * w


def _params(sem):
    return pltpu.CompilerParams(dimension_semantics=sem, vmem_limit_bytes=VMEM_LIMIT)


def _mod_kernel(cond_ref, w0, w1, w2, w3, b0, b1, b2, b3, o0, o1, o2, o3):
    a = _silu(cond_ref[...]).astype(BF16)
    for w, b, o in ((w0, b0, o0), (w1, b1, o1), (w2, b2, o2), (w3, b3, o3)):
        o[...] = jnp.dot(a, w[...].astype(BF16), preferred_element_type=F32) + b[...]


def _modulation(cond, mod_ws, mod_bs):
    tn = 512
    wspec = pl.BlockSpec((D, tn), lambda j: (0, j))
    bspec = pl.BlockSpec((1, tn), lambda j: (0, j))
    ospec = pl.BlockSpec((8, tn), lambda j: (0, j))
    outs = pl.pallas_call(
        _mod_kernel,
        grid=(3 * D // tn,),
        in_specs=[pl.BlockSpec((8, D), lambda j: (0, 0))] + [wspec] * 4 + [bspec] * 4,
        out_specs=[ospec] * 4,
        out_shape=[jax.ShapeDtypeStruct((8, 3 * D), F32)] * 4,
        compiler_params=_params(("arbitrary",)),
        name="adaln_mod",
    )(cond, *mod_ws, *[b.reshape(1, 3 * D) for b in mod_bs])
    return [o.reshape(8, 1, 3 * D) for o in outs]


def _adaln_h(x_ref, nw_ref, mod_ref):
    y = _rms(x_ref[...], nw_ref[...])
    return (y * (1.0 + mod_ref[:, D:2 * D]) + mod_ref[:, 0:D]).astype(BF16)


def _proj(h, w_ref, lo, hi):
    return jnp.dot(h, w_ref[:, lo:hi], preferred_element_type=F32)


def _rope(x, cos_ref, sin_ref):
    width = x.shape[1]
    lane = lax.broadcasted_iota(jnp.int32, x.shape, 1)
    partner = jnp.where((lane & 31) < 16, pltpu.roll(x, width - 16, 1), pltpu.roll(x, 16, 1))
    reps = width // LANES
    return x * jnp.tile(cos_ref[...], (1, reps)) + partner * jnp.tile(sin_ref[...], (1, reps))


def _inproj_a_kernel(*refs, rope):
    if rope:
        x_ref, nw_ref, mod_ref, w_ref, cos_ref, sin_ref, q_ref, k_ref, v_ref, z_ref = refs
    else:
        x_ref, nw_ref, mod_ref, w_ref, q_ref, k_ref, v_ref, z_ref = refs
    h = _adaln_h(x_ref, nw_ref, mod_ref)
    for j in range(2):
        acc = _proj(h, w_ref, j * 512, (j + 1) * 512)
        if rope:
            acc = _rope(acc, cos_ref, sin_ref)
        q_ref[:, j * 512:(j + 1) * 512] = (acc * (DH ** -0.5)).astype(BF16)
    kv = _proj(h, w_ref, 1024, 1536)
    k = kv[:, :256]
    if rope:
        k = _rope(k, cos_ref, sin_ref)
    k_ref[...] = k
    v_ref[...] = kv[:, 256:]
    for j in range(2):
        z_ref[:, j * 512:(j + 1) * 512] = _proj(h, w_ref, 1536 + j * 512, 2048 + j * 512)


def _inproj_b_kernel(*refs, rope):
    if rope:
        x_ref, nw_ref, mod_ref, w_ref, cos_ref, sin_ref, q_ref, k_ref, v_ref, z_ref = refs
    else:
        x_ref, nw_ref, mod_ref, w_ref, q_ref, k_ref, v_ref, z_ref = refs
    h = _adaln_h(x_ref, nw_ref, mod_ref)
    for j in range(2):
        sl = slice(j * 512, (j + 1) * 512)
        q = _proj(h, w_ref, j * 512, (j + 1) * 512)
        k = _proj(h, w_ref, D + j * 512, D + (j + 1) * 512)
        if rope:
            q = _rope(q, cos_ref, sin_ref)
            k = _rope(k, cos_ref, sin_ref)
        q_ref[:, sl] = (q * (DH ** -0.5)).astype(BF16)
        k_ref[:, sl] = k
        v_ref[:, sl] = _proj(h, w_ref, 2 * D + j * 512, 2 * D + (j + 1) * 512)
        z_ref[:, sl] = _proj(h, w_ref, 3 * D + j * 512, 3 * D + (j + 1) * 512)


def _inproj_c_kernel(x_ref, nw_ref, mod_ref, w_ref, wg_ref, alog_ref, dtb_ref, qkv_ref, z_ref, g_ref):
    h = _adaln_h(x_ref, nw_ref, mod_ref)
    for j in range(6):
        qkv_ref[:, j * 512:(j + 1) * 512] = _proj(h, w_ref, j * 512, (j + 1) * 512)
    for j in range(2):
        z_ref[:, j * 512:(j + 1) * 512] = _proj(h, w_ref, 3 * D + j * 512, 3 * D + (j + 1) * 512)
    acc = jnp.dot(h, wg_ref[...], preferred_element_type=F32)
    lane = lax.broadcasted_iota(jnp.int32, acc.shape, 1)
    g = -jnp.exp(alog_ref[...]) * _softplus(acc + dtb_ref[...])
    g_ref[...] = jnp.where(lane < 2 * H_C, _sigmoid(acc), g)


def _inproj(kind, x, norm_w, mod, w, extra_in, outs, latent, rope_tabs=None):
    rows = x.shape[0]
    per_seq = (N_LAT if latent else N_CTX) // TM
    mod_row = (lambda i: (i // per_seq, 0, 0)) if latent else (lambda i: (4, 0, 0))
    in_specs = [
        pl.BlockSpec((TM, D), lambda i: (i, 0)),
        pl.BlockSpec((1, D), lambda i: (0, 0)),
        pl.BlockSpec((None, 1, 3 * D), mod_row),
        pl.BlockSpec(w.shape, lambda i: (0, 0)),
    ]
    args = [x, norm_w.reshape(1, D), mod, w]
    for e in extra_in:
        in_specs.append(pl.BlockSpec(e.shape, lambda i: (0, 0)))
        args.append(e)
    rope = rope_tabs is not None
    if rope:
        for t in rope_tabs:
            in_specs.append(pl.BlockSpec((TM, LANES), lambda i: (i % per_seq, 0)))
            args.append(t)
    if kind == "a":
        body = functools.partial(_inproj_a_kernel, rope=rope)
    elif kind == "b":
        body = functools.partial(_inproj_b_kernel, rope=rope)
    else:
        body = _inproj_c_kernel
    return pl.pallas_call(
        body,
        grid=(rows // TM,),
        in_specs=in_specs,
        out_specs=[pl.BlockSpec((TM, wd), lambda i: (i, 0)) for wd, _ in outs],
        out_shape=[jax.ShapeDtypeStruct((rows, wd), dt) for wd, dt in outs],
        compiler_params=_params(("parallel",)),
        name=f"inproj_{kind}_{'lat' if latent else 'ctx'}",
    )(*args)


def _outproj_kernel(og_ref, w_ref, x_ref, mod_ref, *rest, final):
    out = jnp.dot(og_ref[...], w_ref[...], preferred_element_type=F32)
    xn = x_ref[...] + mod_ref[:, 2 * D:3 * D] * out
    if final:
        fw_ref, y_ref = rest
        y_ref[...] = _rms(xn, fw_ref[...])
    else:
        (o_ref,) = rest
        o_ref[...] = xn


def _outproj(og, w, x, mod, latent, final_w=None):
    rows = x.shape[0]
    per_seq = (N_LAT if latent else N_CTX) // TM
    mod_row = (lambda i: (i // per_seq, 0, 0)) if latent else (lambda i: (4, 0, 0))
    in_specs = [
        pl.BlockSpec((TM, D), lambda i: (i, 0)),
        pl.BlockSpec((D, D), lambda i: (0, 0)),
        pl.BlockSpec((TM, D), lambda i: (i, 0)),
        pl.BlockSpec((None, 1, 3 * D), mod_row),
    ]
    args = [og, w, x, mod]
    if final_w is not None:
        in_specs.append(pl.BlockSpec((1, D), lambda i: (0, 0)))
        args.append(final_w.reshape(1, D))
    return pl.pallas_call(
        functools.partial(_outproj_kernel, final=final_w is not None),
        grid=(rows // TM,),
        in_specs=in_specs,
        out_specs=pl.BlockSpec((TM, D), lambda i: (i, 0)),
        out_shape=jax.ShapeDtypeStruct((rows, D), F32),
        compiler_params=_params(("parallel",)),
        name=f"outproj_{'lat' if latent else 'ctx'}",
    )(*args)


def _lane_lo(shape):
    return lax.broadcasted_iota(jnp.int32, shape, 1) < DH


Q_TILES_PER_KV_TILE = 4
_KV_ALIGNED_HEADS = (0, 4, 1, 5, 2, 6, 3, 7, 8, 12, 9, 13, 10, 14, 11, 15)


def _gqa_block(sink_ref, q_ref, z_ref, o_ref, kv_pieces, lhs_s, s_s, e_s):
    sq = q_ref.shape[0]
    lo = _lane_lo((1, LANES))
    halves = (lo, jnp.logical_not(lo))
    for j in range(HKV_A // 2):
        pieces = [(k2.astype(BF16), v2.astype(BF16), valid) for k2, v2, valid in kv_pieces(j)]
        tiles = range(Q_TILES_PER_KV_TILE * j, Q_TILES_PER_KV_TILE * (j + 1))
        blocks = [(t, a) for t in tiles for a in range(2)]
        for i, (t, a) in enumerate(blocks):
            q2 = q_ref[:, t * LANES:(t + 1) * LANES]
            lhs_s[i * sq:(i + 1) * sq, :] = jnp.where(halves[a], q2, jnp.zeros_like(q2))
        off = 0
        spans = []
        for k2, _, _ in pieces:
            s_s[:, off:off + k2.shape[0]] = lax.dot_general(lhs_s[...], k2, _NT, preferred_element_type=F32)
            spans.append(slice(off, off + k2.shape[0]))
            off += k2.shape[0]
        inv = []
        for i, (t, a) in enumerate(blocks):
            rws = slice(i * sq, (i + 1) * sq)
            sink = sink_ref[2 * t + a]
            scores = []
            for (_, _, valid), sp in zip(pieces, spans):
                s = s_s[rws, sp]
                scores.append(s if valid is None else jnp.where(valid, s, -jnp.inf))
            m = sink
            for s in scores:
                m = jnp.maximum(m, jnp.max(s, axis=-1, keepdims=True))
            denom = jnp.exp(sink - m)
            for s, sp in zip(scores, spans):
                e = jnp.exp(s - m)
                denom = denom + jnp.sum(e, axis=-1, keepdims=True)
                e_s[rws, sp] = e.astype(BF16)
            inv.append(1.0 / denom)
        pv = None
        for (_, v2, _), sp in zip(pieces, spans):
            part = jnp.dot(e_s[:, sp], v2, preferred_element_type=F32)
            pv = part if pv is None else pv + part
        for n_t, t in enumerate(tiles):
            i0, i1 = 2 * n_t, 2 * n_t + 1
            o2 = jnp.where(lo, pv[i0 * sq:(i0 + 1) * sq] * inv[i0], pv[i1 * sq:(i1 + 1) * sq] * inv[i1])
            cols = slice(t * LANES, (t + 1) * LANES)
            o_ref[:, cols] = (o2 * _silu(z_ref[:, cols])).astype(BF16)


def _gqa_scratch(sq, sk):
    stacked = 2 * Q_TILES_PER_KV_TILE * sq
    return [pltpu.VMEM((stacked, LANES), BF16), pltpu.VMEM((stacked, sk), F32), pltpu.VMEM((stacked, sk), BF16)]


def _attn_a_ctx_kernel(sink_ref, q_ref, k_ref, v_ref, z_ref, o_ref, *scratch):
    def kv_pieces(j):
        cols = slice(j * LANES, (j + 1) * LANES)
        return [(k_ref[:, cols], v_ref[:, cols], None)]
    _gqa_block(sink_ref, q_ref, z_ref, o_ref, kv_pieces, *scratch)


def _attn_a_lat_kernel(sink_ref, q_ref, k_ref, v_ref, kc_ref, vc_ref, z_ref, o_ref, *scratch):
    n = pl.program_id(1)
    span = 3 * WINDOW
    start = pl.multiple_of(jnp.clip((n - 1) * WINDOW, 0, N_LAT - span), WINDOW)
    qi = n * WINDOW + lax.broadcasted_iota(jnp.int32, (WINDOW, span), 0)
    kj = start + lax.broadcasted_iota(jnp.int32, (WINDOW, span), 1)
    valid = jnp.abs(kj - qi) <= WINDOW

    def kv_pieces(j):
        cols = slice(j * LANES, (j + 1) * LANES)
        return [(k_ref[pl.ds(start, span), cols], v_ref[pl.ds(start, span), cols], valid),
                (kc_ref[:, cols], vc_ref[:, cols], None)]
    _gqa_block(sink_ref, q_ref, z_ref, o_ref, kv_pieces, *scratch)


def _attn_a(sink, q, k, v, z, cache=None):
    smem = pl.BlockSpec(memory_space=pltpu.SMEM)
    kvw = HKV_A * DH
    if cache is None:
        return pl.pallas_call(
            _attn_a_ctx_kernel,
            grid=(N_CTX_B,),
            in_specs=[smem,
                      pl.BlockSpec((N_CTX, D), lambda b: (b, 0)),
                      pl.BlockSpec((N_CTX, kvw), lambda b: (b, 0)),
                      pl.BlockSpec((N_CTX, kvw), lambda b: (b, 0)),
                      pl.BlockSpec((N_CTX, D), lambda b: (b, 0))],
            out_specs=pl.BlockSpec((N_CTX, D), lambda b: (b, 0)),
            out_shape=jax.ShapeDtypeStruct((N_CTX_B * N_CTX, D), BF16),
            scratch_shapes=_gqa_scratch(N_CTX, N_CTX),
            compiler_params=_params(("parallel",)),
            name="attn_a_ctx",
        )(sink, q, k, v, z)
    kc, vc = cache
    nq = N_LAT // WINDOW
    return pl.pallas_call(
        _attn_a_lat_kernel,
        grid=(N_LAT_B, nq),
        in_specs=[smem,
                  pl.BlockSpec((WINDOW, D), lambda b, n: (b * nq + n, 0)),
                  pl.BlockSpec((N_LAT, kvw), lambda b, n: (b, 0)),
                  pl.BlockSpec((N_LAT, kvw), lambda b, n: (b, 0)),
                  pl.BlockSpec((PAST, kvw), lambda b, n: (b, 0)),
                  pl.BlockSpec((PAST, kvw), lambda b, n: (b, 0)),
                  pl.BlockSpec((WINDOW, D), lambda b, n: (b * nq + n, 0))],
        out_specs=pl.BlockSpec((WINDOW, D), lambda b, n: (b * nq + n, 0)),
        out_shape=jax.ShapeDtypeStruct((N_LAT_B * N_LAT, D), BF16),
        scratch_shapes=_gqa_scratch(WINDOW, 3 * WINDOW + PAST),
        compiler_params=_params(("parallel", "arbitrary")),
        name="attn_a_lat",
    )(sink, q, k, v, kc, vc, z)


def _diff_attn_block(lam_refs, subln_ref, q_ref, z_ref, o_ref, kv_pieces, lam_init):
    lq1, lk1, lq2, lk2 = lam_refs
    dot_exp = lambda a, c: jnp.exp(jnp.sum(a[...] * c[...], axis=-1, keepdims=True))
    lam = dot_exp(lq1, lk1) - dot_exp(lq2, lk2) + lam_init
    lo = _lane_lo((1, LANES))
    for h in range(D // LANES):
        cols = slice(h * LANES, (h + 1) * LANES)
        q2 = q_ref[:, cols]
        pieces = kv_pieces(h)
        exps, denoms = [], []
        for c in range(2):
            half = lo if c == 0 else jnp.logical_not(lo)
            qm = jnp.where(half, q2, jnp.zeros_like(q2))
            scores = [lax.dot_general(qm, k2, _NT, preferred_element_type=F32) for k2, _ in pieces]
            m = None
            for s in scores:
                sm = jnp.max(s, axis=-1, keepdims=True)
                m = sm if m is None else jnp.maximum(m, sm)
            es = [jnp.exp(s - m) for s in scores]
            denom = None
            for e in es:
                se = jnp.sum(e, axis=-1, keepdims=True)
                denom = se if denom is None else denom + se
            exps.append(es)
            denoms.append(denom)
        ratio = lam * denoms[0] / denoms[1]
        o = None
        for i, (_, vx) in enumerate(pieces):
            a = (exps[0][i] - ratio * exps[1][i]).astype(BF16)
            po = jnp.dot(a, vx, preferred_element_type=F32)
            o = po if o is None else o + po
        o = o * (1.0 / denoms[0])
        o = _rms(o, subln_ref[...]) * (1.0 - lam_init)
        o_ref[:, cols] = (o * _silu(z_ref[:, cols])).astype(BF16)


def _attn_b_ctx_kernel(lq1, lk1, lq2, lk2, subln_ref, q_ref, k_ref, v_ref, z_ref, o_ref, *, lam_init):
    def kv_pieces(h):
        cols = slice(h * LANES, (h + 1) * LANES)
        return [(k_ref[:, cols].astype(BF16), v_ref[:, cols].astype(BF16))]
    _diff_attn_block((lq1, lk1, lq2, lk2), subln_ref, q_ref, z_ref, o_ref, kv_pieces, lam_init)


def _attn_b_lat_kernel(lq1, lk1, lq2, lk2, subln_ref, q_ref, k_ref, v_ref, kc_ref, vc_ref, z_ref, o_ref,
                       *, lam_init):
    def kv_pieces(h):
        cols = slice(h * LANES, (h + 1) * LANES)
        return [(k_ref[:, cols].astype(BF16), v_ref[:, cols].astype(BF16)),
                (kc_ref[:, cols].astype(BF16), vc_ref[:, cols].astype(BF16))]
    _diff_attn_block((lq1, lk1, lq2, lk2), subln_ref, q_ref, z_ref, o_ref, kv_pieces, lam_init)


def _attn_b(lams, subln, q, k, v, z, lam_init, cache=None):
    small = [pl.BlockSpec((1, DH), lambda *_: (0, 0))] * 4 + [pl.BlockSpec((1, 2 * DH), lambda *_: (0, 0))]
    small_args = [l.reshape(1, DH) for l in lams] + [subln.reshape(1, 2 * DH)]
    if cache is None:
        blk = pl.BlockSpec((N_CTX, D), lambda b: (b, 0))
        return pl.pallas_call(
            functools.partial(_attn_b_ctx_kernel, lam_init=lam_init),
            grid=(N_CTX_B,),
            in_specs=small + [blk] * 4,
            out_specs=blk,
            out_shape=jax.ShapeDtypeStruct((N_CTX_B * N_CTX, D), BF16),
            compiler_params=_params(("parallel",)),
            name="attn_b_ctx",
        )(*small_args, q, k, v, z)
    kc, vc = cache
    tq = 256
    nq = N_LAT // tq
    qblk = pl.BlockSpec((tq, D), lambda b, n: (b * nq + n, 0))
    return pl.pallas_call(
        functools.partial(_attn_b_lat_kernel, lam_init=lam_init),
        grid=(N_LAT_B, nq),
        in_specs=small + [qblk,
                          pl.BlockSpec((N_LAT, D), lambda b, n: (b, 0)),
                          pl.BlockSpec((N_LAT, D), lambda b, n: (b, 0)),
                          pl.BlockSpec((PAST, D), lambda b, n: (b, 0)),
                          pl.BlockSpec((PAST, D), lambda b, n: (b, 0)),
                          qblk],
        out_specs=qblk,
        out_shape=jax.ShapeDtypeStruct((N_LAT_B * N_LAT, D), BF16),
        compiler_params=_params(("parallel", "arbitrary")),
        name="attn_b_lat",
    )(*small_args, q, k, v, kc, vc, z)


def _mm(a, b):
    return jnp.dot(a.astype(BF16), b.astype(BF16), preferred_element_type=F32)


def _each(fn, *lists):
    return [fn(*xs) for xs in zip(*lists)]


def _unit_tri_inverse_residuals(ls, same_blk):
    rd = [jnp.where(same_blk, -l, 0.0) for l in ls]
    lo = [jnp.where(same_blk, 0.0, l) for l in ls]
    pk = _each(_mm, rd, rd)
    for it in range(3):
        t = _each(_mm, rd, pk)
        nxt = _each(_mm, pk, pk) if it < 2 else None
        rd = _each(lambda r, p, x: r + p + x, rd, pk, t)
        pk = nxt
    m = _each(lambda x, r: x + _mm(r, x), lo, rd)
    m2 = _each(_mm, m, m)
    q = _each(lambda a, a2: a2 - a - _mm(a, a2), m, m2)
    return _each(lambda a, r: a + r + _mm(a, r), q, rd)


def _gdn_kernel(*refs, n, hp, has_s0, want_state):
    it = iter(refs)
    qp_ref, kp_ref, vp_ref, cwq_ref, cwk_ref, cwv_ref, g_ref, z_ref, onw_ref = [next(it) for _ in range(9)]
    s0_ref = next(it) if has_s0 else None
    og_ref = next(it)
    sf_ref = next(it) if want_state else None
    q_s, k_s, v_s, beta_b, gc_b, gtot_b, u_s, wq_s, intra_s, bb_s, ab_s, st_s = it
    nc = n // CHUNK
    head0 = pl.program_id(1) * hp

    row = lax.broadcasted_iota(jnp.int32, (n, LANES), 0)
    lane = lax.broadcasted_iota(jnp.int32, (n, LANES), 1)

    def conv_silu(p_ref, cw_ref, cols):
        x = p_ref[:, cols]
        xm1 = jnp.where(row == 0, 0.0, pltpu.roll(x, 1, 0))
        xp1 = jnp.where(row == n - 1, 0.0, pltpu.roll(x, n - 1, 0))
        return _silu(cw_ref[0:1, cols] * xm1 + cw_ref[1:2, cols] * x + cw_ref[2:3, cols] * xp1)

    def l2n(x):
        return x * lax.rsqrt(jnp.sum(x * x, axis=-1, keepdims=True) + 1e-6)

    for p in range(hp):
        cols = slice(p * LANES, (p + 1) * LANES)
        q_s[p] = l2n(conv_silu(qp_ref, cwq_ref, cols)) * (DK_C ** -0.5)
        k_s[p] = l2n(conv_silu(kp_ref, cwk_ref, cols))
        v_s[p] = conv_silu(vp_ref, cwv_ref, cols)

    gates = g_ref[...]
    local = row & (CHUNK - 1)
    pre = gates
    suf = gates
    s = 1
    while s < CHUNK:
        pre = pre + jnp.where(local >= s, pltpu.roll(pre, s, 0), 0.0)
        suf = suf + jnp.where(local < CHUNK - s, pltpu.roll(suf, n - s, 0), 0.0)
        s *= 2
    tot = pre + suf - gates

    def col(x, idx):
        picked = jnp.sum(jnp.where(lane == idx, x, 0.0), axis=1, keepdims=True)
        return jnp.broadcast_to(picked, (n, LANES))

    for p in range(hp):
        for d in range(2):
            beta_b[p, d] = col(gates, d * H_C + head0 + p)
            gc_b[p, d] = col(pre if d == 0 else suf, 2 * H_C + d * H_C + head0 + p)
            gtot_b[p, d] = col(tot, 2 * H_C + d * H_C + head0 + p)

    ii = lax.broadcasted_iota(jnp.int32, (CHUNK, CHUNK), 0)
    jj = lax.broadcasted_iota(jnp.int32, (CHUNK, CHUNK), 1)
    same_blk = (ii >> 4) == (jj >> 4)
    eye = (lax.broadcasted_iota(jnp.int32, (LANES, LANES), 0)
           == lax.broadcasted_iota(jnp.int32, (LANES, LANES), 1)).astype(BF16)
    incl = (ii >= jj, ii <= jj)
    strict = (ii > jj, ii < jj)
    heads = range(hp)
    pairs = [(p, c) for p in heads for c in range(nc)]
    chains = [(p, c, d) for p, c in pairs for d in range(2)]
    rows = lambda c: pl.ds(c * CHUNK, CHUNK)
    rows2 = lambda c: pl.ds(c * LANES, LANES)

    kc = [k_s[p, rows(c), :] for p, c in pairs]
    qc = [q_s[p, rows(c), :] for p, c in pairs]
    beta = [beta_b[p, d, rows(c), :] for p, c, d in chains]
    gc = [gc_b[p, d, rows(c), :] for p, c, d in chains]
    kb = [kc[i // 2] * b for i, b in enumerate(beta)]
    eg = [jnp.exp(g) for g in gc]
    a = [lax.dot_general(jnp.concatenate([kb[2 * j], kb[2 * j + 1], qc[j]], axis=0).astype(BF16),
                         kc[j].astype(BF16), _NT, preferred_element_type=F32)
         for j in range(len(pairs))]
    kdt = [lax.dot_general(eye, (kc[i // 2] * jnp.exp(gtot_b[p, d, rows(c), :] - gc[i])).astype(BF16), _NT,
                           preferred_element_type=F32).astype(BF16)
           for i, (p, c, d) in enumerate(chains)]
    ls = []
    for i, (p, c, d) in enumerate(chains):
        gci = gc[i][:, :CHUNK]
        gcj = jnp.sum(jnp.where(ii == jj, gci, 0.0), axis=0, keepdims=True)
        decay = jnp.where(incl[d], jnp.exp(jnp.where(incl[d], gci - gcj, 0.0)), 0.0)
        ls.append(jnp.where(strict[d], a[i // 2][d * CHUNK:(d + 1) * CHUNK] * decay, 0.0))
        intra_s[p, d, rows(c), :] = jnp.where(incl[d], a[i // 2][2 * CHUNK:] * decay, 0.0)
        wq_s[p, d, pl.ds(c * LANES + CHUNK, CHUNK), :] = (qc[i // 2] * eg[i]).astype(BF16)
    r = _unit_tri_inverse_residuals(ls, same_blk)
    rhs = [jnp.concatenate([v_s[p, rows(c), :] * b, x * e], axis=1)
           for (p, c, d), b, x, e in zip(chains, beta, kb, eg)]
    sol = _each(lambda x, y: y + _mm(x, y), r, rhs)
    for (p, c, d), x in zip(chains, sol):
        u_s[p, d, rows(c), :] = x[:, :LANES]
        wq_s[p, d, pl.ds(c * LANES, CHUNK), :] = x[:, LANES:].astype(BF16)
    ba = _each(lambda t, x: jnp.dot(t, x.astype(BF16), preferred_element_type=F32), kdt, sol)
    for (p, c, d), x in zip(chains, ba):
        bb_s[p, d, rows2(c), :] = x[:, :LANES]
        ab_s[p, d, rows2(c), :] = x[:, LANES:].astype(BF16)

    scans = [(p, d) for p in heads for d in range(2)]
    if has_s0:
        state = [s0_ref[d, p] for p, d in scans]
    else:
        state = [jnp.zeros((DK_C, LANES), F32) for _ in scans]
    for i in range(nc):
        for j, (p, d) in enumerate(scans):
            c = i if d == 0 else nc - 1 - i
            st_b = state[j].astype(BF16)
            st_s[p, d, rows2(c), :] = st_b
            eg_tot = jnp.exp(gtot_b[p, d, pl.ds(c * CHUNK, 1), :])
            state[j] = (state[j] * eg_tot + bb_s[p, d, rows2(c), :]
                        - jnp.dot(ab_s[p, d, rows2(c), :], st_b, preferred_element_type=F32))

    wo = [jnp.dot(wq_s[p, d, rows2(c), :], st_s[p, d, rows2(c), :], preferred_element_type=F32)
          for p, c, d in chains]
    vnew = [u_s[p, d, rows(c), :] - x[:CHUNK] for (p, c, d), x in zip(chains, wo)]
    o = [x[CHUNK:] + _mm(intra_s[p, d, rows(c), :], vn) for (p, c, d), x, vn in zip(chains, wo, vnew)]
    for j, (p, c) in enumerate(pairs):
        cols = slice(p * LANES, (p + 1) * LANES)
        oc = o[2 * j] + o[2 * j + 1]
        og_ref[rows(c), cols] = (_rms(oc, onw_ref[...]) * _silu(z_ref[rows(c), cols])).astype(BF16)
    if want_state:
        for j, (p, d) in enumerate(scans):
            sf_ref[d, p] = state[j]


def _gdn(qkv, conv_w, gates, z, onorm_w, latent, s0=None):
    n = N_LAT if latent else N_CTX
    nb = N_LAT_B if latent else N_CTX_B
    nc = n // CHUNK
    want_state = not latent
    hp = 1 if latent else 4
    groups = H_C // hp
    col = lambda off: pl.BlockSpec((n, hp * LANES), lambda b, h, off=off: (b, off * groups + h))
    cw = lambda off: pl.BlockSpec((3, hp * LANES), lambda b, h, off=off: (0, off * groups + h))
    state_spec = pl.BlockSpec((None, 2, hp, DK_C, LANES), lambda b, h: (b, 0, h, 0, 0))
    in_specs = [col(0), col(1), col(2), cw(0), cw(1), cw(2),
                pl.BlockSpec((n, LANES), lambda b, h: (b, 0)),
                pl.BlockSpec((n, hp * LANES), lambda b, h: (b, h)),
                pl.BlockSpec((1, LANES), lambda b, h: (0, 0))]
    args = [qkv, qkv, qkv, conv_w, conv_w, conv_w, gates, z, onorm_w.reshape(1, LANES)]
    if s0 is not None:
        in_specs.append(state_spec)
        args.append(s0)
    out_specs = [pl.BlockSpec((n, hp * LANES), lambda b, h: (b, h))]
    out_shape = [jax.ShapeDtypeStruct((nb * n, D), BF16)]
    if want_state:
        out_specs.append(state_spec)
        out_shape.append(jax.ShapeDtypeStruct((nb, 2, H_C, DK_C, LANES), F32))
    seq = lambda: pltpu.VMEM((hp, n, LANES), F32)
    both = lambda: pltpu.VMEM((hp, 2, n, LANES), F32)
    per_chunk = lambda dt: pltpu.VMEM((hp, 2, nc * LANES, LANES), dt)
    scratch = [seq(), seq(), seq(), both(), both(), both(), both(),
               per_chunk(BF16), pltpu.VMEM((hp, 2, n, CHUNK), F32), per_chunk(F32), per_chunk(BF16),
               per_chunk(BF16)]
    return pl.pallas_call(
        functools.partial(_gdn_kernel, n=n, hp=hp, has_s0=s0 is not None, want_state=want_state),
        grid=(nb, groups),
        in_specs=in_specs,
        out_specs=out_specs,
        out_shape=out_shape,
        scratch_shapes=scratch,
        compiler_params=_params(("parallel", "arbitrary")),
        name=f"gdn_{'lat' if latent else 'ctx'}",
    )(*args)


def _rope_tables():
    rows = N_LAT // GRID_W
    row = jnp.repeat(jnp.arange(rows), GRID_W).astype(F32)
    colp = jnp.tile(jnp.arange(GRID_W), rows).astype(F32)
    quarter = DH // 4
    inv = ROPE_BASE ** (-jnp.arange(quarter, dtype=F32) / quarter)
    ar, ac = row[:, None] * inv, colp[:, None] * inv
    cos = jnp.concatenate([jnp.cos(ar)] * 2 + [jnp.cos(ac)] * 2, axis=-1)
    sin = jnp.concatenate([-jnp.sin(ar), jnp.sin(ar), -jnp.sin(ac), jnp.sin(ac)], axis=-1)
    return jnp.tile(cos, (1, 2)), jnp.tile(sin, (1, 2))


def kernel(x_prompt, x_sample, cache_l0_k, cache_l0_v, cache_l1_k, cache_l1_v, state_l2, cache_l3_k, cache_l3_v, c, c_ctx, l0_norm_w, l0_mod_w, l0_mod_b, l0_in_w, l0_out_w, l0_sink, l1_norm_w, l1_mod_w, l1_mod_b, l1_in_w, l1_out_w, l1_lambda_q1, l1_lambda_k1, l1_lambda_q2, l1_lambda_k2, l1_subln_w, l2_norm_w, l2_mod_w, l2_mod_b, l2_in_w, l2_out_w, l2_conv_w, l2_a_log, l2_dt_bias, l2_onorm_w, l3_norm_w, l3_mod_w, l3_mod_b, l3_in_w, l3_out_w, l3_sink, final_norm_w):
    xc = x_prompt.reshape(N_CTX_B * N_CTX, D)
    xl = x_sample.reshape(N_LAT_B * N_LAT, D)
    cond = jnp.concatenate([c, c_ctx[None, :], jnp.zeros((3, D), F32)], axis=0)
    mods = _modulation(cond, (l0_mod_w, l1_mod_w, l2_mod_w, l3_mod_w), (l0_mod_b, l1_mod_b, l2_mod_b, l3_mod_b))
    rope_tabs = _rope_tables()
    kvw = HKV_A * DH
    new_state = []

    def layer_a(xc, xl, mod, norm_w, in_w, out_w, sink, cache_k, cache_v, final_w=None):
        order = jnp.array(_KV_ALIGNED_HEADS)
        by_head = lambda cols: cols.reshape(D, len(_KV_ALIGNED_HEADS), DH)[:, order].reshape(D, D)
        in_w = jnp.concatenate([by_head(in_w[:, :D]), in_w[:, D:D + 2 * kvw], by_head(in_w[:, D + 2 * kvw:])],
                               axis=1)
        out_w = out_w.reshape(len(_KV_ALIGNED_HEADS), DH, D)[order].reshape(D, D)
        sink = sink[order]
        w = in_w.astype(BF16)
        outs = [(D, BF16), (kvw, F32), (kvw, F32), (D, F32)]
        qc, kc, vc, zc = _inproj("a", xc, norm_w, mod, w, [], outs, latent=False)
        ql, kl, vl, zl = _inproj("a", xl, norm_w, mod, w, [], outs, latent=True, rope_tabs=rope_tabs)
        ogc = _attn_a(sink, qc, kc, vc, zc)
        ogl = _attn_a(sink, ql, kl, vl, zl,
                      cache=(cache_k.reshape(N_LAT_B * PAST, kvw), cache_v.reshape(N_LAT_B * PAST, kvw)))
        wo = out_w.astype(BF16)
        xc = _outproj(ogc, wo, xc, mod, latent=False, final_w=final_w)
        xl = _outproj(ogl, wo, xl, mod, latent=True, final_w=final_w)
        new_state.extend([kc.reshape(N_CTX_B, N_CTX, HKV_A, DH), vc.reshape(N_CTX_B, N_CTX, HKV_A, DH)])
        return xc, xl

    xc, xl = layer_a(xc, xl, mods[0], l0_norm_w, l0_in_w, l0_out_w, l0_sink, cache_l0_k, cache_l0_v)

    lam_init = 0.8 - 0.6 * math.exp(-0.3 * 1)
    w = l1_in_w.astype(BF16)
    outs = [(D, BF16), (D, F32), (D, F32), (D, F32)]
    qc, kc, vc, zc = _inproj("b", xc, l1_norm_w, mods[1], w, [], outs, latent=False)
    ql, kl, vl, zl = _inproj("b", xl, l1_norm_w, mods[1], w, [], outs, latent=True, rope_tabs=rope_tabs)
    lams = (l1_lambda_q1, l1_lambda_k1, l1_lambda_q2, l1_lambda_k2)
    ogc = _attn_b(lams, l1_subln_w, qc, kc, vc, zc, lam_init)
    ogl = _attn_b(lams, l1_subln_w, ql, kl, vl, zl, lam_init,
                  cache=(cache_l1_k.reshape(N_LAT_B * PAST, D), cache_l1_v.reshape(N_LAT_B * PAST, D)))
    wo = l1_out_w.astype(BF16)
    xc = _outproj(ogc, wo, xc, mods[1], latent=False)
    xl = _outproj(ogl, wo, xl, mods[1], latent=True)
    new_state.extend([kc.reshape(N_CTX_B, N_CTX, H_C, 2, DH), vc.reshape(N_CTX_B, N_CTX, H_C, 2 * DH)])

    w = l2_in_w[:, :4 * D].astype(BF16)
    wg = jnp.pad(l2_in_w[:, 4 * D:], ((0, 0), (0, LANES - 4 * H_C))).astype(BF16)
    lane_pad = lambda p: jnp.pad(p.reshape(1, 2 * H_C), ((0, 0), (2 * H_C, LANES - 4 * H_C)))
    extra = [wg, lane_pad(l2_a_log), lane_pad(l2_dt_bias)]
    outs = [(3 * D, F32), (D, F32), (LANES, F32)]
    qkvc, zc, gc = _inproj("c", xc, l2_norm_w, mods[2], w, extra, outs, latent=False)
    qkvl, zl, gl = _inproj("c", xl, l2_norm_w, mods[2], w, extra, outs, latent=True)
    ogc, st_new = _gdn(qkvc, l2_conv_w, gc, zc, l2_onorm_w, latent=False)
    (ogl,) = _gdn(qkvl, l2_conv_w, gl, zl, l2_onorm_w, latent=True, s0=state_l2)
    wo = l2_out_w.astype(BF16)
    xc = _outproj(ogc, wo, xc, mods[2], latent=False)
    xl = _outproj(ogl, wo, xl, mods[2], latent=True)
    new_state.append(st_new)

    yc, yl = layer_a(xc, xl, mods[3], l3_norm_w, l3_in_w, l3_out_w, l3_sink, cache_l3_k, cache_l3_v,
                     final_w=final_norm_w)
    return (yc.reshape(N_CTX_B, N_CTX, D), yl.reshape(N_LAT_B, N_LAT, D), *new_state)
```

```python
import functools
import math

import jax
import jax.numpy as jnp
from jax import lax
from jax.experimental import pallas as pl
from jax.experimental.pallas import tpu as pltpu

F32 = jnp.float32
BF16 = jnp.bfloat16

D = 1024
N_CTX_B, N_CTX = 16, 256
N_LAT_B, N_LAT = 4, 1024
PAST = 512
GRID_W = 64
ROPE_BASE = 10000.0
NORM_EPS = 1e-6
DH = 64
HKV_A = 4
WINDOW = 128
H_C = 8
DK_C = 128
CHUNK = 64
LANES = 128
TM = 512
LOG2E = math.log2(math.e)
Q_SCALE = DH ** -0.5 * LOG2E
VMEM_LIMIT = 48 * 1024 * 1024

_NT = (((1,), (1,)), ((), ()))


def _sigmoid(x):
    return 1.0 / (1.0 + jnp.exp(-x))


def _silu(x):
    return x * _sigmoid(x)


def _softplus(x):
    return jnp.maximum(x, 0.0) + jnp.log1p(jnp.exp(-jnp.abs(x)))


def _rms(x, w):
    return x * lax.rsqrt(jnp.mean(x * x, axis=-1, keepdims=True) + NORM_EPS) * w


def _params(sem):
    return pltpu.CompilerParams(dimension_semantics=sem, vmem_limit_bytes=VMEM_LIMIT)


def _mod_kernel(cond_ref, w0, w1, w2, w3, b0, b1, b2, b3, o0, o1, o2, o3):
    a = _silu(cond_ref[...]).astype(BF16)
    for w, b, o in ((w0, b0, o0), (w1, b1, o1), (w2, b2, o2), (w3, b3, o3)):
        o[...] = jnp.dot(a, w[...].astype(BF16), preferred_element_type=F32) + b[...]


def _modulation(cond, mod_ws, mod_bs):
    tn = 512
    wspec = pl.BlockSpec((D, tn), lambda j: (0, j))
    bspec = pl.BlockSpec((1, tn), lambda j: (0, j))
    ospec = pl.BlockSpec((8, tn), lambda j: (0, j))
    outs = pl.pallas_call(
        _mod_kernel,
        grid=(3 * D // tn,),
        in_specs=[pl.BlockSpec((8, D), lambda j: (0, 0))] + [wspec] * 4 + [bspec] * 4,
        out_specs=[ospec] * 4,
        out_shape=[jax.ShapeDtypeStruct((8, 3 * D), F32)] * 4,
        compiler_params=_params(("arbitrary",)),
        name="adaln_mod",
    )(cond, *mod_ws, *[b.reshape(1, 3 * D) for b in mod_bs])
    return [o.reshape(8, 1, 3 * D) for o in outs]


def _adaln_h(x_ref, nw_ref, mod_ref):
    y = _rms(x_ref[...], nw_ref[...])
    return (y * (1.0 + mod_ref[:, D:2 * D]) + mod_ref[:, 0:D]).astype(BF16)


def _proj(h, w_ref, lo, hi):
    return jnp.dot(h, w_ref[:, lo:hi], preferred_element_type=F32)


def _rope(x, cos_ref, sin_ref):
    width = x.shape[1]
    lane = lax.broadcasted_iota(jnp.int32, x.shape, 1)
    partner = jnp.where((lane & 31) < 16, pltpu.roll(x, width - 16, 1), pltpu.roll(x, 16, 1))
    reps = width // LANES
    return x * jnp.tile(cos_ref[...], (1, reps)) + partner * jnp.tile(sin_ref[...], (1, reps))


def _inproj_a_kernel(*refs, rope):
    if rope:
        x_ref, nw_ref, mod_ref, w_ref, cos_ref, sin_ref, q_ref, k_ref, v_ref, z_ref = refs
    else:
        x_ref, nw_ref, mod_ref, w_ref, q_ref, k_ref, v_ref, z_ref = refs
    h = _adaln_h(x_ref, nw_ref, mod_ref)
    for j in range(2):
        acc = _proj(h, w_ref, j * 512, (j + 1) * 512)
        if rope:
            acc = _rope(acc, cos_ref, sin_ref)
        q_ref[:, j * 512:(j + 1) * 512] = (acc * Q_SCALE).astype(BF16)
    kv = _proj(h, w_ref, 1024, 1536)
    k = kv[:, :256]
    if rope:
        k = _rope(k, cos_ref, sin_ref)
    k_ref[...] = k.astype(k_ref.dtype)
    v_ref[...] = kv[:, 256:].astype(v_ref.dtype)
    for j in range(2):
        z_ref[:, j * 512:(j + 1) * 512] = _proj(h, w_ref, 1536 + j * 512, 2048 + j * 512).astype(z_ref.dtype)


def _inproj_b_kernel(*refs, rope):
    if rope:
        x_ref, nw_ref, mod_ref, w_ref, cos_ref, sin_ref, q_ref, k_ref, v_ref, z_ref = refs
    else:
        x_ref, nw_ref, mod_ref, w_ref, q_ref, k_ref, v_ref, z_ref = refs
    h = _adaln_h(x_ref, nw_ref, mod_ref)
    for j in range(2):
        sl = slice(j * 512, (j + 1) * 512)
        q = _proj(h, w_ref, j * 512, (j + 1) * 512)
        k = _proj(h, w_ref, D + j * 512, D + (j + 1) * 512)
        if rope:
            q = _rope(q, cos_ref, sin_ref)
            k = _rope(k, cos_ref, sin_ref)
        q_ref[:, sl] = (q * Q_SCALE).astype(BF16)
        k_ref[:, sl] = k.astype(k_ref.dtype)
        v_ref[:, sl] = _proj(h, w_ref, 2 * D + j * 512, 2 * D + (j + 1) * 512).astype(v_ref.dtype)
        z_ref[:, sl] = _proj(h, w_ref, 3 * D + j * 512, 3 * D + (j + 1) * 512).astype(z_ref.dtype)


def _inproj_c_kernel(x_ref, nw_ref, mod_ref, w_ref, wg_ref, alog_ref, dtb_ref, qkv_ref, z_ref, g_ref):
    h = _adaln_h(x_ref, nw_ref, mod_ref)
    for j in range(6):
        qkv_ref[:, j * 512:(j + 1) * 512] = _proj(h, w_ref, j * 512, (j + 1) * 512).astype(qkv_ref.dtype)
    for j in range(2):
        z_ref[:, j * 512:(j + 1) * 512] = _proj(h, w_ref, 3 * D + j * 512, 3 * D + (j + 1) * 512).astype(z_ref.dtype)
    acc = jnp.dot(h, wg_ref[...], preferred_element_type=F32)
    lane = lax.broadcasted_iota(jnp.int32, acc.shape, 1)
    g = -jnp.exp(alog_ref[...]) * _softplus(acc + dtb_ref[...])
    g_ref[...] = jnp.where(lane < 2 * H_C, _sigmoid(acc), g)


def _inproj(kind, x, norm_w, mod, w, extra_in, outs, latent, rope_tabs=None):
    rows = x.shape[0]
    per_seq = (N_LAT if latent else N_CTX) // TM
    mod_row = (lambda i: (i // per_seq, 0, 0)) if latent else (lambda i: (4, 0, 0))
    in_specs = [
        pl.BlockSpec((TM, D), lambda i: (i, 0)),
        pl.BlockSpec((1, D), lambda i: (0, 0)),
        pl.BlockSpec((None, 1, 3 * D), mod_row),
        pl.BlockSpec(w.shape, lambda i: (0, 0)),
    ]
    args = [x, norm_w.reshape(1, D), mod, w]
    for e in extra_in:
        in_specs.append(pl.BlockSpec(e.shape, lambda i: (0, 0)))
        args.append(e)
    rope = rope_tabs is not None
    if rope:
        for t in rope_tabs:
            in_specs.append(pl.BlockSpec((TM, LANES), lambda i: (i % per_seq, 0)))
            args.append(t)
    if kind == "a":
        body = functools.partial(_inproj_a_kernel, rope=rope)
    elif kind == "b":
        body = functools.partial(_inproj_b_kernel, rope=rope)
    else:
        body = _inproj_c_kernel
    return pl.pallas_call(
        body,
        grid=(rows // TM,),
        in_specs=in_specs,
        out_specs=[pl.BlockSpec((TM, wd), lambda i: (i, 0)) for wd, _ in outs],
        out_shape=[jax.ShapeDtypeStruct((rows, wd), dt) for wd, dt in outs],
        compiler_params=_params(("parallel",)),
        name=f"inproj_{kind}_{'lat' if latent else 'ctx'}",
    )(*args)


def _outproj_kernel(og_ref, w_ref, x_ref, mod_ref, *rest, final):
    out = jnp.dot(og_ref[...], w_ref[...], preferred_element_type=F32)
    xn = x_ref[...] + mod_ref[:, 2 * D:3 * D] * out
    if final:
        fw_ref, y_ref = rest
        y_ref[...] = _rms(xn, fw_ref[...])
    else:
        (o_ref,) = rest
        o_ref[...] = xn


def _outproj(og, w, x, mod, latent, final_w=None):
    rows = x.shape[0]
    per_seq = (N_LAT if latent else N_CTX) // TM
    mod_row = (lambda i: (i // per_seq, 0, 0)) if latent else (lambda i: (4, 0, 0))
    in_specs = [
        pl.BlockSpec((TM, D), lambda i: (i, 0)),
        pl.BlockSpec((D, D), lambda i: (0, 0)),
        pl.BlockSpec((TM, D), lambda i: (i, 0)),
        pl.BlockSpec((None, 1, 3 * D), mod_row),
    ]
    args = [og, w, x, mod]
    if final_w is not None:
        in_specs.append(pl.BlockSpec((1, D), lambda i: (0, 0)))
        args.append(final_w.reshape(1, D))
    return pl.pallas_call(
        functools.partial(_outproj_kernel, final=final_w is not None),
        grid=(rows // TM,),
        in_specs=in_specs,
        out_specs=pl.BlockSpec((TM, D), lambda i: (i, 0)),
        out_shape=jax.ShapeDtypeStruct((rows, D), F32),
        compiler_params=_params(("parallel",)),
        name=f"outproj_{'lat' if latent else 'ctx'}",
    )(*args)


def _lane_lo(shape):
    return lax.broadcasted_iota(jnp.int32, shape, 1) < DH


Q_TILES_PER_KV_TILE = 4
_KV_ALIGNED_HEADS = (0, 4, 1, 5, 2, 6, 3, 7, 8, 12, 9, 13, 10, 14, 11, 15)


def _gqa_block(sink_ref, q_ref, z_ref, o_ref, kv_pieces, lhs_s, s_s, e_s):
    sq = q_ref.shape[0]
    lo = _lane_lo((1, LANES))
    halves = (lo, jnp.logical_not(lo))
    for j in range(HKV_A // 2):
        pieces = [(k2.astype(BF16), v2.astype(BF16), valid) for k2, v2, valid in kv_pieces(j)]
        tiles = range(Q_TILES_PER_KV_TILE * j, Q_TILES_PER_KV_TILE * (j + 1))
        blocks = [(t, a) for t in tiles for a in range(2)]
        for i, (t, a) in enumerate(blocks):
            q2 = q_ref[:, t * LANES:(t + 1) * LANES]
            lhs_s[i * sq:(i + 1) * sq, :] = jnp.where(halves[a], q2, jnp.zeros_like(q2))
        off = 0
        spans = []
        for k2, _, _ in pieces:
            s_s[:, off:off + k2.shape[0]] = lax.dot_general(lhs_s[...], k2, _NT, preferred_element_type=F32)
            spans.append(slice(off, off + k2.shape[0]))
            off += k2.shape[0]
        inv = []
        for i, (t, a) in enumerate(blocks):
            rws = slice(i * sq, (i + 1) * sq)
            sink = sink_ref[2 * t + a] * LOG2E
            scores = []
            for (_, _, valid), sp in zip(pieces, spans):
                s = s_s[rws, sp]
                scores.append(s if valid is None else jnp.where(valid, s, -jnp.inf))
            m = sink
            for s in scores:
                m = jnp.maximum(m, jnp.max(s, axis=-1, keepdims=True))
            denom = jnp.exp2(sink - m)
            for s, sp in zip(scores, spans):
                e = jnp.exp2(s - m)
                denom = denom + jnp.sum(e, axis=-1, keepdims=True)
                e_s[rws, sp] = e.astype(BF16)
            inv.append(1.0 / denom)
        pv = None
        for (_, v2, _), sp in zip(pieces, spans):
            part = jnp.dot(e_s[:, sp], v2, preferred_element_type=F32)
            pv = part if pv is None else pv + part
        for n_t, t in enumerate(tiles):
            i0, i1 = 2 * n_t, 2 * n_t + 1
            o2 = jnp.where(lo, pv[i0 * sq:(i0 + 1) * sq] * inv[i0], pv[i1 * sq:(i1 + 1) * sq] * inv[i1])
            cols = slice(t * LANES, (t + 1) * LANES)
            o_ref[:, cols] = (o2 * _silu(z_ref[:, cols].astype(F32))).astype(BF16)


def _gqa_scratch(sq, sk):
    stacked = 2 * Q_TILES_PER_KV_TILE * sq
    return [pltpu.VMEM((stacked, LANES), BF16), pltpu.VMEM((stacked, sk), F32), pltpu.VMEM((stacked, sk), BF16)]


def _attn_a_ctx_kernel(sink_ref, q_ref, k_ref, v_ref, z_ref, o_ref, *scratch):
    def kv_pieces(j):
        cols = slice(j * LANES, (j + 1) * LANES)
        return [(k_ref[:, cols], v_ref[:, cols], None)]
    _gqa_block(sink_ref, q_ref, z_ref, o_ref, kv_pieces, *scratch)


def _attn_a_lat_kernel(sink_ref, q_ref, k_ref, v_ref, kc_ref, vc_ref, z_ref, o_ref, *scratch):
    n = pl.program_id(1)
    span = 3 * WINDOW
    start = pl.multiple_of(jnp.clip((n - 1) * WINDOW, 0, N_LAT - span), WINDOW)
    qi = n * WINDOW + lax.broadcasted_iota(jnp.int32, (WINDOW, span), 0)
    kj = start + lax.broadcasted_iota(jnp.int32, (WINDOW, span), 1)
    valid = jnp.abs(kj - qi) <= WINDOW

    def kv_pieces(j):
        cols = slice(j * LANES, (j + 1) * LANES)
        return [(k_ref[pl.ds(start, span), cols], v_ref[pl.ds(start, span), cols], valid),
                (kc_ref[:, cols], vc_ref[:, cols], None)]
    _gqa_block(sink_ref, q_ref, z_ref, o_ref, kv_pieces, *scratch)


def _attn_a(sink, q, k, v, z, cache=None):
    smem = pl.BlockSpec(memory_space=pltpu.SMEM)
    kvw = HKV_A * DH
    if cache is None:
        return pl.pallas_call(
            _attn_a_ctx_kernel,
            grid=(N_CTX_B,),
            in_specs=[smem,
                      pl.BlockSpec((N_CTX, D), lambda b: (b, 0)),
                      pl.BlockSpec((N_CTX, kvw), lambda b: (b, 0)),
                      pl.BlockSpec((N_CTX, kvw), lambda b: (b, 0)),
                      pl.BlockSpec((N_CTX, D), lambda b: (b, 0))],
            out_specs=pl.BlockSpec((N_CTX, D), lambda b: (b, 0)),
            out_shape=jax.ShapeDtypeStruct((N_CTX_B * N_CTX, D), BF16),
            scratch_shapes=_gqa_scratch(N_CTX, N_CTX),
            compiler_params=_params(("parallel",)),
            name="attn_a_ctx",
        )(sink, q, k, v, z)
    kc, vc = cache
    nq = N_LAT // WINDOW
    return pl.pallas_call(
        _attn_a_lat_kernel,
        grid=(N_LAT_B, nq),
        in_specs=[smem,
                  pl.BlockSpec((WINDOW, D), lambda b, n: (b * nq + n, 0)),
                  pl.BlockSpec((N_LAT, kvw), lambda b, n: (b, 0)),
                  pl.BlockSpec((N_LAT, kvw), lambda b, n: (b, 0)),
                  pl.BlockSpec((PAST, kvw), lambda b, n: (b, 0)),
                  pl.BlockSpec((PAST, kvw), lambda b, n: (b, 0)),
                  pl.BlockSpec((WINDOW, D), lambda b, n: (b * nq + n, 0))],
        out_specs=pl.BlockSpec((WINDOW, D), lambda b, n: (b * nq + n, 0)),
        out_shape=jax.ShapeDtypeStruct((N_LAT_B * N_LAT, D), BF16),
        scratch_shapes=_gqa_scratch(WINDOW, 3 * WINDOW + PAST),
        compiler_params=_params(("parallel", "arbitrary")),
        name="attn_a_lat",
    )(sink, q, k, v, kc, vc, z)


def _diff_attn_block(lam_refs, subln_ref, q_ref, z_ref, o_ref, kv_pieces, lam_init):
    lq1, lk1, lq2, lk2 = lam_refs
    dot_exp = lambda a, c: jnp.exp(jnp.sum(a[...] * c[...], axis=-1, keepdims=True))
    lam = dot_exp(lq1, lk1) - dot_exp(lq2, lk2) + lam_init
    lo = _lane_lo((1, LANES))
    for h in range(D // LANES):
        cols = slice(h * LANES, (h + 1) * LANES)
        q2 = q_ref[:, cols]
        pieces = kv_pieces(h)
        exps, denoms = [], []
        for c in range(2):
            half = lo if c == 0 else jnp.logical_not(lo)
            qm = jnp.where(half, q2, jnp.zeros_like(q2))
            scores = [lax.dot_general(qm, k2, _NT, preferred_element_type=F32) for k2, _ in pieces]
            m = None
            for s in scores:
                sm = jnp.max(s, axis=-1, keepdims=True)
                m = sm if m is None else jnp.maximum(m, sm)
            es = [jnp.exp2(s - m) for s in scores]
            denom = None
            for e in es:
                se = jnp.sum(e, axis=-1, keepdims=True)
                denom = se if denom is None else denom + se
            exps.append(es)
            denoms.append(denom)
        ratio = lam * denoms[0] / denoms[1]
        o = None
        for i, (_, vx) in enumerate(pieces):
            a = (exps[0][i] - ratio * exps[1][i]).astype(BF16)
            po = jnp.dot(a, vx, preferred_element_type=F32)
            o = po if o is None else o + po
        o = o * (1.0 / denoms[0])
        o = _rms(o, subln_ref[...]) * (1.0 - lam_init)
        o_ref[:, cols] = (o * _silu(z_ref[:, cols].astype(F32))).astype(BF16)


def _attn_b_ctx_kernel(lq1, lk1, lq2, lk2, subln_ref, q_ref, k_ref, v_ref, z_ref, o_ref, *, lam_init):
    def kv_pieces(h):
        cols = slice(h * LANES, (h + 1) * LANES)
        return [(k_ref[:, cols].astype(BF16), v_ref[:, cols].astype(BF16))]
    _diff_attn_block((lq1, lk1, lq2, lk2), subln_ref, q_ref, z_ref, o_ref, kv_pieces, lam_init)


def _attn_b_lat_kernel(lq1, lk1, lq2, lk2, subln_ref, q_ref, k_ref, v_ref, kc_ref, vc_ref, z_ref, o_ref,
                       *, lam_init):
    def kv_pieces(h):
        cols = slice(h * LANES, (h + 1) * LANES)
        return [(k_ref[:, cols].astype(BF16), v_ref[:, cols].astype(BF16)),
                (kc_ref[:, cols].astype(BF16), vc_ref[:, cols].astype(BF16))]
    _diff_attn_block((lq1, lk1, lq2, lk2), subln_ref, q_ref, z_ref, o_ref, kv_pieces, lam_init)


def _attn_b(lams, subln, q, k, v, z, lam_init, cache=None):
    small = [pl.BlockSpec((1, DH), lambda *_: (0, 0))] * 4 + [pl.BlockSpec((1, 2 * DH), lambda *_: (0, 0))]
    small_args = [l.reshape(1, DH) for l in lams] + [subln.reshape(1, 2 * DH)]
    if cache is None:
        blk = pl.BlockSpec((N_CTX, D), lambda b: (b, 0))
        return pl.pallas_call(
            functools.partial(_attn_b_ctx_kernel, lam_init=lam_init),
            grid=(N_CTX_B,),
            in_specs=small + [blk] * 4,
            out_specs=blk,
            out_shape=jax.ShapeDtypeStruct((N_CTX_B * N_CTX, D), BF16),
            compiler_params=_params(("parallel",)),
            name="attn_b_ctx",
        )(*small_args, q, k, v, z)
    kc, vc = cache
    tq = 256
    nq = N_LAT // tq
    qblk = pl.BlockSpec((tq, D), lambda b, n: (b * nq + n, 0))
    return pl.pallas_call(
        functools.partial(_attn_b_lat_kernel, lam_init=lam_init),
        grid=(N_LAT_B, nq),
        in_specs=small + [qblk,
                          pl.BlockSpec((N_LAT, D), lambda b, n: (b, 0)),
                          pl.BlockSpec((N_LAT, D), lambda b, n: (b, 0)),
                          pl.BlockSpec((PAST, D), lambda b, n: (b, 0)),
                          pl.BlockSpec((PAST, D), lambda b, n: (b, 0)),
                          qblk],
        out_specs=qblk,
        out_shape=jax.ShapeDtypeStruct((N_LAT_B * N_LAT, D), BF16),
        compiler_params=_params(("parallel", "arbitrary")),
        name="attn_b_lat",
    )(*small_args, q, k, v, kc, vc, z)


def _mm(a, b):
    return jnp.dot(a.astype(BF16), b.astype(BF16), preferred_element_type=F32)


def _each(fn, *lists):
    return [fn(*xs) for xs in zip(*lists)]


def _unit_tri_inverse_residuals(ls, same_blk):
    rd = [jnp.where(same_blk, -l, 0.0) for l in ls]
    lo = [jnp.where(same_blk, 0.0, l) for l in ls]
    pk = _each(_mm, rd, rd)
    for it in range(3):
        t = _each(_mm, rd, pk)
        nxt = _each(_mm, pk, pk) if it < 2 else None
        rd = _each(lambda r, p, x: r + p + x, rd, pk, t)
        pk = nxt
    m = _each(lambda x, r: x + _mm(r, x), lo, rd)
    m2 = _each(_mm, m, m)
    q = _each(lambda a, a2: a2 - a - _mm(a, a2), m, m2)
    return _each(lambda a, r: a + r + _mm(a, r), q, rd)


def _gdn_kernel(*refs, n, hp, has_s0, want_state):
    it = iter(refs)
    qp_ref, kp_ref, vp_ref, cwq_ref, cwk_ref, cwv_ref, g_ref, z_ref, onw_ref = [next(it) for _ in range(9)]
    s0_ref = next(it) if has_s0 else None
    og_ref = next(it)
    sf_ref = next(it) if want_state else None
    q_s, k_s, v_s, beta_b, gc_b, gtot_b, u_s, wq_s, intra_s, bb_s, ab_s, st_s = it
    nc = n // CHUNK
    head0 = pl.program_id(1) * hp

    row = lax.broadcasted_iota(jnp.int32, (n, LANES), 0)
    lane = lax.broadcasted_iota(jnp.int32, (n, LANES), 1)

    def conv_silu(p_ref, cw_ref, cols):
        x = p_ref[:, cols].astype(F32)
        xm1 = jnp.where(row == 0, 0.0, pltpu.roll(x, 1, 0))
        xp1 = jnp.where(row == n - 1, 0.0, pltpu.roll(x, n - 1, 0))
        return _silu(cw_ref[0:1, cols] * xm1 + cw_ref[1:2, cols] * x + cw_ref[2:3, cols] * xp1)

    def l2n(x):
        return x * lax.rsqrt(jnp.sum(x * x, axis=-1, keepdims=True) + 1e-6)

    for p in range(hp):
        cols = slice(p * LANES, (p + 1) * LANES)
        q_s[p] = l2n(conv_silu(qp_ref, cwq_ref, cols)) * (DK_C ** -0.5)
        k_s[p] = l2n(conv_silu(kp_ref, cwk_ref, cols))
        v_s[p] = conv_silu(vp_ref, cwv_ref, cols)

    gates = g_ref[...]
    local = row & (CHUNK - 1)
    pre = gates
    suf = gates
    s = 1
    while s < CHUNK:
        pre = pre + jnp.where(local >= s, pltpu.roll(pre, s, 0), 0.0)
        suf = suf + jnp.where(local < CHUNK - s, pltpu.roll(suf, n - s, 0), 0.0)
        s *= 2
    tot = pre + suf - gates

    def col(x, idx):
        picked = jnp.sum(jnp.where(lane == idx, x, 0.0), axis=1, keepdims=True)
        return jnp.broadcast_to(picked, (n, LANES))

    for p in range(hp):
        for d in range(2):
            beta_b[p, d] = col(gates, d * H_C + head0 + p)
            gc_b[p, d] = col(pre if d == 0 else suf, 2 * H_C + d * H_C + head0 + p)
            gtot_b[p, d] = col(tot, 2 * H_C + d * H_C + head0 + p)

    ii = lax.broadcasted_iota(jnp.int32, (CHUNK, CHUNK), 0)
    jj = lax.broadcasted_iota(jnp.int32, (CHUNK, CHUNK), 1)
    same_blk = (ii >> 4) == (jj >> 4)
    eye = (lax.broadcasted_iota(jnp.int32, (LANES, LANES), 0)
           == lax.broadcasted_iota(jnp.int32, (LANES, LANES), 1)).astype(BF16)
    incl = (ii >= jj, ii <= jj)
    strict = (ii > jj, ii < jj)
    heads = range(hp)
    pairs = [(p, c) for p in heads for c in range(nc)]
    chains = [(p, c, d) for p, c in pairs for d in range(2)]
    rows = lambda c: pl.ds(c * CHUNK, CHUNK)
    rows2 = lambda c: pl.ds(c * LANES, LANES)

    kc = [k_s[p, rows(c), :] for p, c in pairs]
    qc = [q_s[p, rows(c), :] for p, c in pairs]
    beta = [beta_b[p, d, rows(c), :] for p, c, d in chains]
    gc = [gc_b[p, d, rows(c), :] for p, c, d in chains]
    kb = [kc[i // 2] * b for i, b in enumerate(beta)]
    eg = [jnp.exp(g) for g in gc]
    a = [lax.dot_general(jnp.concatenate([kb[2 * j], kb[2 * j + 1], qc[j]], axis=0).astype(BF16),
                         kc[j].astype(BF16), _NT, preferred_element_type=F32)
         for j in range(len(pairs))]
    kdt = [lax.dot_general(eye, (kc[i // 2] * jnp.exp(gtot_b[p, d, rows(c), :] - gc[i])).astype(BF16), _NT,
                           preferred_element_type=F32).astype(BF16)
           for i, (p, c, d) in enumerate(chains)]
    ls = []
    for i, (p, c, d) in enumerate(chains):
        gci = gc[i][:, :CHUNK]
        gcj = jnp.sum(jnp.where(ii == jj, gci, 0.0), axis=0, keepdims=True)
        decay = jnp.where(incl[d], jnp.exp(jnp.where(incl[d], gci - gcj, 0.0)), 0.0)
        ls.append(jnp.where(strict[d], a[i // 2][d * CHUNK:(d + 1) * CHUNK] * decay, 0.0))
        intra_s[p, d, rows(c), :] = jnp.where(incl[d], a[i // 2][2 * CHUNK:] * decay, 0.0)
        wq_s[p, d, pl.ds(c * LANES + CHUNK, CHUNK), :] = (qc[i // 2] * eg[i]).astype(BF16)
    r = _unit_tri_inverse_residuals(ls, same_blk)
    rhs = [jnp.concatenate([v_s[p, rows(c), :] * b, x * e], axis=1)
           for (p, c, d), b, x, e in zip(chains, beta, kb, eg)]
    sol = _each(lambda x, y: y + _mm(x, y), r, rhs)
    for (p, c, d), x in zip(chains, sol):
        u_s[p, d, rows(c), :] = x[:, :LANES]
        wq_s[p, d, pl.ds(c * LANES, CHUNK), :] = x[:, LANES:].astype(BF16)
    ba = _each(lambda t, x: jnp.dot(t, x.astype(BF16), preferred_element_type=F32), kdt, sol)
    for (p, c, d), x in zip(chains, ba):
        bb_s[p, d, rows2(c), :] = x[:, :LANES]
        ab_s[p, d, rows2(c), :] = x[:, LANES:].astype(BF16)

    scans = [(p, d) for p in heads for d in range(2)]
    if has_s0:
        state = [s0_ref[d, p] for p, d in scans]
    else:
        state = [jnp.zeros((DK_C, LANES), F32) for _ in scans]
    for i in range(nc):
        for j, (p, d) in enumerate(scans):
            c = i if d == 0 else nc - 1 - i
            st_b = state[j].astype(BF16)
            st_s[p, d, rows2(c), :] = st_b
            eg_tot = jnp.exp(gtot_b[p, d, pl.ds(c * CHUNK, 1), :])
            state[j] = (state[j] * eg_tot + bb_s[p, d, rows2(c), :]
                        - jnp.dot(ab_s[p, d, rows2(c), :], st_b, preferred_element_type=F32))

    wo = [jnp.dot(wq_s[p, d, rows2(c), :], st_s[p, d, rows2(c), :], preferred_element_type=F32)
          for p, c, d in chains]
    vnew = [u_s[p, d, rows(c), :] - x[:CHUNK] for (p, c, d), x in zip(chains, wo)]
    o = [x[CHUNK:] + _mm(intra_s[p, d, rows(c), :], vn) for (p, c, d), x, vn in zip(chains, wo, vnew)]
    for j, (p, c) in enumerate(pairs):
        cols = slice(p * LANES, (p + 1) * LANES)
        oc = o[2 * j] + o[2 * j + 1]
        og_ref[rows(c), cols] = (_rms(oc, onw_ref[...]) * _silu(z_ref[rows(c), cols].astype(F32))).astype(BF16)
    if want_state:
        for j, (p, d) in enumerate(scans):
            sf_ref[d, p] = state[j]


def _gdn(qkv, conv_w, gates, z, onorm_w, latent, s0=None):
    n = N_LAT if latent else N_CTX
    nb = N_LAT_B if latent else N_CTX_B
    nc = n // CHUNK
    want_state = not latent
    hp = 1 if latent else 4
    groups = H_C // hp
    col = lambda off: pl.BlockSpec((n, hp * LANES), lambda b, h, off=off: (b, off * groups + h))
    cw = lambda off: pl.BlockSpec((3, hp * LANES), lambda b, h, off=off: (0, off * groups + h))
    state_spec = pl.BlockSpec((None, 2, hp, DK_C, LANES), lambda b, h: (b, 0, h, 0, 0))
    in_specs = [col(0), col(1), col(2), cw(0), cw(1), cw(2),
                pl.BlockSpec((n, LANES), lambda b, h: (b, 0)),
                pl.BlockSpec((n, hp * LANES), lambda b, h: (b, h)),
                pl.BlockSpec((1, LANES), lambda b, h: (0, 0))]
    args = [qkv, qkv, qkv, conv_w, conv_w, conv_w, gates, z, onorm_w.reshape(1, LANES)]
    if s0 is not None:
        in_specs.append(state_spec)
        args.append(s0)
    out_specs = [pl.BlockSpec((n, hp * LANES), lambda b, h: (b, h))]
    out_shape = [jax.ShapeDtypeStruct((nb * n, D), BF16)]
    if want_state:
        out_specs.append(state_spec)
        out_shape.append(jax.ShapeDtypeStruct((nb, 2, H_C, DK_C, LANES), F32))
    seq = lambda: pltpu.VMEM((hp, n, LANES), F32)
    both = lambda: pltpu.VMEM((hp, 2, n, LANES), F32)
    per_chunk = lambda dt: pltpu.VMEM((hp, 2, nc * LANES, LANES), dt)
    scratch = [seq(), seq(), seq(), both(), both(), both(), both(),
               per_chunk(BF16), pltpu.VMEM((hp, 2, n, CHUNK), F32), per_chunk(F32), per_chunk(BF16),
               per_chunk(BF16)]
    return pl.pallas_call(
        functools.partial(_gdn_kernel, n=n, hp=hp, has_s0=s0 is not None, want_state=want_state),
        grid=(nb, groups),
        in_specs=in_specs,
        out_specs=out_specs,
        out_shape=out_shape,
        scratch_shapes=scratch,
        compiler_params=_params(("parallel", "arbitrary")),
        name=f"gdn_{'lat' if latent else 'ctx'}",
    )(*args)


def _rope_tables():
    rows = N_LAT // GRID_W
    row = jnp.repeat(jnp.arange(rows), GRID_W).astype(F32)
    colp = jnp.tile(jnp.arange(GRID_W), rows).astype(F32)
    quarter = DH // 4
    inv = ROPE_BASE ** (-jnp.arange(quarter, dtype=F32) / quarter)
    ar, ac = row[:, None] * inv, colp[:, None] * inv
    cos = jnp.concatenate([jnp.cos(ar)] * 2 + [jnp.cos(ac)] * 2, axis=-1)
    sin = jnp.concatenate([-jnp.sin(ar), jnp.sin(ar), -jnp.sin(ac), jnp.sin(ac)], axis=-1)
    return jnp.tile(cos, (1, 2)), jnp.tile(sin, (1, 2))


def kernel(x_prompt, x_sample, cache_l0_k, cache_l0_v, cache_l1_k, cache_l1_v, state_l2, cache_l3_k, cache_l3_v, c, c_ctx, l0_norm_w, l0_mod_w, l0_mod_b, l0_in_w, l0_out_w, l0_sink, l1_norm_w, l1_mod_w, l1_mod_b, l1_in_w, l1_out_w, l1_lambda_q1, l1_lambda_k1, l1_lambda_q2, l1_lambda_k2, l1_subln_w, l2_norm_w, l2_mod_w, l2_mod_b, l2_in_w, l2_out_w, l2_conv_w, l2_a_log, l2_dt_bias, l2_onorm_w, l3_norm_w, l3_mod_w, l3_mod_b, l3_in_w, l3_out_w, l3_sink, final_norm_w):
    xc = x_prompt.reshape(N_CTX_B * N_CTX, D)
    xl = x_sample.reshape(N_LAT_B * N_LAT, D)
    cond = jnp.concatenate([c, c_ctx[None, :], jnp.zeros((3, D), F32)], axis=0)
    mods = _modulation(cond, (l0_mod_w, l1_mod_w, l2_mod_w, l3_mod_w), (l0_mod_b, l1_mod_b, l2_mod_b, l3_mod_b))
    rope_tabs = _rope_tables()
    kvw = HKV_A * DH
    new_state = []

    def layer_a(xc, xl, mod, norm_w, in_w, out_w, sink, cache_k, cache_v, final_w=None):
        order = jnp.array(_KV_ALIGNED_HEADS)
        by_head = lambda cols: cols.reshape(D, len(_KV_ALIGNED_HEADS), DH)[:, order].reshape(D, D)
        in_w = jnp.concatenate([by_head(in_w[:, :D]), in_w[:, D:D + 2 * kvw], by_head(in_w[:, D + 2 * kvw:])],
                               axis=1)
        out_w = out_w.reshape(len(_KV_ALIGNED_HEADS), DH, D)[order].reshape(D, D)
        sink = sink[order]
        w = in_w.astype(BF16)
        outs = lambda kv_dt: [(D, BF16), (kvw, kv_dt), (kvw, kv_dt), (D, BF16)]
        qc, kc, vc, zc = _inproj("a", xc, norm_w, mod, w, [], outs(F32), latent=False)
        ql, kl, vl, zl = _inproj("a", xl, norm_w, mod, w, [], outs(BF16), latent=True, rope_tabs=rope_tabs)
        ogc = _attn_a(sink, qc, kc, vc, zc)
        ogl = _attn_a(sink, ql, kl, vl, zl,
                      cache=(cache_k.reshape(N_LAT_B * PAST, kvw), cache_v.reshape(N_LAT_B * PAST, kvw)))
        wo = out_w.astype(BF16)
        xc = _outproj(ogc, wo, xc, mod, latent=False, final_w=final_w)
        xl = _outproj(ogl, wo, xl, mod, latent=True, final_w=final_w)
        new_state.extend([kc.reshape(N_CTX_B, N_CTX, HKV_A, DH), vc.reshape(N_CTX_B, N_CTX, HKV_A, DH)])
        return xc, xl

    xc, xl = layer_a(xc, xl, mods[0], l0_norm_w, l0_in_w, l0_out_w, l0_sink, cache_l0_k, cache_l0_v)

    lam_init = 0.8 - 0.6 * math.exp(-0.3 * 1)
    w = l1_in_w.astype(BF16)
    outs = lambda kv_dt: [(D, BF16), (D, kv_dt), (D, kv_dt), (D, BF16)]
    qc, kc, vc, zc = _inproj("b", xc, l1_norm_w, mods[1], w, [], outs(F32), latent=False)
    ql, kl, vl, zl = _inproj("b", xl, l1_norm_w, mods[1], w, [], outs(BF16), latent=True, rope_tabs=rope_tabs)
    lams = (l1_lambda_q1, l1_lambda_k1, l1_lambda_q2, l1_lambda_k2)
    ogc = _attn_b(lams, l1_subln_w, qc, kc, vc, zc, lam_init)
    ogl = _attn_b(lams, l1_subln_w, ql, kl, vl, zl, lam_init,
                  cache=(cache_l1_k.reshape(N_LAT_B * PAST, D), cache_l1_v.reshape(N_LAT_B * PAST, D)))
    wo = l1_out_w.astype(BF16)
    xc = _outproj(ogc, wo, xc, mods[1], latent=False)
    xl = _outproj(ogl, wo, xl, mods[1], latent=True)
    new_state.extend([kc.reshape(N_CTX_B, N_CTX, H_C, 2, DH), vc.reshape(N_CTX_B, N_CTX, H_C, 2 * DH)])

    w = l2_in_w[:, :4 * D].astype(BF16)
    wg = jnp.pad(l2_in_w[:, 4 * D:], ((0, 0), (0, LANES - 4 * H_C))).astype(BF16)
    lane_pad = lambda p: jnp.pad(p.reshape(1, 2 * H_C), ((0, 0), (2 * H_C, LANES - 4 * H_C)))
    extra = [wg, lane_pad(l2_a_log), lane_pad(l2_dt_bias)]
    outs = [(3 * D, BF16), (D, BF16), (LANES, F32)]
    qkvc, zc, gc = _inproj("c", xc, l2_norm_w, mods[2], w, extra, outs, latent=False)
    qkvl, zl, gl = _inproj("c", xl, l2_norm_w, mods[2], w, extra, outs, latent=True)
    ogc, st_new = _gdn(qkvc, l2_conv_w, gc, zc, l2_onorm_w, latent=False)
    (ogl,) = _gdn(qkvl, l2_conv_w, gl, zl, l2_onorm_w, latent=True, s0=state_l2)
    wo = l2_out_w.astype(BF16)
    xc = _outproj(ogc, wo, xc, mods[2], latent=False)
    xl = _outproj(ogl, wo, xl, mods[2], latent=True)
    new_state.append(st_new)

    yc, yl = layer_a(xc, xl, mods[3], l3_norm_w, l3_in_w, l3_out_w, l3_sink, cache_l3_k, cache_l3_v,
                     final_w=final_norm_w)
    return (yc.reshape(N_CTX_B, N_CTX, D), yl.reshape(N_LAT_B, N_LAT, D), *new_state)
```

```python
import functools
import math

import jax
import jax.numpy as jnp
from jax import lax
from jax.experimental import pallas as pl
from jax.experimental.pallas import tpu as pltpu

F32 = jnp.float32
BF16 = jnp.bfloat16

D = 1024
N_CTX_B, N_CTX = 16, 256
N_LAT_B, N_LAT = 4, 1024
PAST = 512
GRID_W = 64
ROPE_BASE = 10000.0
NORM_EPS = 1e-6
DH = 64
HKV_A = 4
WINDOW = 128
H_C = 8
DK_C = 128
CHUNK = 64
LANES = 128
TM = 512
LOG2E = math.log2(math.e)
Q_SCALE = DH ** -0.5 * LOG2E
VMEM_LIMIT = 48 * 1024 * 1024

_NT = (((1,), (1,)), ((), ()))


def _sigmoid(x):
    return 1.0 / (1.0 + jnp.exp(-x))


def _silu(x):
    return x * _sigmoid(x)


def _softplus(x):
    return jnp.maximum(x, 0.0) + jnp.log1p(jnp.exp(-jnp.abs(x)))


def _rms(x, w):
    return x * lax.rsqrt(jnp.mean(x * x, axis=-1, keepdims=True) + NORM_EPS) * w


def _params(sem):
    return pltpu.CompilerParams(dimension_semantics=sem, vmem_limit_bytes=VMEM_LIMIT)


def _mod_kernel(cond_ref, w0, w1, w2, w3, b0, b1, b2, b3, o0, o1, o2, o3):
    a = _silu(cond_ref[...]).astype(BF16)
    for w, b, o in ((w0, b0, o0), (w1, b1, o1), (w2, b2, o2), (w3, b3, o3)):
        o[...] = jnp.dot(a, w[...].astype(BF16), preferred_element_type=F32) + b[...]


def _modulation(cond, mod_ws, mod_bs):
    tn = 512
    wspec = pl.BlockSpec((D, tn), lambda j: (0, j))
    bspec = pl.BlockSpec((1, tn), lambda j: (0, j))
    ospec = pl.BlockSpec((8, tn), lambda j: (0, j))
    outs = pl.pallas_call(
        _mod_kernel,
        grid=(3 * D // tn,),
        in_specs=[pl.BlockSpec((8, D), lambda j: (0, 0))] + [wspec] * 4 + [bspec] * 4,
        out_specs=[ospec] * 4,
        out_shape=[jax.ShapeDtypeStruct((8, 3 * D), F32)] * 4,
        compiler_params=_params(("arbitrary",)),
        name="adaln_mod",
    )(cond, *mod_ws, *[b.reshape(1, 3 * D) for b in mod_bs])
    return [o.reshape(8, 1, 3 * D) for o in outs]


def _adaln_h(x_ref, nw_ref, mod_ref):
    y = _rms(x_ref[...], nw_ref[...])
    return (y * (1.0 + mod_ref[:, D:2 * D]) + mod_ref[:, 0:D]).astype(BF16)


def _proj(h, w_ref, lo, hi):
    return jnp.dot(h, w_ref[:, lo:hi], preferred_element_type=F32)


def _rope(x, cos_ref, sin_ref):
    width = x.shape[1]
    lane = lax.broadcasted_iota(jnp.int32, x.shape, 1)
    partner = jnp.where((lane & 31) < 16, pltpu.roll(x, width - 16, 1), pltpu.roll(x, 16, 1))
    reps = width // LANES
    return x * jnp.tile(cos_ref[...], (1, reps)) + partner * jnp.tile(sin_ref[...], (1, reps))


def _inproj_a_kernel(*refs, rope):
    if rope:
        x_ref, nw_ref, mod_ref, w_ref, cos_ref, sin_ref, q_ref, k_ref, v_ref, z_ref = refs
    else:
        x_ref, nw_ref, mod_ref, w_ref, q_ref, k_ref, v_ref, z_ref = refs
    h = _adaln_h(x_ref, nw_ref, mod_ref)
    for j in range(2):
        acc = _proj(h, w_ref, j * 512, (j + 1) * 512)
        if rope:
            acc = _rope(acc, cos_ref, sin_ref)
        q_ref[:, j * 512:(j + 1) * 512] = (acc * Q_SCALE).astype(BF16)
    kv = _proj(h, w_ref, 1024, 1536)
    k = kv[:, :256]
    if rope:
        k = _rope(k, cos_ref, sin_ref)
    k_ref[...] = k.astype(k_ref.dtype)
    v_ref[...] = kv[:, 256:].astype(v_ref.dtype)
    for j in range(2):
        z_ref[:, j * 512:(j + 1) * 512] = _proj(h, w_ref, 1536 + j * 512, 2048 + j * 512).astype(z_ref.dtype)


def _inproj_b_kernel(*refs, rope):
    if rope:
        x_ref, nw_ref, mod_ref, w_ref, cos_ref, sin_ref, q_ref, k_ref, v_ref, z_ref = refs
    else:
        x_ref, nw_ref, mod_ref, w_ref, q_ref, k_ref, v_ref, z_ref = refs
    h = _adaln_h(x_ref, nw_ref, mod_ref)
    for j in range(2):
        sl = slice(j * 512, (j + 1) * 512)
        q = _proj(h, w_ref, j * 512, (j + 1) * 512)
        k = _proj(h, w_ref, D + j * 512, D + (j + 1) * 512)
        if rope:
            q = _rope(q, cos_ref, sin_ref)
            k = _rope(k, cos_ref, sin_ref)
        q_ref[:, sl] = (q * Q_SCALE).astype(BF16)
        k_ref[:, sl] = k.astype(k_ref.dtype)
        v_ref[:, sl] = _proj(h, w_ref, 2 * D + j * 512, 2 * D + (j + 1) * 512).astype(v_ref.dtype)
        z_ref[:, sl] = _proj(h, w_ref, 3 * D + j * 512, 3 * D + (j + 1) * 512).astype(z_ref.dtype)


def _inproj_c_kernel(x_ref, nw_ref, mod_ref, w_ref, wg_ref, alog_ref, dtb_ref, qkv_ref, z_ref, g_ref):
    h = _adaln_h(x_ref, nw_ref, mod_ref)
    for j in range(6):
        qkv_ref[:, j * 512:(j + 1) * 512] = _proj(h, w_ref, j * 512, (j + 1) * 512).astype(qkv_ref.dtype)
    for j in range(2):
        z_ref[:, j * 512:(j + 1) * 512] = _proj(h, w_ref, 3 * D + j * 512, 3 * D + (j + 1) * 512).astype(z_ref.dtype)
    acc = jnp.dot(h, wg_ref[...], preferred_element_type=F32)
    lane = lax.broadcasted_iota(jnp.int32, acc.shape, 1)
    g = -jnp.exp(alog_ref[...]) * _softplus(acc + dtb_ref[...])
    g_ref[...] = jnp.where(lane < 2 * H_C, _sigmoid(acc), g)


def _inproj(kind, x, norm_w, mod, w, extra_in, outs, latent, rope_tabs=None):
    rows = x.shape[0]
    per_seq = (N_LAT if latent else N_CTX) // TM
    mod_row = (lambda i: (i // per_seq, 0, 0)) if latent else (lambda i: (4, 0, 0))
    in_specs = [
        pl.BlockSpec((TM, D), lambda i: (i, 0)),
        pl.BlockSpec((1, D), lambda i: (0, 0)),
        pl.BlockSpec((None, 1, 3 * D), mod_row),
        pl.BlockSpec(w.shape, lambda i: (0, 0)),
    ]
    args = [x, norm_w.reshape(1, D), mod, w]
    for e in extra_in:
        in_specs.append(pl.BlockSpec(e.shape, lambda i: (0, 0)))
        args.append(e)
    rope = rope_tabs is not None
    if rope:
        for t in rope_tabs:
            in_specs.append(pl.BlockSpec((TM, LANES), lambda i: (i % per_seq, 0)))
            args.append(t)
    if kind == "a":
        body = functools.partial(_inproj_a_kernel, rope=rope)
    elif kind == "b":
        body = functools.partial(_inproj_b_kernel, rope=rope)
    else:
        body = _inproj_c_kernel
    return pl.pallas_call(
        body,
        grid=(rows // TM,),
        in_specs=in_specs,
        out_specs=[pl.BlockSpec((TM, wd), lambda i: (i, 0)) for wd, _ in outs],
        out_shape=[jax.ShapeDtypeStruct((rows, wd), dt) for wd, dt in outs],
        compiler_params=_params(("parallel",)),
        name=f"inproj_{kind}_{'lat' if latent else 'ctx'}",
    )(*args)


def _outproj_kernel(og_ref, w_ref, x_ref, mod_ref, *rest, final):
    out = jnp.dot(og_ref[...], w_ref[...], preferred_element_type=F32)
    xn = x_ref[...] + mod_ref[:, 2 * D:3 * D] * out
    if final:
        fw_ref, y_ref = rest
        y_ref[...] = _rms(xn, fw_ref[...])
    else:
        (o_ref,) = rest
        o_ref[...] = xn


def _outproj(og, w, x, mod, latent, final_w=None):
    rows = x.shape[0]
    per_seq = (N_LAT if latent else N_CTX) // TM
    mod_row = (lambda i: (i // per_seq, 0, 0)) if latent else (lambda i: (4, 0, 0))
    in_specs = [
        pl.BlockSpec((TM, D), lambda i: (i, 0)),
        pl.BlockSpec((D, D), lambda i: (0, 0)),
        pl.BlockSpec((TM, D), lambda i: (i, 0)),
        pl.BlockSpec((None, 1, 3 * D), mod_row),
    ]
    args = [og, w, x, mod]
    if final_w is not None:
        in_specs.append(pl.BlockSpec((1, D), lambda i: (0, 0)))
        args.append(final_w.reshape(1, D))
    return pl.pallas_call(
        functools.partial(_outproj_kernel, final=final_w is not None),
        grid=(rows // TM,),
        in_specs=in_specs,
        out_specs=pl.BlockSpec((TM, D), lambda i: (i, 0)),
        out_shape=jax.ShapeDtypeStruct((rows, D), F32),
        compiler_params=_params(("parallel",)),
        name=f"outproj_{'lat' if latent else 'ctx'}",
    )(*args)


class _PerUse:
    def __init__(self, make):
        self._make = make

    def __getitem__(self, i):
        return self._make(i)


def _lane_lo(shape):
    return lax.broadcasted_iota(jnp.int32, shape, 1) < DH


SOFTMAX_BLOCK_ELEMS = 32 * 1024
SOFTMAX_WHOLE_KEYS = 512


def _softmax_rows(sq, sk):
    if sk <= SOFTMAX_WHOLE_KEYS:
        return sq
    rows = 16
    while rows * 2 <= sq and sq % (rows * 2) == 0 and rows * 2 * sk <= SOFTMAX_BLOCK_ELEMS:
        rows *= 2
    return rows


Q_TILES_PER_KV_TILE = 4
_KV_ALIGNED_HEADS = (0, 4, 1, 5, 2, 6, 3, 7, 8, 12, 9, 13, 10, 14, 11, 15)


def _gqa_block(sink_ref, q_ref, z_ref, o_ref, kv_pieces, lhs_s, s_s, e_s):
    sq = q_ref.shape[0]
    lo = _lane_lo((1, LANES))
    halves = (lo, jnp.logical_not(lo))
    n_kv_tiles = HKV_A // 2
    pieces = _PerUse(lambda j: [(k2.astype(BF16), v2.astype(BF16), valid) for k2, v2, valid in kv_pieces(j)])
    spans, off = [], 0
    for k2, _, _ in pieces[0]:
        spans.append(slice(off, off + k2.shape[0]))
        off += k2.shape[0]
    step = _softmax_rows(sq, off)
    blocks = lambda j: [(t, a) for t in range(Q_TILES_PER_KV_TILE * j, Q_TILES_PER_KV_TILE * (j + 1))
                        for a in range(2)]

    def scores_stage(j):
        for i, (t, a) in enumerate(blocks(j)):
            q2 = q_ref[:, t * LANES:(t + 1) * LANES]
            lhs_s[j, i * sq:(i + 1) * sq, :] = jnp.where(halves[a], q2, jnp.zeros_like(q2))
        for (k2, _, _), sp in zip(pieces[j], spans):
            s_s[j, :, sp] = lax.dot_general(lhs_s[j], k2, _NT, preferred_element_type=F32)

    def softmax_stage(j):
        inv = []
        for i, (t, a) in enumerate(blocks(j)):
            sink = sink_ref[2 * t + a] * LOG2E
            parts = []
            for r in range(0, sq, step):
                rws = slice(i * sq + r, i * sq + r + step)
                scores = []
                for (_, _, valid), sp in zip(pieces[j], spans):
                    s = s_s[j, rws, sp]
                    scores.append(s if valid is None
                                  else jnp.where(valid[r:r + step], s, -jnp.inf))
                m = sink
                for s in scores:
                    m = jnp.maximum(m, jnp.max(s, axis=-1, keepdims=True))
                denom = jnp.exp2(sink - m)
                for s, sp in zip(scores, spans):
                    e = jnp.exp2(s - m)
                    denom = denom + jnp.sum(e, axis=-1, keepdims=True)
                    e_s[j, rws, sp] = e.astype(BF16)
                parts.append(1.0 / denom)
            inv.append(jnp.concatenate(parts, axis=0))
        return inv

    def pv_stage(j, inv):
        pv = None
        for (_, v2, _), sp in zip(pieces[j], spans):
            part = jnp.dot(e_s[j, :, sp], v2, preferred_element_type=F32)
            pv = part if pv is None else pv + part
        for n_t in range(Q_TILES_PER_KV_TILE):
            t = Q_TILES_PER_KV_TILE * j + n_t
            i0, i1 = 2 * n_t, 2 * n_t + 1
            o2 = jnp.where(lo, pv[i0 * sq:(i0 + 1) * sq] * inv[i0], pv[i1 * sq:(i1 + 1) * sq] * inv[i1])
            cols = slice(t * LANES, (t + 1) * LANES)
            o_ref[:, cols] = (o2 * _silu(z_ref[:, cols].astype(F32))).astype(BF16)

    scores_stage(0)
    for j in range(n_kv_tiles):
        if j + 1 < n_kv_tiles:
            scores_stage(j + 1)
        pv_stage(j, softmax_stage(j))


def _gqa_scratch(sq, sk):
    stacked = 2 * Q_TILES_PER_KV_TILE * sq
    tiles = HKV_A // 2
    return [pltpu.VMEM((tiles, stacked, LANES), BF16), pltpu.VMEM((tiles, stacked, sk), F32),
            pltpu.VMEM((tiles, stacked, sk), BF16)]


def _attn_a_ctx_kernel(sink_ref, q_ref, k_ref, v_ref, z_ref, o_ref, *scratch):
    def kv_pieces(j):
        cols = slice(j * LANES, (j + 1) * LANES)
        return [(k_ref[:, cols], v_ref[:, cols], None)]
    _gqa_block(sink_ref, q_ref, z_ref, o_ref, kv_pieces, *scratch)


def _attn_a_lat_kernel(sink_ref, q_ref, k_ref, v_ref, kc_ref, vc_ref, z_ref, o_ref, *scratch):
    n = pl.program_id(1)
    span = 3 * WINDOW
    start = pl.multiple_of(jnp.clip((n - 1) * WINDOW, 0, N_LAT - span), WINDOW)
    qi = n * WINDOW + lax.broadcasted_iota(jnp.int32, (WINDOW, span), 0)
    kj = start + lax.broadcasted_iota(jnp.int32, (WINDOW, span), 1)
    valid = jnp.abs(kj - qi) <= WINDOW

    def kv_pieces(j):
        cols = slice(j * LANES, (j + 1) * LANES)
        return [(k_ref[pl.ds(start, span), cols], v_ref[pl.ds(start, span), cols], valid),
                (kc_ref[:, cols], vc_ref[:, cols], None)]
    _gqa_block(sink_ref, q_ref, z_ref, o_ref, kv_pieces, *scratch)


def _attn_a(sink, q, k, v, z, cache=None):
    smem = pl.BlockSpec(memory_space=pltpu.SMEM)
    kvw = HKV_A * DH
    if cache is None:
        return pl.pallas_call(
            _attn_a_ctx_kernel,
            grid=(N_CTX_B,),
            in_specs=[smem,
                      pl.BlockSpec((N_CTX, D), lambda b: (b, 0)),
                      pl.BlockSpec((N_CTX, kvw), lambda b: (b, 0)),
                      pl.BlockSpec((N_CTX, kvw), lambda b: (b, 0)),
                      pl.BlockSpec((N_CTX, D), lambda b: (b, 0))],
            out_specs=pl.BlockSpec((N_CTX, D), lambda b: (b, 0)),
            out_shape=jax.ShapeDtypeStruct((N_CTX_B * N_CTX, D), BF16),
            scratch_shapes=_gqa_scratch(N_CTX, N_CTX),
            compiler_params=_params(("parallel",)),
            name="attn_a_ctx",
        )(sink, q, k, v, z)
    kc, vc = cache
    nq = N_LAT // WINDOW
    return pl.pallas_call(
        _attn_a_lat_kernel,
        grid=(N_LAT_B, nq),
        in_specs=[smem,
                  pl.BlockSpec((WINDOW, D), lambda b, n: (b * nq + n, 0)),
                  pl.BlockSpec((N_LAT, kvw), lambda b, n: (b, 0)),
                  pl.BlockSpec((N_LAT, kvw), lambda b, n: (b, 0)),
                  pl.BlockSpec((PAST, kvw), lambda b, n: (b, 0)),
                  pl.BlockSpec((PAST, kvw), lambda b, n: (b, 0)),
                  pl.BlockSpec((WINDOW, D), lambda b, n: (b * nq + n, 0))],
        out_specs=pl.BlockSpec((WINDOW, D), lambda b, n: (b * nq + n, 0)),
        out_shape=jax.ShapeDtypeStruct((N_LAT_B * N_LAT, D), BF16),
        scratch_shapes=_gqa_scratch(WINDOW, 3 * WINDOW + PAST),
        compiler_params=_params(("parallel", "arbitrary")),
        name="attn_a_lat",
    )(sink, q, k, v, kc, vc, z)


def _diff_attn_block(lam_refs, subln_ref, q_ref, z_ref, o_ref, kv_pieces, lam_init, lhs_s, s_s, a_s):
    lq1, lk1, lq2, lk2 = lam_refs
    dot_exp = lambda a, c: jnp.exp(jnp.sum(a[...] * c[...], axis=-1, keepdims=True))
    lam = dot_exp(lq1, lk1) - dot_exp(lq2, lk2) + lam_init
    sq = q_ref.shape[0]
    lo = _lane_lo((1, LANES))
    n_heads = D // LANES
    pieces = _PerUse(kv_pieces)
    spans, off = [], 0
    for k2, _ in pieces[0]:
        spans.append(slice(off, off + k2.shape[0]))
        off += k2.shape[0]
    step = _softmax_rows(sq, 2 * off)

    def scores_stage(h):
        buf = h % 2
        q2 = q_ref[:, h * LANES:(h + 1) * LANES]
        lhs_s[buf, 0:sq, :] = jnp.where(lo, q2, jnp.zeros_like(q2))
        lhs_s[buf, sq:2 * sq, :] = jnp.where(lo, jnp.zeros_like(q2), q2)
        for (k2, _), sp in zip(pieces[h], spans):
            s_s[buf, :, sp] = lax.dot_general(lhs_s[buf], k2, _NT, preferred_element_type=F32)

    def softmax_stage(h):
        buf = h % 2
        inv = []
        for r in range(0, sq, step):
            exps, denoms = [], []
            for c in range(2):
                rws = slice(c * sq + r, c * sq + r + step)
                scores = [s_s[buf, rws, sp] for sp in spans]
                m = None
                for s in scores:
                    sm = jnp.max(s, axis=-1, keepdims=True)
                    m = sm if m is None else jnp.maximum(m, sm)
                es = [jnp.exp2(s - m) for s in scores]
                denom = None
                for e in es:
                    se = jnp.sum(e, axis=-1, keepdims=True)
                    denom = se if denom is None else denom + se
                exps.append(es)
                denoms.append(denom)
            ratio = lam * denoms[0] / denoms[1]
            for e1, e2, sp in zip(exps[0], exps[1], spans):
                a_s[buf, r:r + step, sp] = (e1 - ratio * e2).astype(BF16)
            inv.append(1.0 / denoms[0])
        return jnp.concatenate(inv, axis=0)

    def pv_stage(h, inv):
        buf = h % 2
        cols = slice(h * LANES, (h + 1) * LANES)
        o = None
        for (_, vx), sp in zip(pieces[h], spans):
            po = jnp.dot(a_s[buf, :, sp], vx, preferred_element_type=F32)
            o = po if o is None else o + po
        o = _rms(o * inv, subln_ref[...]) * (1.0 - lam_init)
        o_ref[:, cols] = (o * _silu(z_ref[:, cols].astype(F32))).astype(BF16)

    scores_stage(0)
    for h in range(n_heads):
        if h + 1 < n_heads:
            scores_stage(h + 1)
        pv_stage(h, softmax_stage(h))


def _diff_attn_scratch(sq, sk):
    return [pltpu.VMEM((2, 2 * sq, LANES), BF16), pltpu.VMEM((2, 2 * sq, sk), F32),
            pltpu.VMEM((2, sq, sk), BF16)]


def _attn_b_ctx_kernel(lq1, lk1, lq2, lk2, subln_ref, q_ref, k_ref, v_ref, z_ref, o_ref, *scratch, lam_init):
    def kv_pieces(h):
        cols = slice(h * LANES, (h + 1) * LANES)
        return [(k_ref[:, cols].astype(BF16), v_ref[:, cols].astype(BF16))]
    _diff_attn_block((lq1, lk1, lq2, lk2), subln_ref, q_ref, z_ref, o_ref, kv_pieces, lam_init, *scratch)


def _attn_b_lat_kernel(lq1, lk1, lq2, lk2, subln_ref, q_ref, k_ref, v_ref, kc_ref, vc_ref, z_ref, o_ref,
                       *scratch, lam_init):
    def kv_pieces(h):
        cols = slice(h * LANES, (h + 1) * LANES)
        return [(k_ref[:, cols].astype(BF16), v_ref[:, cols].astype(BF16)),
                (kc_ref[:, cols].astype(BF16), vc_ref[:, cols].astype(BF16))]
    _diff_attn_block((lq1, lk1, lq2, lk2), subln_ref, q_ref, z_ref, o_ref, kv_pieces, lam_init, *scratch)


def _attn_b(lams, subln, q, k, v, z, lam_init, cache=None):
    small = [pl.BlockSpec((1, DH), lambda *_: (0, 0))] * 4 + [pl.BlockSpec((1, 2 * DH), lambda *_: (0, 0))]
    small_args = [l.reshape(1, DH) for l in lams] + [subln.reshape(1, 2 * DH)]
    if cache is None:
        blk = pl.BlockSpec((N_CTX, D), lambda b: (b, 0))
        return pl.pallas_call(
            functools.partial(_attn_b_ctx_kernel, lam_init=lam_init),
            grid=(N_CTX_B,),
            in_specs=small + [blk] * 4,
            out_specs=blk,
            out_shape=jax.ShapeDtypeStruct((N_CTX_B * N_CTX, D), BF16),
            scratch_shapes=_diff_attn_scratch(N_CTX, N_CTX),
            compiler_params=_params(("parallel",)),
            name="attn_b_ctx",
        )(*small_args, q, k, v, z)
    kc, vc = cache
    tq = 256
    nq = N_LAT // tq
    qblk = pl.BlockSpec((tq, D), lambda b, n: (b * nq + n, 0))
    return pl.pallas_call(
        functools.partial(_attn_b_lat_kernel, lam_init=lam_init),
        grid=(N_LAT_B, nq),
        in_specs=small + [qblk,
                          pl.BlockSpec((N_LAT, D), lambda b, n: (b, 0)),
                          pl.BlockSpec((N_LAT, D), lambda b, n: (b, 0)),
                          pl.BlockSpec((PAST, D), lambda b, n: (b, 0)),
                          pl.BlockSpec((PAST, D), lambda b, n: (b, 0)),
                          qblk],
        out_specs=qblk,
        out_shape=jax.ShapeDtypeStruct((N_LAT_B * N_LAT, D), BF16),
        scratch_shapes=_diff_attn_scratch(tq, N_LAT + PAST),
        compiler_params=_params(("parallel", "arbitrary")),
        name="attn_b_lat",
    )(*small_args, q, k, v, kc, vc, z)


def _mm(a, b):
    return jnp.dot(a.astype(BF16), b.astype(BF16), preferred_element_type=F32)


def _each(fn, *lists):
    return [fn(*xs) for xs in zip(*lists)]


def _unit_tri_inverse_residuals(ls, same_blk):
    rd = [jnp.where(same_blk, -l, 0.0) for l in ls]
    lo = [jnp.where(same_blk, 0.0, l) for l in ls]
    pk = _each(_mm, rd, rd)
    for it in range(3):
        t = _each(_mm, rd, pk)
        nxt = _each(_mm, pk, pk) if it < 2 else None
        rd = _each(lambda r, p, x: r + p + x, rd, pk, t)
        pk = nxt
    m = _each(lambda x, r: x + _mm(r, x), lo, rd)
    m2 = _each(_mm, m, m)
    q = _each(lambda a, a2: a2 - a - _mm(a, a2), m, m2)
    return _each(lambda a, r: a + r + _mm(a, r), q, rd)


def _gdn_kernel(*refs, n, hp, has_s0, want_state):
    it = iter(refs)
    qp_ref, kp_ref, vp_ref, cwq_ref, cwk_ref, cwv_ref, g_ref, z_ref, onw_ref = [next(it) for _ in range(9)]
    s0_ref = next(it) if has_s0 else None
    og_ref = next(it)
    sf_ref = next(it) if want_state else None
    q_s, k_s, v_s, beta_b, gc_b, gtot_b, u_s, wq_s, intra_s, bb_s, ab_s, st_s = it
    nc = n // CHUNK
    head0 = pl.program_id(1) * hp

    row = lax.broadcasted_iota(jnp.int32, (n, LANES), 0)
    lane = lax.broadcasted_iota(jnp.int32, (n, LANES), 1)

    def conv_silu(p_ref, cw_ref, cols):
        x = p_ref[:, cols].astype(F32)
        xm1 = jnp.where(row == 0, 0.0, pltpu.roll(x, 1, 0))
        xp1 = jnp.where(row == n - 1, 0.0, pltpu.roll(x, n - 1, 0))
        return _silu(cw_ref[0:1, cols] * xm1 + cw_ref[1:2, cols] * x + cw_ref[2:3, cols] * xp1)

    def l2n(x):
        return x * lax.rsqrt(jnp.sum(x * x, axis=-1, keepdims=True) + 1e-6)

    for p in range(hp):
        cols = slice(p * LANES, (p + 1) * LANES)
        q_s[p] = l2n(conv_silu(qp_ref, cwq_ref, cols)) * (DK_C ** -0.5)
        k_s[p] = l2n(conv_silu(kp_ref, cwk_ref, cols))
        v_s[p] = conv_silu(vp_ref, cwv_ref, cols)

    gates = g_ref[...]
    local = row & (CHUNK - 1)
    pre = gates
    suf = gates
    s = 1
    while s < CHUNK:
        pre = pre + jnp.where(local >= s, pltpu.roll(pre, s, 0), 0.0)
        suf = suf + jnp.where(local < CHUNK - s, pltpu.roll(suf, n - s, 0), 0.0)
        s *= 2
    tot = pre + suf - gates

    def col(x, idx):
        picked = jnp.sum(jnp.where(lane == idx, x, 0.0), axis=1, keepdims=True)
        return jnp.broadcast_to(picked, (n, LANES))

    for p in range(hp):
        for d in range(2):
            beta_b[p, d] = col(gates, d * H_C + head0 + p)
            gc_b[p, d] = col(pre if d == 0 else suf, 2 * H_C + d * H_C + head0 + p)
            gtot_b[p, d] = col(tot, 2 * H_C + d * H_C + head0 + p)

    ii = lax.broadcasted_iota(jnp.int32, (CHUNK, CHUNK), 0)
    jj = lax.broadcasted_iota(jnp.int32, (CHUNK, CHUNK), 1)
    same_blk = (ii >> 4) == (jj >> 4)
    eye = (lax.broadcasted_iota(jnp.int32, (LANES, LANES), 0)
           == lax.broadcasted_iota(jnp.int32, (LANES, LANES), 1)).astype(BF16)
    incl = (ii >= jj, ii <= jj)
    strict = (ii > jj, ii < jj)
    heads = range(hp)
    pairs = [(p, c) for p in heads for c in range(nc)]
    chains = [(p, c, d) for p, c in pairs for d in range(2)]
    rows = lambda c: pl.ds(c * CHUNK, CHUNK)
    rows2 = lambda c: pl.ds(c * LANES, LANES)

    kc = [k_s[p, rows(c), :] for p, c in pairs]
    qc = [q_s[p, rows(c), :] for p, c in pairs]
    beta = [beta_b[p, d, rows(c), :] for p, c, d in chains]
    gc = [gc_b[p, d, rows(c), :] for p, c, d in chains]
    kb = [kc[i // 2] * b for i, b in enumerate(beta)]
    eg = [jnp.exp(g) for g in gc]
    a = [lax.dot_general(jnp.concatenate([kb[2 * j], kb[2 * j + 1], qc[j]], axis=0).astype(BF16),
                         kc[j].astype(BF16), _NT, preferred_element_type=F32)
         for j in range(len(pairs))]
    kdt = [lax.dot_general(eye, (kc[i // 2] * jnp.exp(gtot_b[p, d, rows(c), :] - gc[i])).astype(BF16), _NT,
                           preferred_element_type=F32).astype(BF16)
           for i, (p, c, d) in enumerate(chains)]
    ls = []
    for i, (p, c, d) in enumerate(chains):
        gci = gc[i][:, :CHUNK]
        gcj = jnp.sum(jnp.where(ii == jj, gci, 0.0), axis=0, keepdims=True)
        decay = jnp.where(incl[d], jnp.exp(jnp.where(incl[d], gci - gcj, 0.0)), 0.0)
        ls.append(jnp.where(strict[d], a[i // 2][d * CHUNK:(d + 1) * CHUNK] * decay, 0.0))
        intra_s[p, d, rows(c), :] = jnp.where(incl[d], a[i // 2][2 * CHUNK:] * decay, 0.0)
        wq_s[p, d, pl.ds(c * LANES + CHUNK, CHUNK), :] = (qc[i // 2] * eg[i]).astype(BF16)
    r = _unit_tri_inverse_residuals(ls, same_blk)
    rhs = [jnp.concatenate([v_s[p, rows(c), :] * b, x * e], axis=1)
           for (p, c, d), b, x, e in zip(chains, beta, kb, eg)]
    sol = _each(lambda x, y: y + _mm(x, y), r, rhs)
    for (p, c, d), x in zip(chains, sol):
        u_s[p, d, rows(c), :] = x[:, :LANES]
        wq_s[p, d, pl.ds(c * LANES, CHUNK), :] = x[:, LANES:].astype(BF16)
    ba = _each(lambda t, x: jnp.dot(t, x.astype(BF16), preferred_element_type=F32), kdt, sol)
    for (p, c, d), x in zip(chains, ba):
        bb_s[p, d, rows2(c), :] = x[:, :LANES]
        ab_s[p, d, rows2(c), :] = x[:, LANES:].astype(BF16)

    scans = [(p, d) for p in heads for d in range(2)]
    if has_s0:
        state = [s0_ref[d, p] for p, d in scans]
    else:
        state = [jnp.zeros((DK_C, LANES), F32) for _ in scans]
    for i in range(nc):
        for j, (p, d) in enumerate(scans):
            c = i if d == 0 else nc - 1 - i
            st_b = state[j].astype(BF16)
            st_s[p, d, rows2(c), :] = st_b
            eg_tot = jnp.exp(gtot_b[p, d, pl.ds(c * CHUNK, 1), :])
            state[j] = (state[j] * eg_tot + bb_s[p, d, rows2(c), :]
                        - jnp.dot(ab_s[p, d, rows2(c), :], st_b, preferred_element_type=F32))

    wo = [jnp.dot(wq_s[p, d, rows2(c), :], st_s[p, d, rows2(c), :], preferred_element_type=F32)
          for p, c, d in chains]
    vnew = [u_s[p, d, rows(c), :] - x[:CHUNK] for (p, c, d), x in zip(chains, wo)]
    o = [x[CHUNK:] + _mm(intra_s[p, d, rows(c), :], vn) for (p, c, d), x, vn in zip(chains, wo, vnew)]
    for j, (p, c) in enumerate(pairs):
        cols = slice(p * LANES, (p + 1) * LANES)
        oc = o[2 * j] + o[2 * j + 1]
        og_ref[rows(c), cols] = (_rms(oc, onw_ref[...]) * _silu(z_ref[rows(c), cols].astype(F32))).astype(BF16)
    if want_state:
        for j, (p, d) in enumerate(scans):
            sf_ref[d, p] = state[j]


def _gdn(qkv, conv_w, gates, z, onorm_w, latent, s0=None):
    n = N_LAT if latent else N_CTX
    nb = N_LAT_B if latent else N_CTX_B
    nc = n // CHUNK
    want_state = not latent
    hp = 1 if latent else 4
    groups = H_C // hp
    col = lambda off: pl.BlockSpec((n, hp * LANES), lambda b, h, off=off: (b, off * groups + h))
    cw = lambda off: pl.BlockSpec((3, hp * LANES), lambda b, h, off=off: (0, off * groups + h))
    state_spec = pl.BlockSpec((None, 2, hp, DK_C, LANES), lambda b, h: (b, 0, h, 0, 0))
    in_specs = [col(0), col(1), col(2), cw(0), cw(1), cw(2),
                pl.BlockSpec((n, LANES), lambda b, h: (b, 0)),
                pl.BlockSpec((n, hp * LANES), lambda b, h: (b, h)),
                pl.BlockSpec((1, LANES), lambda b, h: (0, 0))]
    args = [qkv, qkv, qkv, conv_w, conv_w, conv_w, gates, z, onorm_w.reshape(1, LANES)]
    if s0 is not None:
        in_specs.append(state_spec)
        args.append(s0)
    out_specs = [pl.BlockSpec((n, hp * LANES), lambda b, h: (b, h))]
    out_shape = [jax.ShapeDtypeStruct((nb * n, D), BF16)]
    if want_state:
        out_specs.append(state_spec)
        out_shape.append(jax.ShapeDtypeStruct((nb, 2, H_C, DK_C, LANES), F32))
    seq = lambda: pltpu.VMEM((hp, n, LANES), F32)
    both = lambda: pltpu.VMEM((hp, 2, n, LANES), F32)
    per_chunk = lambda dt: pltpu.VMEM((hp, 2, nc * LANES, LANES), dt)
    scratch = [seq(), seq(), seq(), both(), both(), both(), both(),
               per_chunk(BF16), pltpu.VMEM((hp, 2, n, CHUNK), F32), per_chunk(F32), per_chunk(BF16),
               per_chunk(BF16)]
    return pl.pallas_call(
        functools.partial(_gdn_kernel, n=n, hp=hp, has_s0=s0 is not None, want_state=want_state),
        grid=(nb, groups),
        in_specs=in_specs,
        out_specs=out_specs,
        out_shape=out_shape,
        scratch_shapes=scratch,
        compiler_params=_params(("parallel", "arbitrary")),
        name=f"gdn_{'lat' if latent else 'ctx'}",
    )(*args)


def _rope_tables():
    rows = N_LAT // GRID_W
    row = jnp.repeat(jnp.arange(rows), GRID_W).astype(F32)
    colp = jnp.tile(jnp.arange(GRID_W), rows).astype(F32)
    quarter = DH // 4
    inv = ROPE_BASE ** (-jnp.arange(quarter, dtype=F32) / quarter)
    ar, ac = row[:, None] * inv, colp[:, None] * inv
    cos = jnp.concatenate([jnp.cos(ar)] * 2 + [jnp.cos(ac)] * 2, axis=-1)
    sin = jnp.concatenate([-jnp.sin(ar), jnp.sin(ar), -jnp.sin(ac), jnp.sin(ac)], axis=-1)
    return jnp.tile(cos, (1, 2)), jnp.tile(sin, (1, 2))


def kernel(x_prompt, x_sample, cache_l0_k, cache_l0_v, cache_l1_k, cache_l1_v, state_l2, cache_l3_k, cache_l3_v, c, c_ctx, l0_norm_w, l0_mod_w, l0_mod_b, l0_in_w, l0_out_w, l0_sink, l1_norm_w, l1_mod_w, l1_mod_b, l1_in_w, l1_out_w, l1_lambda_q1, l1_lambda_k1, l1_lambda_q2, l1_lambda_k2, l1_subln_w, l2_norm_w, l2_mod_w, l2_mod_b, l2_in_w, l2_out_w, l2_conv_w, l2_a_log, l2_dt_bias, l2_onorm_w, l3_norm_w, l3_mod_w, l3_mod_b, l3_in_w, l3_out_w, l3_sink, final_norm_w):
    xc = x_prompt.reshape(N_CTX_B * N_CTX, D)
    xl = x_sample.reshape(N_LAT_B * N_LAT, D)
    cond = jnp.concatenate([c, c_ctx[None, :], jnp.zeros((3, D), F32)], axis=0)
    mods = _modulation(cond, (l0_mod_w, l1_mod_w, l2_mod_w, l3_mod_w), (l0_mod_b, l1_mod_b, l2_mod_b, l3_mod_b))
    rope_tabs = _rope_tables()
    kvw = HKV_A * DH
    new_state = []

    def layer_a(xc, xl, mod, norm_w, in_w, out_w, sink, cache_k, cache_v, final_w=None):
        order = jnp.array(_KV_ALIGNED_HEADS)
        by_head = lambda cols: cols.reshape(D, len(_KV_ALIGNED_HEADS), DH)[:, order].reshape(D, D)
        in_w = jnp.concatenate([by_head(in_w[:, :D]), in_w[:, D:D + 2 * kvw], by_head(in_w[:, D + 2 * kvw:])],
                               axis=1)
        out_w = out_w.reshape(len(_KV_ALIGNED_HEADS), DH, D)[order].reshape(D, D)
        sink = sink[order]
        w = in_w.astype(BF16)
        outs = lambda kv_dt: [(D, BF16), (kvw, kv_dt), (kvw, kv_dt), (D, BF16)]
        qc, kc, vc, zc = _inproj("a", xc, norm_w, mod, w, [], outs(F32), latent=False)
        ql, kl, vl, zl = _inproj("a", xl, norm_w, mod, w, [], outs(BF16), latent=True, rope_tabs=rope_tabs)
        ogc = _attn_a(sink, qc, kc, vc, zc)
        ogl = _attn_a(sink, ql, kl, vl, zl,
                      cache=(cache_k.reshape(N_LAT_B * PAST, kvw), cache_v.reshape(N_LAT_B * PAST, kvw)))
        wo = out_w.astype(BF16)
        xc = _outproj(ogc, wo, xc, mod, latent=False, final_w=final_w)
        xl = _outproj(ogl, wo, xl, mod, latent=True, final_w=final_w)
        new_state.extend([kc.reshape(N_CTX_B, N_CTX, HKV_A, DH), vc.reshape(N_CTX_B, N_CTX, HKV_A, DH)])
        return xc, xl

    xc, xl = layer_a(xc, xl, mods[0], l0_norm_w, l0_in_w, l0_out_w, l0_sink, cache_l0_k, cache_l0_v)

    lam_init = 0.8 - 0.6 * math.exp(-0.3 * 1)
    w = l1_in_w.astype(BF16)
    outs = lambda kv_dt: [(D, BF16), (D, kv_dt), (D, kv_dt), (D, BF16)]
    qc, kc, vc, zc = _inproj("b", xc, l1_norm_w, mods[1], w, [], outs(F32), latent=False)
    ql, kl, vl, zl = _inproj("b", xl, l1_norm_w, mods[1], w, [], outs(BF16), latent=True, rope_tabs=rope_tabs)
    lams = (l1_lambda_q1, l1_lambda_k1, l1_lambda_q2, l1_lambda_k2)
    ogc = _attn_b(lams, l1_subln_w, qc, kc, vc, zc, lam_init)
    ogl = _attn_b(lams, l1_subln_w, ql, kl, vl, zl, lam_init,
                  cache=(cache_l1_k.reshape(N_LAT_B * PAST, D), cache_l1_v.reshape(N_LAT_B * PAST, D)))
    wo = l1_out_w.astype(BF16)
    xc = _outproj(ogc, wo, xc, mods[1], latent=False)
    xl = _outproj(ogl, wo, xl, mods[1], latent=True)
    new_state.extend([kc.reshape(N_CTX_B, N_CTX, H_C, 2, DH), vc.reshape(N_CTX_B, N_CTX, H_C, 2 * DH)])

    w = l2_in_w.astype(BF16)
    wg = jnp.pad(l2_in_w[:, 4 * D:], ((0, 0), (0, LANES - 4 * H_C))).astype(BF16)
    lane_pad = lambda p: jnp.pad(p.reshape(1, 2 * H_C), ((0, 0), (2 * H_C, LANES - 4 * H_C)))
    extra = [wg, lane_pad(l2_a_log), lane_pad(l2_dt_bias)]
    outs = [(3 * D, BF16), (D, BF16), (LANES, F32)]
    qkvc, zc, gc = _inproj("c", xc, l2_norm_w, mods[2], w, extra, outs, latent=False)
    qkvl, zl, gl = _inproj("c", xl, l2_norm_w, mods[2], w, extra, outs, latent=True)
    ogc, st_new = _gdn(qkvc, l2_conv_w, gc, zc, l2_onorm_w, latent=False)
    (ogl,) = _gdn(qkvl, l2_conv_w, gl, zl, l2_onorm_w, latent=True, s0=state_l2)
    wo = l2_out_w.astype(BF16)
    xc = _outproj(ogc, wo, xc, mods[2], latent=False)
    xl = _outproj(ogl, wo, xl, mods[2], latent=True)
    new_state.append(st_new)

    yc, yl = layer_a(xc, xl, mods[3], l3_norm_w, l3_in_w, l3_out_w, l3_sink, cache_l3_k, cache_l3_v,
                     final_w=final_norm_w)
    return (yc.reshape(N_CTX_B, N_CTX, D), yl.reshape(N_LAT_B, N_LAT, D), *new_state)
```

```python
import functools
import math

import jax
import jax.numpy as jnp
from jax import lax
from jax.experimental import pallas as pl
from jax.experimental.pallas import tpu as pltpu

F32 = jnp.float32
BF16 = jnp.bfloat16

D = 1024
N_CTX_B, N_CTX = 16, 256
N_LAT_B, N_LAT = 4, 1024
PAST = 512
GRID_W = 64
ROPE_BASE = 10000.0
NORM_EPS = 1e-6
DH = 64
HKV_A = 4
WINDOW = 128
H_C = 8
DK_C = 128
CHUNK = 64
LANES = 128
TM = 512
LOG2E = math.log2(math.e)
Q_SCALE = DH ** -0.5 * LOG2E
VMEM_LIMIT = 48 * 1024 * 1024

_NT = (((1,), (1,)), ((), ()))
_TN = (((0,), (0,)), ((), ()))


def _sigmoid(x):
    return 1.0 / (1.0 + jnp.exp(-x))


def _silu(x):
    return x * _sigmoid(x)


def _softplus(x):
    return jnp.maximum(x, 0.0) + jnp.log1p(jnp.exp(-jnp.abs(x)))


def _rms(x, w):
    return x * lax.rsqrt(jnp.mean(x * x, axis=-1, keepdims=True) + NORM_EPS) * w


def _params(sem):
    return pltpu.CompilerParams(dimension_semantics=sem, vmem_limit_bytes=VMEM_LIMIT)


def _mod_kernel(cond_ref, w0, w1, w2, w3, b0, b1, b2, b3, o0, o1, o2, o3):
    a = _silu(cond_ref[...]).astype(BF16)
    for w, b, o in ((w0, b0, o0), (w1, b1, o1), (w2, b2, o2), (w3, b3, o3)):
        o[...] = jnp.dot(a, w[...].astype(BF16), preferred_element_type=F32) + b[...]


def _modulation(cond, mod_ws, mod_bs):
    tn = 512
    wspec = pl.BlockSpec((D, tn), lambda j: (0, j))
    bspec = pl.BlockSpec((1, tn), lambda j: (0, j))
    ospec = pl.BlockSpec((8, tn), lambda j: (0, j))
    outs = pl.pallas_call(
        _mod_kernel,
        grid=(3 * D // tn,),
        in_specs=[pl.BlockSpec((8, D), lambda j: (0, 0))] + [wspec] * 4 + [bspec] * 4,
        out_specs=[ospec] * 4,
        out_shape=[jax.ShapeDtypeStruct((8, 3 * D), F32)] * 4,
        compiler_params=_params(("arbitrary",)),
        name="adaln_mod",
    )(cond, *mod_ws, *[b.reshape(1, 3 * D) for b in mod_bs])
    return [o.reshape(8, 1, 3 * D) for o in outs]


def _adaln_h(x_ref, nw_ref, mod_ref):
    y = _rms(x_ref[...], nw_ref[...])
    return (y * (1.0 + mod_ref[:, D:2 * D]) + mod_ref[:, 0:D]).astype(BF16)


def _proj(h, w_ref, lo, hi):
    return jnp.dot(h, w_ref[:, lo:hi], preferred_element_type=F32)


def _rope(x, cos_ref, sin_ref):
    width = x.shape[1]
    lane = lax.broadcasted_iota(jnp.int32, x.shape, 1)
    partner = jnp.where((lane & 31) < 16, pltpu.roll(x, width - 16, 1), pltpu.roll(x, 16, 1))
    reps = width // LANES
    return x * jnp.tile(cos_ref[...], (1, reps)) + partner * jnp.tile(sin_ref[...], (1, reps))


def _inproj_a_kernel(*refs, rope):
    if rope:
        x_ref, nw_ref, mod_ref, w_ref, cos_ref, sin_ref, q_ref, k_ref, v_ref, z_ref = refs
    else:
        x_ref, nw_ref, mod_ref, w_ref, q_ref, k_ref, v_ref, z_ref = refs
    h = _adaln_h(x_ref, nw_ref, mod_ref)
    for j in range(2):
        acc = _proj(h, w_ref, j * 512, (j + 1) * 512)
        if rope:
            acc = _rope(acc, cos_ref, sin_ref)
        q_ref[:, j * 512:(j + 1) * 512] = (acc * Q_SCALE).astype(BF16)
    kv = _proj(h, w_ref, 1024, 1536)
    k = kv[:, :256]
    if rope:
        k = _rope(k, cos_ref, sin_ref)
    k_ref[...] = k.astype(k_ref.dtype)
    v_ref[...] = kv[:, 256:].astype(v_ref.dtype)
    for j in range(2):
        z_ref[:, j * 512:(j + 1) * 512] = _proj(h, w_ref, 1536 + j * 512, 2048 + j * 512).astype(z_ref.dtype)


def _inproj_b_kernel(*refs, rope):
    if rope:
        x_ref, nw_ref, mod_ref, w_ref, cos_ref, sin_ref, q_ref, k_ref, v_ref, z_ref = refs
    else:
        x_ref, nw_ref, mod_ref, w_ref, q_ref, k_ref, v_ref, z_ref = refs
    h = _adaln_h(x_ref, nw_ref, mod_ref)
    for j in range(2):
        sl = slice(j * 512, (j + 1) * 512)
        q = _proj(h, w_ref, j * 512, (j + 1) * 512)
        k = _proj(h, w_ref, D + j * 512, D + (j + 1) * 512)
        if rope:
            q = _rope(q, cos_ref, sin_ref)
            k = _rope(k, cos_ref, sin_ref)
        q_ref[:, sl] = (q * Q_SCALE).astype(BF16)
        k_ref[:, sl] = k.astype(k_ref.dtype)
        v_ref[:, sl] = _proj(h, w_ref, 2 * D + j * 512, 2 * D + (j + 1) * 512).astype(v_ref.dtype)
        z_ref[:, sl] = _proj(h, w_ref, 3 * D + j * 512, 3 * D + (j + 1) * 512).astype(z_ref.dtype)


def _inproj_c_kernel(x_ref, nw_ref, mod_ref, w_ref, wg_ref, alog_ref, dtb_ref, qkv_ref, z_ref, g_ref):
    h = _adaln_h(x_ref, nw_ref, mod_ref)
    for j in range(6):
        qkv_ref[:, j * 512:(j + 1) * 512] = _proj(h, w_ref, j * 512, (j + 1) * 512).astype(qkv_ref.dtype)
    for j in range(2):
        z_ref[:, j * 512:(j + 1) * 512] = _proj(h, w_ref, 3 * D + j * 512, 3 * D + (j + 1) * 512).astype(z_ref.dtype)
    acc = jnp.dot(h, wg_ref[...], preferred_element_type=F32)
    lane = lax.broadcasted_iota(jnp.int32, acc.shape, 1)
    g = -jnp.exp(alog_ref[...]) * _softplus(acc + dtb_ref[...])
    g_ref[...] = jnp.where(lane < 2 * H_C, _sigmoid(acc), g)


def _inproj(kind, x, norm_w, mod, w, extra_in, outs, latent, rope_tabs=None):
    rows = x.shape[0]
    per_seq = (N_LAT if latent else N_CTX) // TM
    mod_row = (lambda i: (i // per_seq, 0, 0)) if latent else (lambda i: (4, 0, 0))
    in_specs = [
        pl.BlockSpec((TM, D), lambda i: (i, 0)),
        pl.BlockSpec((1, D), lambda i: (0, 0)),
        pl.BlockSpec((None, 1, 3 * D), mod_row),
        pl.BlockSpec(w.shape, lambda i: (0, 0)),
    ]
    args = [x, norm_w.reshape(1, D), mod, w]
    for e in extra_in:
        in_specs.append(pl.BlockSpec(e.shape, lambda i: (0, 0)))
        args.append(e)
    rope = rope_tabs is not None
    if rope:
        for t in rope_tabs:
            in_specs.append(pl.BlockSpec((TM, LANES), lambda i: (i % per_seq, 0)))
            args.append(t)
    if kind == "a":
        body = functools.partial(_inproj_a_kernel, rope=rope)
    elif kind == "b":
        body = functools.partial(_inproj_b_kernel, rope=rope)
    else:
        body = _inproj_c_kernel
    return pl.pallas_call(
        body,
        grid=(rows // TM,),
        in_specs=in_specs,
        out_specs=[pl.BlockSpec((TM, wd), lambda i: (i, 0)) for wd, _ in outs],
        out_shape=[jax.ShapeDtypeStruct((rows, wd), dt) for wd, dt in outs],
        compiler_params=_params(("parallel",)),
        name=f"inproj_{kind}_{'lat' if latent else 'ctx'}",
    )(*args)


def _outproj_kernel(og_ref, w_ref, x_ref, mod_ref, *rest, final):
    out = jnp.dot(og_ref[...], w_ref[...], preferred_element_type=F32)
    xn = x_ref[...] + mod_ref[:, 2 * D:3 * D] * out
    if final:
        fw_ref, y_ref = rest
        y_ref[...] = _rms(xn, fw_ref[...])
    else:
        (o_ref,) = rest
        o_ref[...] = xn


def _outproj(og, w, x, mod, latent, final_w=None):
    rows = x.shape[0]
    per_seq = (N_LAT if latent else N_CTX) // TM
    mod_row = (lambda i: (i // per_seq, 0, 0)) if latent else (lambda i: (4, 0, 0))
    in_specs = [
        pl.BlockSpec((TM, D), lambda i: (i, 0)),
        pl.BlockSpec((D, D), lambda i: (0, 0)),
        pl.BlockSpec((TM, D), lambda i: (i, 0)),
        pl.BlockSpec((None, 1, 3 * D), mod_row),
    ]
    args = [og, w, x, mod]
    if final_w is not None:
        in_specs.append(pl.BlockSpec((1, D), lambda i: (0, 0)))
        args.append(final_w.reshape(1, D))
    return pl.pallas_call(
        functools.partial(_outproj_kernel, final=final_w is not None),
        grid=(rows // TM,),
        in_specs=in_specs,
        out_specs=pl.BlockSpec((TM, D), lambda i: (i, 0)),
        out_shape=jax.ShapeDtypeStruct((rows, D), F32),
        compiler_params=_params(("parallel",)),
        name=f"outproj_{'lat' if latent else 'ctx'}",
    )(*args)


class _PerUse:
    def __init__(self, make):
        self._make = make

    def __getitem__(self, i):
        return self._make(i)


def _lane_lo(shape):
    return lax.broadcasted_iota(jnp.int32, shape, 1) < DH


SOFTMAX_BLOCK_ELEMS = 32 * 1024
SOFTMAX_WHOLE_KEYS = 512


def _softmax_rows(sq, sk):
    if sk <= SOFTMAX_WHOLE_KEYS:
        return sq
    rows = 16
    while rows * 2 <= sq and sq % (rows * 2) == 0 and rows * 2 * sk <= SOFTMAX_BLOCK_ELEMS:
        rows *= 2
    return rows


Q_TILES_PER_KV_TILE = 4
_KV_ALIGNED_HEADS = (0, 4, 1, 5, 2, 6, 3, 7, 8, 12, 9, 13, 10, 14, 11, 15)


def _gqa_block(sink_ref, q_ref, z_ref, o_ref, kv_pieces, lhs_s, s_s, e_s):
    sq = q_ref.shape[0]
    lo = _lane_lo((1, LANES))
    halves = (lo, jnp.logical_not(lo))
    n_kv_tiles = HKV_A // 2
    pieces = _PerUse(lambda j: [(k2.astype(BF16), v2.astype(BF16), valid) for k2, v2, valid in kv_pieces(j)])
    spans, off = [], 0
    for k2, _, _ in pieces[0]:
        spans.append(slice(off, off + k2.shape[0]))
        off += k2.shape[0]
    step = _softmax_rows(sq, off)
    blocks = lambda j: [(t, a) for t in range(Q_TILES_PER_KV_TILE * j, Q_TILES_PER_KV_TILE * (j + 1))
                        for a in range(2)]

    def scores_stage(j):
        for i, (t, a) in enumerate(blocks(j)):
            q2 = q_ref[:, t * LANES:(t + 1) * LANES]
            lhs_s[j, i * sq:(i + 1) * sq, :] = jnp.where(halves[a], q2, jnp.zeros_like(q2))
        for (k2, _, _), sp in zip(pieces[j], spans):
            s_s[j, :, sp] = lax.dot_general(lhs_s[j], k2, _NT, preferred_element_type=F32)

    def softmax_stage(j):
        inv = []
        for i, (t, a) in enumerate(blocks(j)):
            sink = sink_ref[2 * t + a] * LOG2E
            parts = []
            for r in range(0, sq, step):
                rws = slice(i * sq + r, i * sq + r + step)
                scores = []
                for (_, _, valid), sp in zip(pieces[j], spans):
                    s = s_s[j, rws, sp]
                    scores.append(s if valid is None
                                  else jnp.where(valid[r:r + step], s, -jnp.inf))
                m = sink
                for s in scores:
                    m = jnp.maximum(m, jnp.max(s, axis=-1, keepdims=True))
                denom = jnp.exp2(sink - m)
                for s, sp in zip(scores, spans):
                    e = jnp.exp2(s - m)
                    denom = denom + jnp.sum(e, axis=-1, keepdims=True)
                    e_s[j, rws, sp] = e.astype(BF16)
                parts.append(1.0 / denom)
            inv.append(jnp.concatenate(parts, axis=0))
        return inv

    def pv_stage(j, inv):
        pv = None
        for (_, v2, _), sp in zip(pieces[j], spans):
            part = jnp.dot(e_s[j, :, sp], v2, preferred_element_type=F32)
            pv = part if pv is None else pv + part
        for n_t in range(Q_TILES_PER_KV_TILE):
            t = Q_TILES_PER_KV_TILE * j + n_t
            i0, i1 = 2 * n_t, 2 * n_t + 1
            o2 = jnp.where(lo, pv[i0 * sq:(i0 + 1) * sq] * inv[i0], pv[i1 * sq:(i1 + 1) * sq] * inv[i1])
            cols = slice(t * LANES, (t + 1) * LANES)
            o_ref[:, cols] = (o2 * _silu(z_ref[:, cols].astype(F32))).astype(BF16)

    scores_stage(0)
    for j in range(n_kv_tiles):
        if j + 1 < n_kv_tiles:
            scores_stage(j + 1)
        pv_stage(j, softmax_stage(j))


def _gqa_scratch(sq, sk):
    stacked = 2 * Q_TILES_PER_KV_TILE * sq
    tiles = HKV_A // 2
    return [pltpu.VMEM((tiles, stacked, LANES), BF16), pltpu.VMEM((tiles, stacked, sk), F32),
            pltpu.VMEM((tiles, stacked, sk), BF16)]


def _attn_a_ctx_kernel(sink_ref, q_ref, k_ref, v_ref, z_ref, o_ref, *scratch):
    def kv_pieces(j):
        cols = slice(j * LANES, (j + 1) * LANES)
        return [(k_ref[:, cols], v_ref[:, cols], None)]
    _gqa_block(sink_ref, q_ref, z_ref, o_ref, kv_pieces, *scratch)


def _attn_a_lat_kernel(sink_ref, q_ref, k_ref, v_ref, kc_ref, vc_ref, z_ref, o_ref, *scratch):
    n = pl.program_id(1)
    span = 3 * WINDOW
    start = pl.multiple_of(jnp.clip((n - 1) * WINDOW, 0, N_LAT - span), WINDOW)
    qi = n * WINDOW + lax.broadcasted_iota(jnp.int32, (WINDOW, span), 0)
    kj = start + lax.broadcasted_iota(jnp.int32, (WINDOW, span), 1)
    valid = jnp.abs(kj - qi) <= WINDOW

    def kv_pieces(j):
        cols = slice(j * LANES, (j + 1) * LANES)
        return [(k_ref[pl.ds(start, span), cols], v_ref[pl.ds(start, span), cols], valid),
                (kc_ref[:, cols], vc_ref[:, cols], None)]
    _gqa_block(sink_ref, q_ref, z_ref, o_ref, kv_pieces, *scratch)


def _attn_a(sink, q, k, v, z, cache=None):
    smem = pl.BlockSpec(memory_space=pltpu.SMEM)
    kvw = HKV_A * DH
    if cache is None:
        return pl.pallas_call(
            _attn_a_ctx_kernel,
            grid=(N_CTX_B,),
            in_specs=[smem,
                      pl.BlockSpec((N_CTX, D), lambda b: (b, 0)),
                      pl.BlockSpec((N_CTX, kvw), lambda b: (b, 0)),
                      pl.BlockSpec((N_CTX, kvw), lambda b: (b, 0)),
                      pl.BlockSpec((N_CTX, D), lambda b: (b, 0))],
            out_specs=pl.BlockSpec((N_CTX, D), lambda b: (b, 0)),
            out_shape=jax.ShapeDtypeStruct((N_CTX_B * N_CTX, D), BF16),
            scratch_shapes=_gqa_scratch(N_CTX, N_CTX),
            compiler_params=_params(("parallel",)),
            name="attn_a_ctx",
        )(sink, q, k, v, z)
    kc, vc = cache
    nq = N_LAT // WINDOW
    return pl.pallas_call(
        _attn_a_lat_kernel,
        grid=(N_LAT_B, nq),
        in_specs=[smem,
                  pl.BlockSpec((WINDOW, D), lambda b, n: (b * nq + n, 0)),
                  pl.BlockSpec((N_LAT, kvw), lambda b, n: (b, 0)),
                  pl.BlockSpec((N_LAT, kvw), lambda b, n: (b, 0)),
                  pl.BlockSpec((PAST, kvw), lambda b, n: (b, 0)),
                  pl.BlockSpec((PAST, kvw), lambda b, n: (b, 0)),
                  pl.BlockSpec((WINDOW, D), lambda b, n: (b * nq + n, 0))],
        out_specs=pl.BlockSpec((WINDOW, D), lambda b, n: (b * nq + n, 0)),
        out_shape=jax.ShapeDtypeStruct((N_LAT_B * N_LAT, D), BF16),
        scratch_shapes=_gqa_scratch(WINDOW, 3 * WINDOW + PAST),
        compiler_params=_params(("parallel", "arbitrary")),
        name="attn_a_lat",
    )(sink, q, k, v, kc, vc, z)


def _diff_attn_block(lam_refs, subln_ref, q_ref, z_ref, o_ref, kv_pieces, lam_init, lhs_s, s_s, a_s):
    lq1, lk1, lq2, lk2 = lam_refs
    dot_exp = lambda a, c: jnp.exp(jnp.sum(a[...] * c[...], axis=-1, keepdims=True))
    lam = dot_exp(lq1, lk1) - dot_exp(lq2, lk2) + lam_init
    sq = q_ref.shape[0]
    lo = _lane_lo((1, LANES))
    n_heads = D // LANES
    pieces = _PerUse(kv_pieces)
    spans, off = [], 0
    for k2, _ in pieces[0]:
        spans.append(slice(off, off + k2.shape[0]))
        off += k2.shape[0]
    step = _softmax_rows(sq, 2 * off)

    def scores_stage(h):
        buf = h % 2
        q2 = q_ref[:, h * LANES:(h + 1) * LANES]
        lhs_s[buf, 0:sq, :] = jnp.where(lo, q2, jnp.zeros_like(q2))
        lhs_s[buf, sq:2 * sq, :] = jnp.where(lo, jnp.zeros_like(q2), q2)
        for (k2, _), sp in zip(pieces[h], spans):
            s_s[buf, :, sp] = lax.dot_general(lhs_s[buf], k2, _NT, preferred_element_type=F32)

    def softmax_stage(h):
        buf = h % 2
        inv = []
        for r in range(0, sq, step):
            exps, denoms = [], []
            for c in range(2):
                rws = slice(c * sq + r, c * sq + r + step)
                scores = [s_s[buf, rws, sp] for sp in spans]
                m = None
                for s in scores:
                    sm = jnp.max(s, axis=-1, keepdims=True)
                    m = sm if m is None else jnp.maximum(m, sm)
                es = [jnp.exp2(s - m) for s in scores]
                denom = None
                for e in es:
                    se = jnp.sum(e, axis=-1, keepdims=True)
                    denom = se if denom is None else denom + se
                exps.append(es)
                denoms.append(denom)
            ratio = lam * denoms[0] / denoms[1]
            for e1, e2, sp in zip(exps[0], exps[1], spans):
                a_s[buf, r:r + step, sp] = (e1 - ratio * e2).astype(BF16)
            inv.append(1.0 / denoms[0])
        return jnp.concatenate(inv, axis=0)

    def pv_stage(h, inv):
        buf = h % 2
        cols = slice(h * LANES, (h + 1) * LANES)
        o = None
        for (_, vx), sp in zip(pieces[h], spans):
            po = jnp.dot(a_s[buf, :, sp], vx, preferred_element_type=F32)
            o = po if o is None else o + po
        o = _rms(o * inv, subln_ref[...]) * (1.0 - lam_init)
        o_ref[:, cols] = (o * _silu(z_ref[:, cols].astype(F32))).astype(BF16)

    scores_stage(0)
    for h in range(n_heads):
        if h + 1 < n_heads:
            scores_stage(h + 1)
        pv_stage(h, softmax_stage(h))


def _diff_attn_scratch(sq, sk):
    return [pltpu.VMEM((2, 2 * sq, LANES), BF16), pltpu.VMEM((2, 2 * sq, sk), F32),
            pltpu.VMEM((2, sq, sk), BF16)]


def _attn_b_ctx_kernel(lq1, lk1, lq2, lk2, subln_ref, q_ref, k_ref, v_ref, z_ref, o_ref, *scratch, lam_init):
    def kv_pieces(h):
        cols = slice(h * LANES, (h + 1) * LANES)
        return [(k_ref[:, cols].astype(BF16), v_ref[:, cols].astype(BF16))]
    _diff_attn_block((lq1, lk1, lq2, lk2), subln_ref, q_ref, z_ref, o_ref, kv_pieces, lam_init, *scratch)


def _attn_b_lat_kernel(lq1, lk1, lq2, lk2, subln_ref, q_ref, k_ref, v_ref, kc_ref, vc_ref, z_ref, o_ref,
                       *scratch, lam_init):
    def kv_pieces(h):
        cols = slice(h * LANES, (h + 1) * LANES)
        return [(k_ref[:, cols].astype(BF16), v_ref[:, cols].astype(BF16)),
                (kc_ref[:, cols].astype(BF16), vc_ref[:, cols].astype(BF16))]
    _diff_attn_block((lq1, lk1, lq2, lk2), subln_ref, q_ref, z_ref, o_ref, kv_pieces, lam_init, *scratch)


def _attn_b(lams, subln, q, k, v, z, lam_init, cache=None):
    small = [pl.BlockSpec((1, DH), lambda *_: (0, 0))] * 4 + [pl.BlockSpec((1, 2 * DH), lambda *_: (0, 0))]
    small_args = [l.reshape(1, DH) for l in lams] + [subln.reshape(1, 2 * DH)]
    if cache is None:
        blk = pl.BlockSpec((N_CTX, D), lambda b: (b, 0))
        return pl.pallas_call(
            functools.partial(_attn_b_ctx_kernel, lam_init=lam_init),
            grid=(N_CTX_B,),
            in_specs=small + [blk] * 4,
            out_specs=blk,
            out_shape=jax.ShapeDtypeStruct((N_CTX_B * N_CTX, D), BF16),
            scratch_shapes=_diff_attn_scratch(N_CTX, N_CTX),
            compiler_params=_params(("parallel",)),
            name="attn_b_ctx",
        )(*small_args, q, k, v, z)
    kc, vc = cache
    tq = 256
    nq = N_LAT // tq
    qblk = pl.BlockSpec((tq, D), lambda b, n: (b * nq + n, 0))
    return pl.pallas_call(
        functools.partial(_attn_b_lat_kernel, lam_init=lam_init),
        grid=(N_LAT_B, nq),
        in_specs=small + [qblk,
                          pl.BlockSpec((N_LAT, D), lambda b, n: (b, 0)),
                          pl.BlockSpec((N_LAT, D), lambda b, n: (b, 0)),
                          pl.BlockSpec((PAST, D), lambda b, n: (b, 0)),
                          pl.BlockSpec((PAST, D), lambda b, n: (b, 0)),
                          qblk],
        out_specs=qblk,
        out_shape=jax.ShapeDtypeStruct((N_LAT_B * N_LAT, D), BF16),
        scratch_shapes=_diff_attn_scratch(tq, N_LAT + PAST),
        compiler_params=_params(("parallel", "arbitrary")),
        name="attn_b_lat",
    )(*small_args, q, k, v, kc, vc, z)


def _mm(a, b):
    return jnp.dot(a.astype(BF16), b.astype(BF16), preferred_element_type=F32)


def _each(fn, *lists):
    return [fn(*xs) for xs in zip(*lists)]


def _unit_tri_inverse_residuals(ls, same_blk, _mm):
    rd = [jnp.where(same_blk, -l, 0.0) for l in ls]
    lo = [jnp.where(same_blk, 0.0, l) for l in ls]
    pk = _each(_mm, rd, rd)
    for it in range(3):
        t = _each(_mm, rd, pk)
        nxt = _each(_mm, pk, pk) if it < 2 else None
        rd = _each(lambda r, p, x: r + p + x, rd, pk, t)
        pk = nxt
    m = _each(lambda x, r: x + _mm(r, x), lo, rd)
    m2 = _each(_mm, m, m)
    q = _each(lambda a, a2: a2 - a - _mm(a, a2), m, m2)
    return _each(lambda a, r: a + r + _mm(a, r), q, rd)


def _gdn_kernel(*refs, n, hp, has_s0, want_state):
    it = iter(refs)
    qp_ref, kp_ref, vp_ref, cwq_ref, cwk_ref, cwv_ref, g_ref, z_ref, onw_ref = [next(it) for _ in range(9)]
    s0_ref = next(it) if has_s0 else None
    og_ref = next(it)
    sf_ref = next(it) if want_state else None
    q_s, k_s, v_s, beta_b, gc_b, gtot_b, u_s, wq_s, intra_s, bb_s, ab_s, st_s = it
    nc = n // CHUNK
    head0 = pl.program_id(1) * hp

    row = lax.broadcasted_iota(jnp.int32, (n, LANES), 0)
    lane = lax.broadcasted_iota(jnp.int32, (n, LANES), 1)

    def conv_silu(p_ref, cw_ref, cols):
        x = p_ref[:, cols].astype(F32)
        xm1 = jnp.where(row == 0, 0.0, pltpu.roll(x, 1, 0))
        xp1 = jnp.where(row == n - 1, 0.0, pltpu.roll(x, n - 1, 0))
        return _silu(cw_ref[0:1, cols] * xm1 + cw_ref[1:2, cols] * x + cw_ref[2:3, cols] * xp1)

    def l2n(x):
        return x * lax.rsqrt(jnp.sum(x * x, axis=-1, keepdims=True) + 1e-6)

    for p in range(hp):
        cols = slice(p * LANES, (p + 1) * LANES)
        q_s[p] = l2n(conv_silu(qp_ref, cwq_ref, cols)) * (DK_C ** -0.5)
        k_s[p] = l2n(conv_silu(kp_ref, cwk_ref, cols))
        v_s[p] = conv_silu(vp_ref, cwv_ref, cols)

    gates = g_ref[...]
    local = row & (CHUNK - 1)
    pre = gates
    suf = gates
    s = 1
    while s < CHUNK:
        pre = pre + jnp.where(local >= s, pltpu.roll(pre, s, 0), 0.0)
        suf = suf + jnp.where(local < CHUNK - s, pltpu.roll(suf, n - s, 0), 0.0)
        s *= 2
    tot = pre + suf - gates

    def col(x, idx):
        picked = jnp.sum(jnp.where(lane == idx, x, 0.0), axis=1, keepdims=True)
        return jnp.broadcast_to(picked, (n, LANES))

    for p in range(hp):
        for d in range(2):
            beta_b[p, d] = col(gates, d * H_C + head0 + p)
            gc_b[p, d] = col(pre if d == 0 else suf, 2 * H_C + d * H_C + head0 + p)
            gtot_b[p, d] = col(tot, 2 * H_C + d * H_C + head0 + p)

    ii = lax.broadcasted_iota(jnp.int32, (CHUNK, LANES), 0)
    lane2 = lax.broadcasted_iota(jnp.int32, (CHUNK, LANES), 1)
    fwd = lane2 < CHUNK
    jj = lane2 & (CHUNK - 1)
    same_blk = (ii >> 4) == (jj >> 4)
    diag = ii == jj
    ahead = jnp.where(fwd, ii - jj, jj - ii)
    incl = ahead >= 0
    strict = ahead > 0
    heads = range(hp)
    pairs = [(p, c) for p in heads for c in range(nc)]
    chains = [(p, c, d) for p, c in pairs for d in range(2)]
    rows = lambda c: pl.ds(c * CHUNK, CHUNK)
    rows2 = lambda c: pl.ds(c * LANES, LANES)
    zeros_b = jnp.zeros((CHUNK, LANES), BF16)

    def block_diag(x2):
        return jnp.concatenate([jnp.where(fwd, x2, 0.0), jnp.where(fwd, 0.0, x2)], axis=0).astype(BF16)

    def packed_mm(x2, y2):
        return jnp.dot(x2.astype(BF16), block_diag(y2), preferred_element_type=F32)

    kc = [k_s[p, rows(c), :] for p, c in pairs]
    qc = [q_s[p, rows(c), :] for p, c in pairs]
    beta = [beta_b[p, d, rows(c), :] for p, c, d in chains]
    gc = [gc_b[p, d, rows(c), :] for p, c, d in chains]
    kb = [kc[i // 2] * b for i, b in enumerate(beta)]
    eg = [jnp.exp(g) for g in gc]
    a = []
    for j in range(len(pairs)):
        k_b = kc[j].astype(BF16)
        lhs = jnp.concatenate([jnp.concatenate([kb[2 * j], kb[2 * j + 1]], axis=1),
                               jnp.concatenate([qc[j], qc[j]], axis=1)], axis=0).astype(BF16)
        rhs_nt = jnp.concatenate([jnp.concatenate([k_b, zeros_b], axis=1),
                                  jnp.concatenate([zeros_b, k_b], axis=1)], axis=0)
        a.append(lax.dot_general(lhs, rhs_nt, _NT, preferred_element_type=F32))
    kd = [(kc[i // 2] * jnp.exp(gtot_b[p, d, rows(c), :] - gc[i])).astype(BF16)
          for i, (p, c, d) in enumerate(chains)]
    ls = []
    for j, (p, c) in enumerate(pairs):
        gci = jnp.where(fwd, gc[2 * j], gc[2 * j + 1])
        gcj = jnp.sum(jnp.where(diag, gci, 0.0), axis=0, keepdims=True)
        decay = jnp.where(incl, jnp.exp(jnp.where(incl, gci - gcj, 0.0)), 0.0)
        ls.append(jnp.where(strict, a[j][:CHUNK] * decay, 0.0))
        intra_s[p, rows(c), :] = jnp.where(incl, a[j][CHUNK:] * decay, 0.0)
        for d in range(2):
            wq_s[p, d, pl.ds(c * LANES + CHUNK, CHUNK), :] = (qc[j] * eg[2 * j + d]).astype(BF16)
    r = _unit_tri_inverse_residuals(ls, same_blk, packed_mm)
    rhs = [jnp.concatenate([v_s[p, rows(c), :] * b, x * e], axis=1)
           for (p, c, d), b, x, e in zip(chains, beta, kb, eg)]
    zeros_w = jnp.zeros((CHUNK, 2 * LANES), BF16)
    sol = []
    for j in range(len(pairs)):
        rf, rb = rhs[2 * j], rhs[2 * j + 1]
        both = jnp.concatenate([jnp.concatenate([rf.astype(BF16), zeros_w], axis=1),
                                jnp.concatenate([zeros_w, rb.astype(BF16)], axis=1)], axis=0)
        x = jnp.dot(r[j].astype(BF16), both, preferred_element_type=F32)
        sol.extend([rf + x[:, :2 * LANES], rb + x[:, 2 * LANES:]])
    for (p, c, d), x in zip(chains, sol):
        u_s[p, d, rows(c), :] = x[:, :LANES]
        wq_s[p, d, pl.ds(c * LANES, CHUNK), :] = x[:, LANES:].astype(BF16)
    ba = _each(lambda t, x: lax.dot_general(t, x.astype(BF16), _TN, preferred_element_type=F32),
               kd, sol)
    for (p, c, d), x in zip(chains, ba):
        bb_s[p, d, rows2(c), :] = x[:, :LANES]
        ab_s[p, d, rows2(c), :] = x[:, LANES:].astype(BF16)

    scans = [(p, d) for p in heads for d in range(2)]
    if has_s0:
        state = [s0_ref[d, p] for p, d in scans]
    else:
        state = [jnp.zeros((DK_C, LANES), F32) for _ in scans]
    for i in range(nc):
        for j, (p, d) in enumerate(scans):
            c = i if d == 0 else nc - 1 - i
            st_b = state[j].astype(BF16)
            st_s[p, d, rows2(c), :] = st_b
            eg_tot = jnp.exp(gtot_b[p, d, pl.ds(c * CHUNK, 1), :])
            state[j] = (state[j] * eg_tot + bb_s[p, d, rows2(c), :]
                        - jnp.dot(ab_s[p, d, rows2(c), :], st_b, preferred_element_type=F32))

    wo = [jnp.dot(wq_s[p, d, rows2(c), :], st_s[p, d, rows2(c), :], preferred_element_type=F32)
          for p, c, d in chains]
    vnew = [u_s[p, d, rows(c), :] - x[:CHUNK] for (p, c, d), x in zip(chains, wo)]
    for j, (p, c) in enumerate(pairs):
        cols = slice(p * LANES, (p + 1) * LANES)
        oc = (wo[2 * j][CHUNK:] + wo[2 * j + 1][CHUNK:]
              + _mm(intra_s[p, rows(c), :], jnp.concatenate([vnew[2 * j], vnew[2 * j + 1]], axis=0)))
        og_ref[rows(c), cols] = (_rms(oc, onw_ref[...]) * _silu(z_ref[rows(c), cols].astype(F32))).astype(BF16)
    if want_state:
        for j, (p, d) in enumerate(scans):
            sf_ref[d, p] = state[j]


def _gdn(qkv, conv_w, gates, z, onorm_w, latent, s0=None):
    n = N_LAT if latent else N_CTX
    nb = N_LAT_B if latent else N_CTX_B
    nc = n // CHUNK
    want_state = not latent
    hp = 1 if latent else 4
    groups = H_C // hp
    col = lambda off: pl.BlockSpec((n, hp * LANES), lambda b, h, off=off: (b, off * groups + h))
    cw = lambda off: pl.BlockSpec((3, hp * LANES), lambda b, h, off=off: (0, off * groups + h))
    state_spec = pl.BlockSpec((None, 2, hp, DK_C, LANES), lambda b, h: (b, 0, h, 0, 0))
    in_specs = [col(0), col(1), col(2), cw(0), cw(1), cw(2),
                pl.BlockSpec((n, LANES), lambda b, h: (b, 0)),
                pl.BlockSpec((n, hp * LANES), lambda b, h: (b, h)),
                pl.BlockSpec((1, LANES), lambda b, h: (0, 0))]
    args = [qkv, qkv, qkv, conv_w, conv_w, conv_w, gates, z, onorm_w.reshape(1, LANES)]
    if s0 is not None:
        in_specs.append(state_spec)
        args.append(s0)
    out_specs = [pl.BlockSpec((n, hp * LANES), lambda b, h: (b, h))]
    out_shape = [jax.ShapeDtypeStruct((nb * n, D), BF16)]
    if want_state:
        out_specs.append(state_spec)
        out_shape.append(jax.ShapeDtypeStruct((nb, 2, H_C, DK_C, LANES), F32))
    seq = lambda: pltpu.VMEM((hp, n, LANES), F32)
    both = lambda: pltpu.VMEM((hp, 2, n, LANES), F32)
    per_chunk = lambda dt: pltpu.VMEM((hp, 2, nc * LANES, LANES), dt)
    scratch = [seq(), seq(), seq(), both(), both(), both(), both(),
               per_chunk(BF16), seq(), per_chunk(F32), per_chunk(BF16),
               per_chunk(BF16)]
    return pl.pallas_call(
        functools.partial(_gdn_kernel, n=n, hp=hp, has_s0=s0 is not None, want_state=want_state),
        grid=(nb, groups),
        in_specs=in_specs,
        out_specs=out_specs,
        out_shape=out_shape,
        scratch_shapes=scratch,
        compiler_params=_params(("parallel", "arbitrary")),
        name=f"gdn_{'lat' if latent else 'ctx'}",
    )(*args)


def _rope_tables():
    rows = N_LAT // GRID_W
    row = jnp.repeat(jnp.arange(rows), GRID_W).astype(F32)
    colp = jnp.tile(jnp.arange(GRID_W), rows).astype(F32)
    quarter = DH // 4
    inv = ROPE_BASE ** (-jnp.arange(quarter, dtype=F32) / quarter)
    ar, ac = row[:, None] * inv, colp[:, None] * inv
    cos = jnp.concatenate([jnp.cos(ar)] * 2 + [jnp.cos(ac)] * 2, axis=-1)
    sin = jnp.concatenate([-jnp.sin(ar), jnp.sin(ar), -jnp.sin(ac), jnp.sin(ac)], axis=-1)
    return jnp.tile(cos, (1, 2)), jnp.tile(sin, (1, 2))


def kernel(x_prompt, x_sample, cache_l0_k, cache_l0_v, cache_l1_k, cache_l1_v, state_l2, cache_l3_k, cache_l3_v, c, c_ctx, l0_norm_w, l0_mod_w, l0_mod_b, l0_in_w, l0_out_w, l0_sink, l1_norm_w, l1_mod_w, l1_mod_b, l1_in_w, l1_out_w, l1_lambda_q1, l1_lambda_k1, l1_lambda_q2, l1_lambda_k2, l1_subln_w, l2_norm_w, l2_mod_w, l2_mod_b, l2_in_w, l2_out_w, l2_conv_w, l2_a_log, l2_dt_bias, l2_onorm_w, l3_norm_w, l3_mod_w, l3_mod_b, l3_in_w, l3_out_w, l3_sink, final_norm_w):
    xc = x_prompt.reshape(N_CTX_B * N_CTX, D)
    xl = x_sample.reshape(N_LAT_B * N_LAT, D)
    cond = jnp.concatenate([c, c_ctx[None, :], jnp.zeros((3, D), F32)], axis=0)
    mods = _modulation(cond, (l0_mod_w, l1_mod_w, l2_mod_w, l3_mod_w), (l0_mod_b, l1_mod_b, l2_mod_b, l3_mod_b))
    rope_tabs = _rope_tables()
    kvw = HKV_A * DH
    new_state = []

    def layer_a(xc, xl, mod, norm_w, in_w, out_w, sink, cache_k, cache_v, final_w=None):
        order = jnp.array(_KV_ALIGNED_HEADS)
        by_head = lambda cols: cols.reshape(D, len(_KV_ALIGNED_HEADS), DH)[:, order].reshape(D, D)
        in_w = jnp.concatenate([by_head(in_w[:, :D]), in_w[:, D:D + 2 * kvw], by_head(in_w[:, D + 2 * kvw:])],
                               axis=1)
        out_w = out_w.reshape(len(_KV_ALIGNED_HEADS), DH, D)[order].reshape(D, D)
        sink = sink[order]
        w = in_w.astype(BF16)
        outs = lambda kv_dt: [(D, BF16), (kvw, kv_dt), (kvw, kv_dt), (D, BF16)]
        qc, kc, vc, zc = _inproj("a", xc, norm_w, mod, w, [], outs(F32), latent=False)
        ql, kl, vl, zl = _inproj("a", xl, norm_w, mod, w, [], outs(BF16), latent=True, rope_tabs=rope_tabs)
        ogc = _attn_a(sink, qc, kc, vc, zc)
        ogl = _attn_a(sink, ql, kl, vl, zl,
                      cache=(cache_k.reshape(N_LAT_B * PAST, kvw), cache_v.reshape(N_LAT_B * PAST, kvw)))
        wo = out_w.astype(BF16)
        xc = _outproj(ogc, wo, xc, mod, latent=False, final_w=final_w)
        xl = _outproj(ogl, wo, xl, mod, latent=True, final_w=final_w)
        new_state.extend([kc.reshape(N_CTX_B, N_CTX, HKV_A, DH), vc.reshape(N_CTX_B, N_CTX, HKV_A, DH)])
        return xc, xl

    xc, xl = layer_a(xc, xl, mods[0], l0_norm_w, l0_in_w, l0_out_w, l0_sink, cache_l0_k, cache_l0_v)

    lam_init = 0.8 - 0.6 * math.exp(-0.3 * 1)
    w = l1_in_w.astype(BF16)
    outs = lambda kv_dt: [(D, BF16), (D, kv_dt), (D, kv_dt), (D, BF16)]
    qc, kc, vc, zc = _inproj("b", xc, l1_norm_w, mods[1], w, [], outs(F32), latent=False)
    ql, kl, vl, zl = _inproj("b", xl, l1_norm_w, mods[1], w, [], outs(BF16), latent=True, rope_tabs=rope_tabs)
    lams = (l1_lambda_q1, l1_lambda_k1, l1_lambda_q2, l1_lambda_k2)
    ogc = _attn_b(lams, l1_subln_w, qc, kc, vc, zc, lam_init)
    ogl = _attn_b(lams, l1_subln_w, ql, kl, vl, zl, lam_init,
                  cache=(cache_l1_k.reshape(N_LAT_B * PAST, D), cache_l1_v.reshape(N_LAT_B * PAST, D)))
    wo = l1_out_w.astype(BF16)
    xc = _outproj(ogc, wo, xc, mods[1], latent=False)
    xl = _outproj(ogl, wo, xl, mods[1], latent=True)
    new_state.extend([kc.reshape(N_CTX_B, N_CTX, H_C, 2, DH), vc.reshape(N_CTX_B, N_CTX, H_C, 2 * DH)])

    w = l2_in_w.astype(BF16)
    wg = jnp.pad(l2_in_w[:, 4 * D:], ((0, 0), (0, LANES - 4 * H_C))).astype(BF16)
    lane_pad = lambda p: jnp.pad(p.reshape(1, 2 * H_C), ((0, 0), (2 * H_C, LANES - 4 * H_C)))
    extra = [wg, lane_pad(l2_a_log), lane_pad(l2_dt_bias)]
    outs = [(3 * D, BF16), (D, BF16), (LANES, F32)]
    qkvc, zc, gc = _inproj("c", xc, l2_norm_w, mods[2], w, extra, outs, latent=False)
    qkvl, zl, gl = _inproj("c", xl, l2_norm_w, mods[2], w, extra, outs, latent=True)
    ogc, st_new = _gdn(qkvc, l2_conv_w, gc, zc, l2_onorm_w, latent=False)
    (ogl,) = _gdn(qkvl, l2_conv_w, gl, zl, l2_onorm_w, latent=True, s0=state_l2)
    wo = l2_out_w.astype(BF16)
    xc = _outproj(ogc, wo, xc, mods[2], latent=False)
    xl = _outproj(ogl, wo, xl, mods[2], latent=True)
    new_state.append(st_new)

    yc, yl = layer_a(xc, xl, mods[3], l3_norm_w, l3_in_w, l3_out_w, l3_sink, cache_l3_k, cache_l3_v,
                     final_w=final_norm_w)
    return (yc.reshape(N_CTX_B, N_CTX, D), yl.reshape(N_LAT_B, N_LAT, D), *new_state)
```

```python
import functools
import math

import jax
import jax.numpy as jnp
from jax import lax
from jax.experimental import pallas as pl
from jax.experimental.pallas import tpu as pltpu

F32 = jnp.float32
BF16 = jnp.bfloat16

D = 1024
N_CTX_B, N_CTX = 16, 256
N_LAT_B, N_LAT = 4, 1024
PAST = 512
GRID_W = 64
ROPE_BASE = 10000.0
NORM_EPS = 1e-6
DH = 64
HKV_A = 4
WINDOW = 128
H_C = 8
DK_C = 128
CHUNK = 64
LANES = 128
TM = 512
LOG2E = math.log2(math.e)
Q_SCALE = DH ** -0.5 * LOG2E
VMEM_LIMIT = 48 * 1024 * 1024

_NT = (((1,), (1,)), ((), ()))
_TN = (((0,), (0,)), ((), ()))


def _sigmoid(x):
    return 1.0 / (1.0 + jnp.exp(-x))


def _silu(x):
    return x * _sigmoid(x)


def _softplus(x):
    return jnp.maximum(x, 0.0) + jnp.log1p(jnp.exp(-jnp.abs(x)))


def _rms(x, w):
    return x * lax.rsqrt(jnp.mean(x * x, axis=-1, keepdims=True) + NORM_EPS) * w


def _params(sem):
    return pltpu.CompilerParams(dimension_semantics=sem, vmem_limit_bytes=VMEM_LIMIT)


def _mod_kernel(cond_ref, w0, w1, w2, w3, b0, b1, b2, b3, o0, o1, o2, o3):
    a = _silu(cond_ref[...]).astype(BF16)
    for w, b, o in ((w0, b0, o0), (w1, b1, o1), (w2, b2, o2), (w3, b3, o3)):
        o[...] = jnp.dot(a, w[...].astype(BF16), preferred_element_type=F32) + b[...]


def _modulation(cond, mod_ws, mod_bs):
    tn = 512
    wspec = pl.BlockSpec((D, tn), lambda j: (0, j))
    bspec = pl.BlockSpec((1, tn), lambda j: (0, j))
    ospec = pl.BlockSpec((8, tn), lambda j: (0, j))
    outs = pl.pallas_call(
        _mod_kernel,
        grid=(3 * D // tn,),
        in_specs=[pl.BlockSpec((8, D), lambda j: (0, 0))] + [wspec] * 4 + [bspec] * 4,
        out_specs=[ospec] * 4,
        out_shape=[jax.ShapeDtypeStruct((8, 3 * D), F32)] * 4,
        compiler_params=_params(("arbitrary",)),
        name="adaln_mod",
    )(cond, *mod_ws, *[b.reshape(1, 3 * D) for b in mod_bs])
    return [o.reshape(8, 1, 3 * D) for o in outs]


def _adaln_h(x_ref, nw_ref, mod_ref):
    y = _rms(x_ref[...], nw_ref[...])
    return (y * (1.0 + mod_ref[:, D:2 * D]) + mod_ref[:, 0:D]).astype(BF16)


def _proj(h, w_ref, lo, hi):
    return jnp.dot(h, w_ref[:, lo:hi], preferred_element_type=F32)


def _rope(x, cos_ref, sin_ref):
    width = x.shape[1]
    lane = lax.broadcasted_iota(jnp.int32, x.shape, 1)
    partner = jnp.where((lane & 31) < 16, pltpu.roll(x, width - 16, 1), pltpu.roll(x, 16, 1))
    reps = width // LANES
    return x * jnp.tile(cos_ref[...], (1, reps)) + partner * jnp.tile(sin_ref[...], (1, reps))


def _store_cols(ref, cols, val):
    if len(ref.shape) == 2:
        ref[:, cols] = val.astype(ref.dtype)
    else:
        for s in range(ref.shape[0]):
            ref[s, cols, :] = val[s * N_CTX:(s + 1) * N_CTX, :].T.astype(ref.dtype)


def _inproj_a_kernel(*refs, rope):
    if rope:
        x_ref, nw_ref, mod_ref, w_ref, cos_ref, sin_ref, q_ref, k_ref, v_ref, z_ref = refs
    else:
        x_ref, nw_ref, mod_ref, w_ref, q_ref, k_ref, v_ref, z_ref = refs
    h = _adaln_h(x_ref, nw_ref, mod_ref)
    for j in range(2):
        acc = _proj(h, w_ref, j * 512, (j + 1) * 512)
        if rope:
            acc = _rope(acc, cos_ref, sin_ref)
        q_ref[:, j * 512:(j + 1) * 512] = (acc * Q_SCALE).astype(BF16)
    kv = _proj(h, w_ref, 1024, 1536)
    k = kv[:, :256]
    if rope:
        k = _rope(k, cos_ref, sin_ref)
    _store_cols(k_ref, slice(0, 256), k)
    _store_cols(v_ref, slice(0, 256), kv[:, 256:])
    for j in range(2):
        z_ref[:, j * 512:(j + 1) * 512] = _proj(h, w_ref, 1536 + j * 512, 2048 + j * 512).astype(z_ref.dtype)


def _inproj_b_kernel(*refs, rope):
    if rope:
        x_ref, nw_ref, mod_ref, w_ref, cos_ref, sin_ref, q_ref, k_ref, v_ref, z_ref = refs
    else:
        x_ref, nw_ref, mod_ref, w_ref, q_ref, k_ref, v_ref, z_ref = refs
    h = _adaln_h(x_ref, nw_ref, mod_ref)
    for j in range(2):
        sl = slice(j * 512, (j + 1) * 512)
        q = _proj(h, w_ref, j * 512, (j + 1) * 512)
        k = _proj(h, w_ref, D + j * 512, D + (j + 1) * 512)
        if rope:
            q = _rope(q, cos_ref, sin_ref)
            k = _rope(k, cos_ref, sin_ref)
        q_ref[:, sl] = (q * Q_SCALE).astype(BF16)
        _store_cols(k_ref, sl, k)
        _store_cols(v_ref, sl, _proj(h, w_ref, 2 * D + j * 512, 2 * D + (j + 1) * 512))
        z_ref[:, sl] = _proj(h, w_ref, 3 * D + j * 512, 3 * D + (j + 1) * 512).astype(z_ref.dtype)


def _inproj_c_kernel(x_ref, nw_ref, mod_ref, w_ref, wg_ref, alog_ref, dtb_ref, qkv_ref, z_ref, g_ref):
    h = _adaln_h(x_ref, nw_ref, mod_ref)
    for j in range(6):
        qkv_ref[:, j * 512:(j + 1) * 512] = _proj(h, w_ref, j * 512, (j + 1) * 512).astype(qkv_ref.dtype)
    for j in range(2):
        z_ref[:, j * 512:(j + 1) * 512] = _proj(h, w_ref, 3 * D + j * 512, 3 * D + (j + 1) * 512).astype(z_ref.dtype)
    acc = jnp.dot(h, wg_ref[...], preferred_element_type=F32)
    lane = lax.broadcasted_iota(jnp.int32, acc.shape, 1)
    g = -jnp.exp(alog_ref[...]) * _softplus(acc + dtb_ref[...])
    g_ref[...] = jnp.where(lane < 2 * H_C, _sigmoid(acc), g)


class _FEATURE_MAJOR:
    @staticmethod
    def block(width):
        return pl.BlockSpec((TM // N_CTX, width, N_CTX), lambda i: (i, 0, 0))

    @staticmethod
    def shape(rows, width):
        return jax.ShapeDtypeStruct((rows // N_CTX, width, N_CTX), F32)


def _inproj(kind, x, norm_w, mod, w, extra_in, outs, latent, rope_tabs=None):
    rows = x.shape[0]
    per_seq = (N_LAT if latent else N_CTX) // TM
    mod_row = (lambda i: (i // per_seq, 0, 0)) if latent else (lambda i: (4, 0, 0))
    in_specs = [
        pl.BlockSpec((TM, D), lambda i: (i, 0)),
        pl.BlockSpec((1, D), lambda i: (0, 0)),
        pl.BlockSpec((None, 1, 3 * D), mod_row),
        pl.BlockSpec(w.shape, lambda i: (0, 0)),
    ]
    args = [x, norm_w.reshape(1, D), mod, w]
    for e in extra_in:
        in_specs.append(pl.BlockSpec(e.shape, lambda i: (0, 0)))
        args.append(e)
    rope = rope_tabs is not None
    if rope:
        for t in rope_tabs:
            in_specs.append(pl.BlockSpec((TM, LANES), lambda i: (i % per_seq, 0)))
            args.append(t)
    if kind == "a":
        body = functools.partial(_inproj_a_kernel, rope=rope)
    elif kind == "b":
        body = functools.partial(_inproj_b_kernel, rope=rope)
    else:
        body = _inproj_c_kernel
    return pl.pallas_call(
        body,
        grid=(rows // TM,),
        in_specs=in_specs,
        out_specs=[_FEATURE_MAJOR.block(wd) if dt is _FEATURE_MAJOR else pl.BlockSpec((TM, wd), lambda i: (i, 0))
                   for wd, dt in outs],
        out_shape=[_FEATURE_MAJOR.shape(rows, wd) if dt is _FEATURE_MAJOR else jax.ShapeDtypeStruct((rows, wd), dt)
                   for wd, dt in outs],
        compiler_params=_params(("parallel",)),
        name=f"inproj_{kind}_{'lat' if latent else 'ctx'}",
    )(*args)


def _outproj_kernel(og_ref, w_ref, x_ref, mod_ref, *rest, final):
    out = jnp.dot(og_ref[...], w_ref[...], preferred_element_type=F32)
    xn = x_ref[...] + mod_ref[:, 2 * D:3 * D] * out
    if final:
        fw_ref, y_ref = rest
        y_ref[...] = _rms(xn, fw_ref[...])
    else:
        (o_ref,) = rest
        o_ref[...] = xn


def _outproj(og, w, x, mod, latent, final_w=None):
    rows = x.shape[0]
    per_seq = (N_LAT if latent else N_CTX) // TM
    mod_row = (lambda i: (i // per_seq, 0, 0)) if latent else (lambda i: (4, 0, 0))
    in_specs = [
        pl.BlockSpec((TM, D), lambda i: (i, 0)),
        pl.BlockSpec((D, D), lambda i: (0, 0)),
        pl.BlockSpec((TM, D), lambda i: (i, 0)),
        pl.BlockSpec((None, 1, 3 * D), mod_row),
    ]
    args = [og, w, x, mod]
    if final_w is not None:
        in_specs.append(pl.BlockSpec((1, D), lambda i: (0, 0)))
        args.append(final_w.reshape(1, D))
    return pl.pallas_call(
        functools.partial(_outproj_kernel, final=final_w is not None),
        grid=(rows // TM,),
        in_specs=in_specs,
        out_specs=pl.BlockSpec((TM, D), lambda i: (i, 0)),
        out_shape=jax.ShapeDtypeStruct((rows, D), F32),
        compiler_params=_params(("parallel",)),
        name=f"outproj_{'lat' if latent else 'ctx'}",
    )(*args)


class _PerUse:
    def __init__(self, make):
        self._make = make

    def __getitem__(self, i):
        return self._make(i)


def _lane_lo(shape):
    return lax.broadcasted_iota(jnp.int32, shape, 1) < DH


SOFTMAX_BLOCK_ELEMS = 32 * 1024
SOFTMAX_WHOLE_KEYS = 512


def _softmax_rows(sq, sk):
    if sk <= SOFTMAX_WHOLE_KEYS:
        return sq
    rows = 16
    while rows * 2 <= sq and sq % (rows * 2) == 0 and rows * 2 * sk <= SOFTMAX_BLOCK_ELEMS:
        rows *= 2
    return rows


Q_TILES_PER_KV_TILE = 4
_KV_ALIGNED_HEADS = (0, 4, 1, 5, 2, 6, 3, 7, 8, 12, 9, 13, 10, 14, 11, 15)


def _n_keys(x, feature_major):
    return x.shape[1] if feature_major else x.shape[0]


def _scores(lhs, k, feature_major):
    if feature_major:
        return jnp.dot(lhs, k, preferred_element_type=F32)
    return lax.dot_general(lhs, k, _NT, preferred_element_type=F32)


def _weighted_values(p, v, feature_major):
    if feature_major:
        return lax.dot_general(p, v, _NT, preferred_element_type=F32)
    return jnp.dot(p, v, preferred_element_type=F32)


def _gqa_block(sink_ref, q_ref, z_ref, o_ref, kv_pieces, lhs_s, s_s, e_s):
    sq = q_ref.shape[0]
    lo = _lane_lo((1, LANES))
    halves = (lo, jnp.logical_not(lo))
    n_kv_tiles = HKV_A // 2
    pieces = _PerUse(lambda j: [(k2.astype(BF16), v2.astype(BF16), valid, fm)
                                for k2, v2, valid, fm in kv_pieces(j)])
    spans, off = [], 0
    for k2, _, _, fm in pieces[0]:
        spans.append(slice(off, off + _n_keys(k2, fm)))
        off += _n_keys(k2, fm)
    step = _softmax_rows(sq, off)
    blocks = lambda j: [(t, a) for t in range(Q_TILES_PER_KV_TILE * j, Q_TILES_PER_KV_TILE * (j + 1))
                        for a in range(2)]

    def scores_stage(j):
        for i, (t, a) in enumerate(blocks(j)):
            q2 = q_ref[:, t * LANES:(t + 1) * LANES]
            lhs_s[j, i * sq:(i + 1) * sq, :] = jnp.where(halves[a], q2, jnp.zeros_like(q2))
        for (k2, _, _, fm), sp in zip(pieces[j], spans):
            s_s[j, :, sp] = _scores(lhs_s[j], k2, fm)

    def softmax_stage(j):
        inv = []
        for i, (t, a) in enumerate(blocks(j)):
            sink = sink_ref[2 * t + a] * LOG2E
            parts = []
            for r in range(0, sq, step):
                rws = slice(i * sq + r, i * sq + r + step)
                scores = []
                for (_, _, valid, _), sp in zip(pieces[j], spans):
                    s = s_s[j, rws, sp]
                    scores.append(s if valid is None
                                  else jnp.where(valid[r:r + step], s, -jnp.inf))
                m = sink
                for s in scores:
                    m = jnp.maximum(m, jnp.max(s, axis=-1, keepdims=True))
                denom = jnp.exp2(sink - m)
                for s, sp in zip(scores, spans):
                    e = jnp.exp2(s - m)
                    denom = denom + jnp.sum(e, axis=-1, keepdims=True)
                    e_s[j, rws, sp] = e.astype(BF16)
                parts.append(1.0 / denom)
            inv.append(jnp.concatenate(parts, axis=0))
        return inv

    def pv_stage(j, inv):
        pv = None
        for (_, v2, _, fm), sp in zip(pieces[j], spans):
            part = _weighted_values(e_s[j, :, sp], v2, fm)
            pv = part if pv is None else pv + part
        for n_t in range(Q_TILES_PER_KV_TILE):
            t = Q_TILES_PER_KV_TILE * j + n_t
            i0, i1 = 2 * n_t, 2 * n_t + 1
            o2 = jnp.where(lo, pv[i0 * sq:(i0 + 1) * sq] * inv[i0], pv[i1 * sq:(i1 + 1) * sq] * inv[i1])
            cols = slice(t * LANES, (t + 1) * LANES)
            o_ref[:, cols] = (o2 * _silu(z_ref[:, cols].astype(F32))).astype(BF16)

    scores_stage(0)
    for j in range(n_kv_tiles):
        if j + 1 < n_kv_tiles:
            scores_stage(j + 1)
        pv_stage(j, softmax_stage(j))


def _gqa_scratch(sq, sk):
    stacked = 2 * Q_TILES_PER_KV_TILE * sq
    tiles = HKV_A // 2
    return [pltpu.VMEM((tiles, stacked, LANES), BF16), pltpu.VMEM((tiles, stacked, sk), F32),
            pltpu.VMEM((tiles, stacked, sk), BF16)]


def _attn_a_ctx_kernel(sink_ref, q_ref, k_ref, v_ref, z_ref, o_ref, *scratch):
    def kv_pieces(j):
        rws = slice(j * LANES, (j + 1) * LANES)
        return [(k_ref[rws, :], v_ref[rws, :], None, True)]
    _gqa_block(sink_ref, q_ref, z_ref, o_ref, kv_pieces, *scratch)


def _attn_a_lat_kernel(sink_ref, q_ref, k_ref, v_ref, kc_ref, vc_ref, z_ref, o_ref, *scratch):
    n = pl.program_id(1)
    span = 3 * WINDOW
    start = pl.multiple_of(jnp.clip((n - 1) * WINDOW, 0, N_LAT - span), WINDOW)
    qi = n * WINDOW + lax.broadcasted_iota(jnp.int32, (WINDOW, span), 0)
    kj = start + lax.broadcasted_iota(jnp.int32, (WINDOW, span), 1)
    valid = jnp.abs(kj - qi) <= WINDOW

    def kv_pieces(j):
        cols = slice(j * LANES, (j + 1) * LANES)
        return [(k_ref[pl.ds(start, span), cols], v_ref[pl.ds(start, span), cols], valid, False),
                (kc_ref[cols, :], vc_ref[cols, :], None, True)]
    _gqa_block(sink_ref, q_ref, z_ref, o_ref, kv_pieces, *scratch)


def _attn_a(sink, q, k, v, z, cache=None):
    smem = pl.BlockSpec(memory_space=pltpu.SMEM)
    kvw = HKV_A * DH
    if cache is None:
        return pl.pallas_call(
            _attn_a_ctx_kernel,
            grid=(N_CTX_B,),
            in_specs=[smem,
                      pl.BlockSpec((N_CTX, D), lambda b: (b, 0)),
                      pl.BlockSpec((None, kvw, N_CTX), lambda b: (b, 0, 0)),
                      pl.BlockSpec((None, kvw, N_CTX), lambda b: (b, 0, 0)),
                      pl.BlockSpec((N_CTX, D), lambda b: (b, 0))],
            out_specs=pl.BlockSpec((N_CTX, D), lambda b: (b, 0)),
            out_shape=jax.ShapeDtypeStruct((N_CTX_B * N_CTX, D), BF16),
            scratch_shapes=_gqa_scratch(N_CTX, N_CTX),
            compiler_params=_params(("parallel",)),
            name="attn_a_ctx",
        )(sink, q, k, v, z)
    kc, vc = cache
    nq = N_LAT // WINDOW
    return pl.pallas_call(
        _attn_a_lat_kernel,
        grid=(N_LAT_B, nq),
        in_specs=[smem,
                  pl.BlockSpec((WINDOW, D), lambda b, n: (b * nq + n, 0)),
                  pl.BlockSpec((N_LAT, kvw), lambda b, n: (b, 0)),
                  pl.BlockSpec((N_LAT, kvw), lambda b, n: (b, 0)),
                  pl.BlockSpec((None, kvw, PAST), lambda b, n: (b, 0, 0)),
                  pl.BlockSpec((None, kvw, PAST), lambda b, n: (b, 0, 0)),
                  pl.BlockSpec((WINDOW, D), lambda b, n: (b * nq + n, 0))],
        out_specs=pl.BlockSpec((WINDOW, D), lambda b, n: (b * nq + n, 0)),
        out_shape=jax.ShapeDtypeStruct((N_LAT_B * N_LAT, D), BF16),
        scratch_shapes=_gqa_scratch(WINDOW, 3 * WINDOW + PAST),
        compiler_params=_params(("parallel", "arbitrary")),
        name="attn_a_lat",
    )(sink, q, k, v, kc, vc, z)


def _diff_attn_block(lam_refs, subln_ref, q_ref, z_ref, o_ref, kv_pieces, lam_init, lhs_s, s_s, a_s):
    lq1, lk1, lq2, lk2 = lam_refs
    dot_exp = lambda a, c: jnp.exp(jnp.sum(a[...] * c[...], axis=-1, keepdims=True))
    lam = dot_exp(lq1, lk1) - dot_exp(lq2, lk2) + lam_init
    sq = q_ref.shape[0]
    lo = _lane_lo((1, LANES))
    n_heads = D // LANES
    pieces = _PerUse(kv_pieces)
    spans, off = [], 0
    for k2, _, fm in pieces[0]:
        spans.append(slice(off, off + _n_keys(k2, fm)))
        off += _n_keys(k2, fm)
    step = _softmax_rows(sq, 2 * off)

    def scores_stage(h):
        buf = h % 2
        q2 = q_ref[:, h * LANES:(h + 1) * LANES]
        lhs_s[buf, 0:sq, :] = jnp.where(lo, q2, jnp.zeros_like(q2))
        lhs_s[buf, sq:2 * sq, :] = jnp.where(lo, jnp.zeros_like(q2), q2)
        for (k2, _, fm), sp in zip(pieces[h], spans):
            s_s[buf, :, sp] = _scores(lhs_s[buf], k2, fm)

    def softmax_stage(h):
        buf = h % 2
        inv = []
        for r in range(0, sq, step):
            exps, denoms = [], []
            for c in range(2):
                rws = slice(c * sq + r, c * sq + r + step)
                scores = [s_s[buf, rws, sp] for sp in spans]
                m = None
                for s in scores:
                    sm = jnp.max(s, axis=-1, keepdims=True)
                    m = sm if m is None else jnp.maximum(m, sm)
                es = [jnp.exp2(s - m) for s in scores]
                denom = None
                for e in es:
                    se = jnp.sum(e, axis=-1, keepdims=True)
                    denom = se if denom is None else denom + se
                exps.append(es)
                denoms.append(denom)
            ratio = lam * denoms[0] / denoms[1]
            for e1, e2, sp in zip(exps[0], exps[1], spans):
                a_s[buf, r:r + step, sp] = (e1 - ratio * e2).astype(BF16)
            inv.append(1.0 / denoms[0])
        return jnp.concatenate(inv, axis=0)

    def pv_stage(h, inv):
        buf = h % 2
        cols = slice(h * LANES, (h + 1) * LANES)
        o = None
        for (_, vx, _), sp in zip(pieces[h], spans):
            po = jnp.dot(a_s[buf, :, sp], vx, preferred_element_type=F32)
            o = po if o is None else o + po
        o = _rms(o * inv, subln_ref[...]) * (1.0 - lam_init)
        o_ref[:, cols] = (o * _silu(z_ref[:, cols].astype(F32))).astype(BF16)

    scores_stage(0)
    for h in range(n_heads):
        if h + 1 < n_heads:
            scores_stage(h + 1)
        pv_stage(h, softmax_stage(h))


def _diff_attn_scratch(sq, sk):
    return [pltpu.VMEM((2, 2 * sq, LANES), BF16), pltpu.VMEM((2, 2 * sq, sk), F32),
            pltpu.VMEM((2, sq, sk), BF16)]


def _attn_b_ctx_kernel(lq1, lk1, lq2, lk2, subln_ref, q_ref, k_ref, v_ref, z_ref, o_ref, *scratch, lam_init):
    def kv_pieces(h):
        cols = slice(h * LANES, (h + 1) * LANES)
        return [(k_ref[cols, :].astype(BF16), v_ref[:, cols].astype(BF16), True)]
    _diff_attn_block((lq1, lk1, lq2, lk2), subln_ref, q_ref, z_ref, o_ref, kv_pieces, lam_init, *scratch)


def _attn_b_lat_kernel(lq1, lk1, lq2, lk2, subln_ref, q_ref, k_ref, v_ref, kc_ref, vc_ref, z_ref, o_ref,
                       *scratch, lam_init):
    def kv_pieces(h):
        cols = slice(h * LANES, (h + 1) * LANES)
        return [(k_ref[:, cols].astype(BF16), v_ref[:, cols].astype(BF16), False),
                (kc_ref[cols, :].astype(BF16), vc_ref[:, cols].astype(BF16), True)]
    _diff_attn_block((lq1, lk1, lq2, lk2), subln_ref, q_ref, z_ref, o_ref, kv_pieces, lam_init, *scratch)


def _attn_b(lams, subln, q, k, v, z, lam_init, cache=None):
    small = [pl.BlockSpec((1, DH), lambda *_: (0, 0))] * 4 + [pl.BlockSpec((1, 2 * DH), lambda *_: (0, 0))]
    small_args = [l.reshape(1, DH) for l in lams] + [subln.reshape(1, 2 * DH)]
    if cache is None:
        blk = pl.BlockSpec((N_CTX, D), lambda b: (b, 0))
        return pl.pallas_call(
            functools.partial(_attn_b_ctx_kernel, lam_init=lam_init),
            grid=(N_CTX_B,),
            in_specs=small + [blk, pl.BlockSpec((None, D, N_CTX), lambda b: (b, 0, 0)), blk, blk],
            out_specs=blk,
            out_shape=jax.ShapeDtypeStruct((N_CTX_B * N_CTX, D), BF16),
            scratch_shapes=_diff_attn_scratch(N_CTX, N_CTX),
            compiler_params=_params(("parallel",)),
            name="attn_b_ctx",
        )(*small_args, q, k, v, z)
    kc, vc = cache
    tq = 256
    nq = N_LAT // tq
    qblk = pl.BlockSpec((tq, D), lambda b, n: (b * nq + n, 0))
    return pl.pallas_call(
        functools.partial(_attn_b_lat_kernel, lam_init=lam_init),
        grid=(N_LAT_B, nq),
        in_specs=small + [qblk,
                          pl.BlockSpec((N_LAT, D), lambda b, n: (b, 0)),
                          pl.BlockSpec((N_LAT, D), lambda b, n: (b, 0)),
                          pl.BlockSpec((None, D, PAST), lambda b, n: (b, 0, 0)),
                          pl.BlockSpec((PAST, D), lambda b, n: (b, 0)),
                          qblk],
        out_specs=qblk,
        out_shape=jax.ShapeDtypeStruct((N_LAT_B * N_LAT, D), BF16),
        scratch_shapes=_diff_attn_scratch(tq, N_LAT + PAST),
        compiler_params=_params(("parallel", "arbitrary")),
        name="attn_b_lat",
    )(*small_args, q, k, v, kc, vc, z)


def _mm(a, b):
    return jnp.dot(a.astype(BF16), b.astype(BF16), preferred_element_type=F32)


def _each(fn, *lists):
    return [fn(*xs) for xs in zip(*lists)]


def _unit_tri_inverse_residuals(ls, same_blk, _mm):
    rd = [jnp.where(same_blk, -l, 0.0) for l in ls]
    lo = [jnp.where(same_blk, 0.0, l) for l in ls]
    pk = _each(_mm, rd, rd)
    for it in range(3):
        t = _each(_mm, rd, pk)
        nxt = _each(_mm, pk, pk) if it < 2 else None
        rd = _each(lambda r, p, x: r + p + x, rd, pk, t)
        pk = nxt
    m = _each(lambda x, r: x + _mm(r, x), lo, rd)
    m2 = _each(_mm, m, m)
    q = _each(lambda a, a2: a2 - a - _mm(a, a2), m, m2)
    return _each(lambda a, r: a + r + _mm(a, r), q, rd)


def _gdn_kernel(*refs, n, hp, has_s0, want_state):
    it = iter(refs)
    qp_ref, kp_ref, vp_ref, cwq_ref, cwk_ref, cwv_ref, g_ref, z_ref, onw_ref = [next(it) for _ in range(9)]
    s0_ref = next(it) if has_s0 else None
    og_ref = next(it)
    sf_ref = next(it) if want_state else None
    q_s, k_s, v_s, beta_b, gc_b, gtot_b, u_s, wq_s, intra_s, bb_s, ab_s, st_s = it
    nc = n // CHUNK
    head0 = pl.program_id(1) * hp

    row = lax.broadcasted_iota(jnp.int32, (n, LANES), 0)
    lane = lax.broadcasted_iota(jnp.int32, (n, LANES), 1)

    def conv_silu(p_ref, cw_ref, cols):
        x = p_ref[:, cols].astype(F32)
        xm1 = jnp.where(row == 0, 0.0, pltpu.roll(x, 1, 0))
        xp1 = jnp.where(row == n - 1, 0.0, pltpu.roll(x, n - 1, 0))
        return _silu(cw_ref[0:1, cols] * xm1 + cw_ref[1:2, cols] * x + cw_ref[2:3, cols] * xp1)

    def l2n(x):
        return x * lax.rsqrt(jnp.sum(x * x, axis=-1, keepdims=True) + 1e-6)

    for p in range(hp):
        cols = slice(p * LANES, (p + 1) * LANES)
        q_s[p] = l2n(conv_silu(qp_ref, cwq_ref, cols)) * (DK_C ** -0.5)
        k_s[p] = l2n(conv_silu(kp_ref, cwk_ref, cols))
        v_s[p] = conv_silu(vp_ref, cwv_ref, cols)

    gates = g_ref[...]
    local = row & (CHUNK - 1)
    pre = gates
    suf = gates
    s = 1
    while s < CHUNK:
        pre = pre + jnp.where(local >= s, pltpu.roll(pre, s, 0), 0.0)
        suf = suf + jnp.where(local < CHUNK - s, pltpu.roll(suf, n - s, 0), 0.0)
        s *= 2
    tot = pre + suf - gates

    def col(x, idx):
        picked = jnp.sum(jnp.where(lane == idx, x, 0.0), axis=1, keepdims=True)
        return jnp.broadcast_to(picked, (n, LANES))

    for p in range(hp):
        for d in range(2):
            beta_b[p, d] = col(gates, d * H_C + head0 + p)
            gc_b[p, d] = col(pre if d == 0 else suf, 2 * H_C + d * H_C + head0 + p)
            gtot_b[p, d] = col(tot, 2 * H_C + d * H_C + head0 + p)

    ii = lax.broadcasted_iota(jnp.int32, (CHUNK, LANES), 0)
    lane2 = lax.broadcasted_iota(jnp.int32, (CHUNK, LANES), 1)
    fwd = lane2 < CHUNK
    jj = lane2 & (CHUNK - 1)
    same_blk = (ii >> 4) == (jj >> 4)
    diag = ii == jj
    ahead = jnp.where(fwd, ii - jj, jj - ii)
    incl = ahead >= 0
    strict = ahead > 0
    heads = range(hp)
    pairs = [(p, c) for p in heads for c in range(nc)]
    chains = [(p, c, d) for p, c in pairs for d in range(2)]
    rows = lambda c: pl.ds(c * CHUNK, CHUNK)
    rows2 = lambda c: pl.ds(c * LANES, LANES)
    zeros_b = jnp.zeros((CHUNK, LANES), BF16)

    def block_diag(x2):
        return jnp.concatenate([jnp.where(fwd, x2, 0.0), jnp.where(fwd, 0.0, x2)], axis=0).astype(BF16)

    def packed_mm(x2, y2):
        return jnp.dot(x2.astype(BF16), block_diag(y2), preferred_element_type=F32)

    kc = [k_s[p, rows(c), :] for p, c in pairs]
    qc = [q_s[p, rows(c), :] for p, c in pairs]
    beta = [beta_b[p, d, rows(c), :] for p, c, d in chains]
    gc = [gc_b[p, d, rows(c), :] for p, c, d in chains]
    kb = [kc[i // 2] * b for i, b in enumerate(beta)]
    eg = [jnp.exp(g) for g in gc]
    a = []
    for j in range(len(pairs)):
        k_b = kc[j].astype(BF16)
        lhs = jnp.concatenate([jnp.concatenate([kb[2 * j], kb[2 * j + 1]], axis=1),
                               jnp.concatenate([qc[j], qc[j]], axis=1)], axis=0).astype(BF16)
        rhs_nt = jnp.concatenate([jnp.concatenate([k_b, zeros_b], axis=1),
                                  jnp.concatenate([zeros_b, k_b], axis=1)], axis=0)
        a.append(lax.dot_general(lhs, rhs_nt, _NT, preferred_element_type=F32))
    kd = [(kc[i // 2] * jnp.exp(gtot_b[p, d, rows(c), :] - gc[i])).astype(BF16)
          for i, (p, c, d) in enumerate(chains)]
    ls = []
    for j, (p, c) in enumerate(pairs):
        gci = jnp.where(fwd, gc[2 * j], gc[2 * j + 1])
        gcj = jnp.sum(jnp.where(diag, gci, 0.0), axis=0, keepdims=True)
        decay = jnp.where(incl, jnp.exp(jnp.where(incl, gci - gcj, 0.0)), 0.0)
        ls.append(jnp.where(strict, a[j][:CHUNK] * decay, 0.0))
        intra_s[p, rows(c), :] = jnp.where(incl, a[j][CHUNK:] * decay, 0.0)
        for d in range(2):
            wq_s[p, d, pl.ds(c * LANES + CHUNK, CHUNK), :] = (qc[j] * eg[2 * j + d]).astype(BF16)
    r = _unit_tri_inverse_residuals(ls, same_blk, packed_mm)
    rhs = [jnp.concatenate([v_s[p, rows(c), :] * b, x * e], axis=1)
           for (p, c, d), b, x, e in zip(chains, beta, kb, eg)]
    zeros_w = jnp.zeros((CHUNK, 2 * LANES), BF16)
    sol = []
    for j in range(len(pairs)):
        rf, rb = rhs[2 * j], rhs[2 * j + 1]
        both = jnp.concatenate([jnp.concatenate([rf.astype(BF16), zeros_w], axis=1),
                                jnp.concatenate([zeros_w, rb.astype(BF16)], axis=1)], axis=0)
        x = jnp.dot(r[j].astype(BF16), both, preferred_element_type=F32)
        sol.extend([rf + x[:, :2 * LANES], rb + x[:, 2 * LANES:]])
    for (p, c, d), x in zip(chains, sol):
        u_s[p, d, rows(c), :] = x[:, :LANES]
        wq_s[p, d, pl.ds(c * LANES, CHUNK), :] = x[:, LANES:].astype(BF16)
    ba = _each(lambda t, x: lax.dot_general(t, x.astype(BF16), _TN, preferred_element_type=F32),
               kd, sol)
    for (p, c, d), x in zip(chains, ba):
        bb_s[p, d, rows2(c), :] = x[:, :LANES]
        ab_s[p, d, rows2(c), :] = x[:, LANES:].astype(BF16)

    scans = [(p, d) for p in heads for d in range(2)]
    if has_s0:
        state = [s0_ref[d, p] for p, d in scans]
    else:
        state = [jnp.zeros((DK_C, LANES), F32) for _ in scans]
    for i in range(nc):
        for j, (p, d) in enumerate(scans):
            c = i if d == 0 else nc - 1 - i
            st_b = state[j].astype(BF16)
            st_s[p, d, rows2(c), :] = st_b
            eg_tot = jnp.exp(gtot_b[p, d, pl.ds(c * CHUNK, 1), :])
            state[j] = (state[j] * eg_tot + bb_s[p, d, rows2(c), :]
                        - jnp.dot(ab_s[p, d, rows2(c), :], st_b, preferred_element_type=F32))

    wo = [jnp.dot(wq_s[p, d, rows2(c), :], st_s[p, d, rows2(c), :], preferred_element_type=F32)
          for p, c, d in chains]
    vnew = [u_s[p, d, rows(c), :] - x[:CHUNK] for (p, c, d), x in zip(chains, wo)]
    for j, (p, c) in enumerate(pairs):
        cols = slice(p * LANES, (p + 1) * LANES)
        oc = (wo[2 * j][CHUNK:] + wo[2 * j + 1][CHUNK:]
              + _mm(intra_s[p, rows(c), :], jnp.concatenate([vnew[2 * j], vnew[2 * j + 1]], axis=0)))
        og_ref[rows(c), cols] = (_rms(oc, onw_ref[...]) * _silu(z_ref[rows(c), cols].astype(F32))).astype(BF16)
    if want_state:
        for j, (p, d) in enumerate(scans):
            sf_ref[d, p] = state[j]


def _gdn(qkv, conv_w, gates, z, onorm_w, latent, s0=None):
    n = N_LAT if latent else N_CTX
    nb = N_LAT_B if latent else N_CTX_B
    nc = n // CHUNK
    want_state = not latent
    hp = 1 if latent else 4
    groups = H_C // hp
    col = lambda off: pl.BlockSpec((n, hp * LANES), lambda b, h, off=off: (b, off * groups + h))
    cw = lambda off: pl.BlockSpec((3, hp * LANES), lambda b, h, off=off: (0, off * groups + h))
    state_spec = pl.BlockSpec((None, 2, hp, DK_C, LANES), lambda b, h: (b, 0, h, 0, 0))
    in_specs = [col(0), col(1), col(2), cw(0), cw(1), cw(2),
                pl.BlockSpec((n, LANES), lambda b, h: (b, 0)),
                pl.BlockSpec((n, hp * LANES), lambda b, h: (b, h)),
                pl.BlockSpec((1, LANES), lambda b, h: (0, 0))]
    args = [qkv, qkv, qkv, conv_w, conv_w, conv_w, gates, z, onorm_w.reshape(1, LANES)]
    if s0 is not None:
        in_specs.append(state_spec)
        args.append(s0)
    out_specs = [pl.BlockSpec((n, hp * LANES), lambda b, h: (b, h))]
    out_shape = [jax.ShapeDtypeStruct((nb * n, D), BF16)]
    if want_state:
        out_specs.append(state_spec)
        out_shape.append(jax.ShapeDtypeStruct((nb, 2, H_C, DK_C, LANES), F32))
    seq = lambda: pltpu.VMEM((hp, n, LANES), F32)
    both = lambda: pltpu.VMEM((hp, 2, n, LANES), F32)
    per_chunk = lambda dt: pltpu.VMEM((hp, 2, nc * LANES, LANES), dt)
    scratch = [seq(), seq(), seq(), both(), both(), both(), both(),
               per_chunk(BF16), seq(), per_chunk(F32), per_chunk(BF16),
               per_chunk(BF16)]
    return pl.pallas_call(
        functools.partial(_gdn_kernel, n=n, hp=hp, has_s0=s0 is not None, want_state=want_state),
        grid=(nb, groups),
        in_specs=in_specs,
        out_specs=out_specs,
        out_shape=out_shape,
        scratch_shapes=scratch,
        compiler_params=_params(("parallel", "arbitrary")),
        name=f"gdn_{'lat' if latent else 'ctx'}",
    )(*args)


def _rope_tables():
    rows = N_LAT // GRID_W
    row = jnp.repeat(jnp.arange(rows), GRID_W).astype(F32)
    colp = jnp.tile(jnp.arange(GRID_W), rows).astype(F32)
    quarter = DH // 4
    inv = ROPE_BASE ** (-jnp.arange(quarter, dtype=F32) / quarter)
    ar, ac = row[:, None] * inv, colp[:, None] * inv
    cos = jnp.concatenate([jnp.cos(ar)] * 2 + [jnp.cos(ac)] * 2, axis=-1)
    sin = jnp.concatenate([-jnp.sin(ar), jnp.sin(ar), -jnp.sin(ac), jnp.sin(ac)], axis=-1)
    return jnp.tile(cos, (1, 2)), jnp.tile(sin, (1, 2))


def kernel(x_prompt, x_sample, cache_l0_k, cache_l0_v, cache_l1_k, cache_l1_v, state_l2, cache_l3_k, cache_l3_v, c, c_ctx, l0_norm_w, l0_mod_w, l0_mod_b, l0_in_w, l0_out_w, l0_sink, l1_norm_w, l1_mod_w, l1_mod_b, l1_in_w, l1_out_w, l1_lambda_q1, l1_lambda_k1, l1_lambda_q2, l1_lambda_k2, l1_subln_w, l2_norm_w, l2_mod_w, l2_mod_b, l2_in_w, l2_out_w, l2_conv_w, l2_a_log, l2_dt_bias, l2_onorm_w, l3_norm_w, l3_mod_w, l3_mod_b, l3_in_w, l3_out_w, l3_sink, final_norm_w):
    xc = x_prompt.reshape(N_CTX_B * N_CTX, D)
    xl = x_sample.reshape(N_LAT_B * N_LAT, D)
    cond = jnp.concatenate([c, c_ctx[None, :], jnp.zeros((3, D), F32)], axis=0)
    mods = _modulation(cond, (l0_mod_w, l1_mod_w, l2_mod_w, l3_mod_w), (l0_mod_b, l1_mod_b, l2_mod_b, l3_mod_b))
    rope_tabs = _rope_tables()
    kvw = HKV_A * DH
    new_state = []

    def feature_major(t):
        nd = t.ndim
        return jnp.transpose(t, (0, *range(2, nd), 1)).reshape(t.shape[0], -1, t.shape[1])

    def token_major(t, feature_dims):
        t = t.reshape(t.shape[0], *feature_dims, t.shape[-1])
        return jnp.transpose(t, (0, t.ndim - 1, *range(1, t.ndim - 1)))

    def layer_a(xc, xl, mod, norm_w, in_w, out_w, sink, cache_k, cache_v, final_w=None):
        head = lambda h: slice(h * DH, (h + 1) * DH)
        by_head = lambda cols: [cols[:, head(h)] for h in _KV_ALIGNED_HEADS]
        in_w = jnp.concatenate(by_head(in_w[:, :D]) + [in_w[:, D:D + 2 * kvw]] + by_head(in_w[:, D + 2 * kvw:]),
                               axis=1)
        out_w = jnp.concatenate([out_w[head(h)] for h in _KV_ALIGNED_HEADS], axis=0)
        sink = jnp.concatenate([sink[h:h + 1] for h in _KV_ALIGNED_HEADS])
        w = in_w.astype(BF16)
        outs = lambda kv_dt: [(D, BF16), (kvw, kv_dt), (kvw, kv_dt), (D, BF16)]
        qc, kc, vc, zc = _inproj("a", xc, norm_w, mod, w, [], outs(_FEATURE_MAJOR), latent=False)
        ql, kl, vl, zl = _inproj("a", xl, norm_w, mod, w, [], outs(BF16), latent=True, rope_tabs=rope_tabs)
        ogc = _attn_a(sink, qc, kc, vc, zc)
        ogl = _attn_a(sink, ql, kl, vl, zl, cache=(feature_major(cache_k), feature_major(cache_v)))
        wo = out_w.astype(BF16)
        xc = _outproj(ogc, wo, xc, mod, latent=False, final_w=final_w)
        xl = _outproj(ogl, wo, xl, mod, latent=True, final_w=final_w)
        new_state.extend([token_major(kc, (HKV_A, DH)), token_major(vc, (HKV_A, DH))])
        return xc, xl

    xc, xl = layer_a(xc, xl, mods[0], l0_norm_w, l0_in_w, l0_out_w, l0_sink, cache_l0_k, cache_l0_v)

    lam_init = 0.8 - 0.6 * math.exp(-0.3 * 1)
    w = l1_in_w.astype(BF16)
    outs = lambda k_dt, v_dt: [(D, BF16), (D, k_dt), (D, v_dt), (D, BF16)]
    qc, kc, vc, zc = _inproj("b", xc, l1_norm_w, mods[1], w, [], outs(_FEATURE_MAJOR, F32), latent=False)
    ql, kl, vl, zl = _inproj("b", xl, l1_norm_w, mods[1], w, [], outs(BF16, BF16), latent=True,
                             rope_tabs=rope_tabs)
    lams = (l1_lambda_q1, l1_lambda_k1, l1_lambda_q2, l1_lambda_k2)
    ogc = _attn_b(lams, l1_subln_w, qc, kc, vc, zc, lam_init)
    ogl = _attn_b(lams, l1_subln_w, ql, kl, vl, zl, lam_init,
                  cache=(feature_major(cache_l1_k), cache_l1_v.reshape(N_LAT_B * PAST, D)))
    wo = l1_out_w.astype(BF16)
    xc = _outproj(ogc, wo, xc, mods[1], latent=False)
    xl = _outproj(ogl, wo, xl, mods[1], latent=True)
    new_state.extend([token_major(kc, (H_C, 2, DH)), vc.reshape(N_CTX_B, N_CTX, H_C, 2 * DH)])

    w = l2_in_w.astype(BF16)
    wg = jnp.pad(l2_in_w[:, 4 * D:], ((0, 0), (0, LANES - 4 * H_C))).astype(BF16)
    lane_pad = lambda p: jnp.pad(p.reshape(1, 2 * H_C), ((0, 0), (2 * H_C, LANES - 4 * H_C)))
    extra = [wg, lane_pad(l2_a_log), lane_pad(l2_dt_bias)]
    outs = [(3 * D, BF16), (D, BF16), (LANES, F32)]
    qkvc, zc, gc = _inproj("c", xc, l2_norm_w, mods[2], w, extra, outs, latent=False)
    qkvl, zl, gl = _inproj("c", xl, l2_norm_w, mods[2], w, extra, outs, latent=True)
    ogc, st_new = _gdn(qkvc, l2_conv_w, gc, zc, l2_onorm_w, latent=False)
    (ogl,) = _gdn(qkvl, l2_conv_w, gl, zl, l2_onorm_w, latent=True, s0=state_l2)
    wo = l2_out_w.astype(BF16)
    xc = _outproj(ogc, wo, xc, mods[2], latent=False)
    xl = _outproj(ogl, wo, xl, mods[2], latent=True)
    new_state.append(st_new)

    yc, yl = layer_a(xc, xl, mods[3], l3_norm_w, l3_in_w, l3_out_w, l3_sink, cache_l3_k, cache_l3_v,
                     final_w=final_norm_w)
    return (yc.reshape(N_CTX_B, N_CTX, D), yl.reshape(N_LAT_B, N_LAT, D), *new_state)
```

```python
import functools
import math

import jax
import jax.numpy as jnp
from jax import lax
from jax.experimental import pallas as pl
from jax.experimental.pallas import tpu as pltpu

F32 = jnp.float32
BF16 = jnp.bfloat16

D = 1024
N_CTX_B, N_CTX = 16, 256
N_LAT_B, N_LAT = 4, 1024
PAST = 512
GRID_W = 64
ROPE_BASE = 10000.0
NORM_EPS = 1e-6
DH = 64
HKV_A = 4
WINDOW = 128
H_C = 8
DK_C = 128
CHUNK = 64
LANES = 128
TM = 512
LOG2E = math.log2(math.e)
Q_SCALE = DH ** -0.5 * LOG2E
VMEM_LIMIT = 48 * 1024 * 1024

_NT = (((1,), (1,)), ((), ()))
_TN = (((0,), (0,)), ((), ()))


def _sigmoid(x):
    return 1.0 / (1.0 + jnp.exp(-x))


def _silu(x):
    return x * _sigmoid(x)


def _softplus(x):
    return jnp.maximum(x, 0.0) + jnp.log1p(jnp.exp(-jnp.abs(x)))


def _rms(x, w):
    return x * lax.rsqrt(jnp.mean(x * x, axis=-1, keepdims=True) + NORM_EPS) * w


def _params(sem):
    return pltpu.CompilerParams(dimension_semantics=sem, vmem_limit_bytes=VMEM_LIMIT)


def _mod_kernel(cond_ref, w0, w1, w2, w3, b0, b1, b2, b3, o0, o1, o2, o3):
    a = _silu(cond_ref[...]).astype(BF16)
    for w, b, o in ((w0, b0, o0), (w1, b1, o1), (w2, b2, o2), (w3, b3, o3)):
        o[...] = jnp.dot(a, w[...].astype(BF16), preferred_element_type=F32) + b[...]


def _modulation(cond, mod_ws, mod_bs):
    tn = 512
    wspec = pl.BlockSpec((D, tn), lambda j: (0, j))
    bspec = pl.BlockSpec((1, tn), lambda j: (0, j))
    ospec = pl.BlockSpec((8, tn), lambda j: (0, j))
    outs = pl.pallas_call(
        _mod_kernel,
        grid=(3 * D // tn,),
        in_specs=[pl.BlockSpec((8, D), lambda j: (0, 0))] + [wspec] * 4 + [bspec] * 4,
        out_specs=[ospec] * 4,
        out_shape=[jax.ShapeDtypeStruct((8, 3 * D), F32)] * 4,
        compiler_params=_params(("arbitrary",)),
        name="adaln_mod",
    )(cond, *mod_ws, *[b.reshape(1, 3 * D) for b in mod_bs])
    return [o.reshape(8, 1, 3 * D) for o in outs]


def _adaln_h(x_ref, nw_ref, mod_ref):
    y = _rms(x_ref[...], nw_ref[...])
    return (y * (1.0 + mod_ref[:, D:2 * D]) + mod_ref[:, 0:D]).astype(BF16)


def _proj(h, w_ref, lo, hi):
    return jnp.dot(h, w_ref[:, lo:hi], preferred_element_type=F32)


def _rope(x, cos_ref, sin_ref):
    width = x.shape[1]
    lane = lax.broadcasted_iota(jnp.int32, x.shape, 1)
    partner = jnp.where((lane & 31) < 16, pltpu.roll(x, width - 16, 1), pltpu.roll(x, 16, 1))
    reps = width // LANES
    return x * jnp.tile(cos_ref[...], (1, reps)) + partner * jnp.tile(sin_ref[...], (1, reps))


def _store_cols(ref, cols, val):
    if len(ref.shape) == 2:
        ref[:, cols] = val.astype(ref.dtype)
    else:
        for s in range(ref.shape[0]):
            ref[s, cols, :] = val[s * N_CTX:(s + 1) * N_CTX, :].T.astype(ref.dtype)


def _inproj_a_kernel(*refs, rope):
    if rope:
        x_ref, nw_ref, mod_ref, w_ref, cos_ref, sin_ref, q_ref, k_ref, v_ref, z_ref = refs
    else:
        x_ref, nw_ref, mod_ref, w_ref, q_ref, k_ref, v_ref, z_ref = refs
    h = _adaln_h(x_ref, nw_ref, mod_ref)
    for j in range(2):
        acc = _proj(h, w_ref, j * 512, (j + 1) * 512)
        if rope:
            acc = _rope(acc, cos_ref, sin_ref)
        q_ref[:, j * 512:(j + 1) * 512] = (acc * Q_SCALE).astype(BF16)
    kv = _proj(h, w_ref, 1024, 1536)
    k = kv[:, :256]
    if rope:
        k = _rope(k, cos_ref, sin_ref)
    _store_cols(k_ref, slice(0, 256), k)
    _store_cols(v_ref, slice(0, 256), kv[:, 256:])
    for j in range(2):
        z_ref[:, j * 512:(j + 1) * 512] = _proj(h, w_ref, 1536 + j * 512, 2048 + j * 512).astype(z_ref.dtype)


def _inproj_b_kernel(*refs, rope):
    if rope:
        x_ref, nw_ref, mod_ref, w_ref, cos_ref, sin_ref, q_ref, k_ref, v_ref, z_ref = refs
    else:
        x_ref, nw_ref, mod_ref, w_ref, q_ref, k_ref, v_ref, z_ref = refs
    h = _adaln_h(x_ref, nw_ref, mod_ref)
    for j in range(2):
        sl = slice(j * 512, (j + 1) * 512)
        q = _proj(h, w_ref, j * 512, (j + 1) * 512)
        k = _proj(h, w_ref, D + j * 512, D + (j + 1) * 512)
        if rope:
            q = _rope(q, cos_ref, sin_ref)
            k = _rope(k, cos_ref, sin_ref)
        q_ref[:, sl] = (q * Q_SCALE).astype(BF16)
        _store_cols(k_ref, sl, k)
        _store_cols(v_ref, sl, _proj(h, w_ref, 2 * D + j * 512, 2 * D + (j + 1) * 512))
        z_ref[:, sl] = _proj(h, w_ref, 3 * D + j * 512, 3 * D + (j + 1) * 512).astype(z_ref.dtype)


def _inproj_c_kernel(x_ref, nw_ref, mod_ref, w_ref, wg_ref, alog_ref, dtb_ref, qkv_ref, z_ref, g_ref):
    h = _adaln_h(x_ref, nw_ref, mod_ref)
    for j in range(6):
        qkv_ref[:, j * 512:(j + 1) * 512] = _proj(h, w_ref, j * 512, (j + 1) * 512).astype(qkv_ref.dtype)
    for j in range(2):
        z_ref[:, j * 512:(j + 1) * 512] = _proj(h, w_ref, 3 * D + j * 512, 3 * D + (j + 1) * 512).astype(z_ref.dtype)
    acc = jnp.dot(h, wg_ref[...], preferred_element_type=F32)
    lane = lax.broadcasted_iota(jnp.int32, acc.shape, 1)
    g = -jnp.exp(alog_ref[...]) * _softplus(acc + dtb_ref[...])
    g_ref[...] = jnp.where(lane < 2 * H_C, _sigmoid(acc), g)


class _FEATURE_MAJOR:
    @staticmethod
    def block(width):
        return pl.BlockSpec((TM // N_CTX, width, N_CTX), lambda i: (i, 0, 0))

    @staticmethod
    def shape(rows, width):
        return jax.ShapeDtypeStruct((rows // N_CTX, width, N_CTX), F32)


class _Loaded:
    def __init__(self, value):
        self._value = value

    def __getitem__(self, idx):
        return self._value


def _residual_then(inner, n_inner_inputs, og_ref, ow_ref, prev_mod_ref, x_ref, *refs):
    inner_inputs, (xn_ref, *inner_outputs) = refs[:n_inner_inputs - 1], refs[n_inner_inputs - 1:]
    out = jnp.dot(og_ref[...], ow_ref[...], preferred_element_type=F32)
    xn = x_ref[...] + prev_mod_ref[:, 2 * D:3 * D] * out
    xn_ref[...] = xn
    inner(_Loaded(xn), *inner_inputs, *inner_outputs)


def _inproj(kind, x, norm_w, mod, w, extra_in, outs, latent, rope_tabs=None, prev=None):
    rows = x.shape[0]
    per_seq = (N_LAT if latent else N_CTX) // TM
    mod_row = (lambda i: (i // per_seq, 0, 0)) if latent else (lambda i: (4, 0, 0))
    in_specs = [
        pl.BlockSpec((TM, D), lambda i: (i, 0)),
        pl.BlockSpec((1, D), lambda i: (0, 0)),
        pl.BlockSpec((None, 1, 3 * D), mod_row),
        pl.BlockSpec(w.shape, lambda i: (0, 0), pipeline_mode=pl.Buffered(1)),
    ]
    args = [x, norm_w.reshape(1, D), mod, w]
    for e in extra_in:
        in_specs.append(pl.BlockSpec(e.shape, lambda i: (0, 0)))
        args.append(e)
    rope = rope_tabs is not None
    if rope:
        for t in rope_tabs:
            in_specs.append(pl.BlockSpec((TM, LANES), lambda i: (i % per_seq, 0)))
            args.append(t)
    if kind == "a":
        body = functools.partial(_inproj_a_kernel, rope=rope)
    elif kind == "b":
        body = functools.partial(_inproj_b_kernel, rope=rope)
    else:
        body = _inproj_c_kernel
    out_specs = [_FEATURE_MAJOR.block(wd) if dt is _FEATURE_MAJOR else pl.BlockSpec((TM, wd), lambda i: (i, 0))
                 for wd, dt in outs]
    out_shape = [_FEATURE_MAJOR.shape(rows, wd) if dt is _FEATURE_MAJOR else jax.ShapeDtypeStruct((rows, wd), dt)
                 for wd, dt in outs]
    if prev is not None:
        og, out_w, prev_mod = prev
        body = functools.partial(_residual_then, body, len(in_specs))
        in_specs = [pl.BlockSpec((TM, D), lambda i: (i, 0)),
                    pl.BlockSpec((D, D), lambda i: (0, 0), pipeline_mode=pl.Buffered(1)),
                    pl.BlockSpec((None, 1, 3 * D), mod_row)] + in_specs
        args = [og, out_w, prev_mod] + args
        out_specs = [pl.BlockSpec((TM, D), lambda i: (i, 0))] + out_specs
        out_shape = [jax.ShapeDtypeStruct((rows, D), F32)] + out_shape
    return pl.pallas_call(
        body,
        grid=(rows // TM,),
        in_specs=in_specs,
        out_specs=out_specs,
        out_shape=out_shape,
        compiler_params=_params(("parallel",)),
        name=f"inproj_{kind}_{'lat' if latent else 'ctx'}",
    )(*args)


def _outproj_kernel(og_ref, w_ref, x_ref, mod_ref, *rest, final):
    out = jnp.dot(og_ref[...], w_ref[...], preferred_element_type=F32)
    xn = x_ref[...] + mod_ref[:, 2 * D:3 * D] * out
    if final:
        fw_ref, y_ref = rest
        y_ref[...] = _rms(xn, fw_ref[...])
    else:
        (o_ref,) = rest
        o_ref[...] = xn


def _outproj(og, w, x, mod, latent, final_w=None):
    rows = x.shape[0]
    per_seq = (N_LAT if latent else N_CTX) // TM
    mod_row = (lambda i: (i // per_seq, 0, 0)) if latent else (lambda i: (4, 0, 0))
    in_specs = [
        pl.BlockSpec((TM, D), lambda i: (i, 0)),
        pl.BlockSpec((D, D), lambda i: (0, 0)),
        pl.BlockSpec((TM, D), lambda i: (i, 0)),
        pl.BlockSpec((None, 1, 3 * D), mod_row),
    ]
    args = [og, w, x, mod]
    if final_w is not None:
        in_specs.append(pl.BlockSpec((1, D), lambda i: (0, 0)))
        args.append(final_w.reshape(1, D))
    return pl.pallas_call(
        functools.partial(_outproj_kernel, final=final_w is not None),
        grid=(rows // TM,),
        in_specs=in_specs,
        out_specs=pl.BlockSpec((TM, D), lambda i: (i, 0)),
        out_shape=jax.ShapeDtypeStruct((rows, D), F32),
        compiler_params=_params(("parallel",)),
        name=f"outproj_{'lat' if latent else 'ctx'}",
    )(*args)


class _PerUse:
    def __init__(self, make):
        self._make = make

    def __getitem__(self, i):
        return self._make(i)


def _lane_lo(shape):
    return lax.broadcasted_iota(jnp.int32, shape, 1) < DH


SOFTMAX_BLOCK_ELEMS = 32 * 1024
SOFTMAX_WHOLE_KEYS = 512


def _softmax_rows(sq, sk):
    if sk <= SOFTMAX_WHOLE_KEYS:
        return sq
    rows = 16
    while rows * 2 <= sq and sq % (rows * 2) == 0 and rows * 2 * sk <= SOFTMAX_BLOCK_ELEMS:
        rows *= 2
    return rows


Q_TILES_PER_KV_TILE = 4
_KV_ALIGNED_HEADS = (0, 4, 1, 5, 2, 6, 3, 7, 8, 12, 9, 13, 10, 14, 11, 15)


def _n_keys(x, feature_major):
    return x.shape[1] if feature_major else x.shape[0]


def _scores(lhs, k, feature_major):
    if feature_major:
        return jnp.dot(lhs, k, preferred_element_type=F32)
    return lax.dot_general(lhs, k, _NT, preferred_element_type=F32)


def _weighted_values(p, v, feature_major):
    if feature_major:
        return lax.dot_general(p, v, _NT, preferred_element_type=F32)
    return jnp.dot(p, v, preferred_element_type=F32)


def _gqa_block(sink_ref, q_ref, z_ref, o_ref, kv_pieces, lhs_s, s_s, e_s):
    sq = q_ref.shape[0]
    lo = _lane_lo((1, LANES))
    halves = (lo, jnp.logical_not(lo))
    n_kv_tiles = HKV_A // 2
    pieces = _PerUse(lambda j: [(k2.astype(BF16), v2.astype(BF16), valid, fm)
                                for k2, v2, valid, fm in kv_pieces(j)])
    spans, off = [], 0
    for k2, _, _, fm in pieces[0]:
        spans.append(slice(off, off + _n_keys(k2, fm)))
        off += _n_keys(k2, fm)
    step = _softmax_rows(sq, off)
    blocks = lambda j: [(t, a) for t in range(Q_TILES_PER_KV_TILE * j, Q_TILES_PER_KV_TILE * (j + 1))
                        for a in range(2)]

    def scores_stage(j):
        for i, (t, a) in enumerate(blocks(j)):
            q2 = q_ref[:, t * LANES:(t + 1) * LANES]
            lhs_s[j, i * sq:(i + 1) * sq, :] = jnp.where(halves[a], q2, jnp.zeros_like(q2))
        for (k2, _, _, fm), sp in zip(pieces[j], spans):
            s_s[j, :, sp] = _scores(lhs_s[j], k2, fm)

    def softmax_stage(j):
        inv = []
        for i, (t, a) in enumerate(blocks(j)):
            sink = sink_ref[2 * t + a] * LOG2E
            parts = []
            for r in range(0, sq, step):
                rws = slice(i * sq + r, i * sq + r + step)
                scores = []
                for (_, _, valid, _), sp in zip(pieces[j], spans):
                    s = s_s[j, rws, sp]
                    scores.append(s if valid is None
                                  else jnp.where(valid[r:r + step], s, -jnp.inf))
                m = sink
                for s in scores:
                    m = jnp.maximum(m, jnp.max(s, axis=-1, keepdims=True))
                denom = jnp.exp2(sink - m)
                for s, sp in zip(scores, spans):
                    e = jnp.exp2(s - m)
                    denom = denom + jnp.sum(e, axis=-1, keepdims=True)
                    e_s[j, rws, sp] = e.astype(BF16)
                parts.append(1.0 / denom)
            inv.append(jnp.concatenate(parts, axis=0))
        return inv

    def pv_stage(j, inv):
        pv = None
        for (_, v2, _, fm), sp in zip(pieces[j], spans):
            part = _weighted_values(e_s[j, :, sp], v2, fm)
            pv = part if pv is None else pv + part
        for n_t in range(Q_TILES_PER_KV_TILE):
            t = Q_TILES_PER_KV_TILE * j + n_t
            i0, i1 = 2 * n_t, 2 * n_t + 1
            o2 = jnp.where(lo, pv[i0 * sq:(i0 + 1) * sq] * inv[i0], pv[i1 * sq:(i1 + 1) * sq] * inv[i1])
            cols = slice(t * LANES, (t + 1) * LANES)
            o_ref[:, cols] = (o2 * _silu(z_ref[:, cols].astype(F32))).astype(BF16)

    scores_stage(0)
    for j in range(n_kv_tiles):
        if j + 1 < n_kv_tiles:
            scores_stage(j + 1)
        pv_stage(j, softmax_stage(j))


def _gqa_scratch(sq, sk):
    stacked = 2 * Q_TILES_PER_KV_TILE * sq
    tiles = HKV_A // 2
    return [pltpu.VMEM((tiles, stacked, LANES), BF16), pltpu.VMEM((tiles, stacked, sk), F32),
            pltpu.VMEM((tiles, stacked, sk), BF16)]


def _attn_a_ctx_kernel(sink_ref, q_ref, k_ref, v_ref, z_ref, o_ref, *scratch):
    def kv_pieces(j):
        rws = slice(j * LANES, (j + 1) * LANES)
        return [(k_ref[rws, :], v_ref[rws, :], None, True)]
    _gqa_block(sink_ref, q_ref, z_ref, o_ref, kv_pieces, *scratch)


def _attn_a_lat_kernel(sink_ref, q_ref, k_ref, v_ref, kc_ref, vc_ref, z_ref, o_ref, *scratch):
    n = pl.program_id(1)
    span = 3 * WINDOW
    start = pl.multiple_of(jnp.clip((n - 1) * WINDOW, 0, N_LAT - span), WINDOW)
    qi = n * WINDOW + lax.broadcasted_iota(jnp.int32, (WINDOW, span), 0)
    kj = start + lax.broadcasted_iota(jnp.int32, (WINDOW, span), 1)
    valid = jnp.abs(kj - qi) <= WINDOW

    def kv_pieces(j):
        cols = slice(j * LANES, (j + 1) * LANES)
        return [(k_ref[pl.ds(start, span), cols], v_ref[pl.ds(start, span), cols], valid, False),
                (kc_ref[cols, :], vc_ref[cols, :], None, True)]
    _gqa_block(sink_ref, q_ref, z_ref, o_ref, kv_pieces, *scratch)


def _attn_a(sink, q, k, v, z, cache=None):
    smem = pl.BlockSpec(memory_space=pltpu.SMEM)
    kvw = HKV_A * DH
    if cache is None:
        return pl.pallas_call(
            _attn_a_ctx_kernel,
            grid=(N_CTX_B,),
            in_specs=[smem,
                      pl.BlockSpec((N_CTX, D), lambda b: (b, 0)),
                      pl.BlockSpec((None, kvw, N_CTX), lambda b: (b, 0, 0)),
                      pl.BlockSpec((None, kvw, N_CTX), lambda b: (b, 0, 0)),
                      pl.BlockSpec((N_CTX, D), lambda b: (b, 0))],
            out_specs=pl.BlockSpec((N_CTX, D), lambda b: (b, 0)),
            out_shape=jax.ShapeDtypeStruct((N_CTX_B * N_CTX, D), BF16),
            scratch_shapes=_gqa_scratch(N_CTX, N_CTX),
            compiler_params=_params(("parallel",)),
            name="attn_a_ctx",
        )(sink, q, k, v, z)
    kc, vc = cache
    nq = N_LAT // WINDOW
    return pl.pallas_call(
        _attn_a_lat_kernel,
        grid=(N_LAT_B, nq),
        in_specs=[smem,
                  pl.BlockSpec((WINDOW, D), lambda b, n: (b * nq + n, 0)),
                  pl.BlockSpec((N_LAT, kvw), lambda b, n: (b, 0)),
                  pl.BlockSpec((N_LAT, kvw), lambda b, n: (b, 0)),
                  pl.BlockSpec((None, kvw, PAST), lambda b, n: (b, 0, 0)),
                  pl.BlockSpec((None, kvw, PAST), lambda b, n: (b, 0, 0)),
                  pl.BlockSpec((WINDOW, D), lambda b, n: (b * nq + n, 0))],
        out_specs=pl.BlockSpec((WINDOW, D), lambda b, n: (b * nq + n, 0)),
        out_shape=jax.ShapeDtypeStruct((N_LAT_B * N_LAT, D), BF16),
        scratch_shapes=_gqa_scratch(WINDOW, 3 * WINDOW + PAST),
        compiler_params=_params(("parallel", "arbitrary")),
        name="attn_a_lat",
    )(sink, q, k, v, kc, vc, z)


def _diff_attn_block(lam_refs, subln_ref, q_ref, z_ref, o_ref, kv_pieces, lam_init, lhs_s, s_s, a_s):
    lq1, lk1, lq2, lk2 = lam_refs
    dot_exp = lambda a, c: jnp.exp(jnp.sum(a[...] * c[...], axis=-1, keepdims=True))
    lam = dot_exp(lq1, lk1) - dot_exp(lq2, lk2) + lam_init
    sq = q_ref.shape[0]
    lo = _lane_lo((1, LANES))
    n_heads = D // LANES
    pieces = _PerUse(kv_pieces)
    spans, off = [], 0
    for k2, _, fm in pieces[0]:
        spans.append(slice(off, off + _n_keys(k2, fm)))
        off += _n_keys(k2, fm)
    step = _softmax_rows(sq, 2 * off)

    def scores_stage(h):
        buf = h % 2
        q2 = q_ref[:, h * LANES:(h + 1) * LANES]
        lhs_s[buf, 0:sq, :] = jnp.where(lo, q2, jnp.zeros_like(q2))
        lhs_s[buf, sq:2 * sq, :] = jnp.where(lo, jnp.zeros_like(q2), q2)
        for (k2, _, fm), sp in zip(pieces[h], spans):
            s_s[buf, :, sp] = _scores(lhs_s[buf], k2, fm)

    def softmax_stage(h):
        buf = h % 2
        inv = []
        for r in range(0, sq, step):
            exps, denoms = [], []
            for c in range(2):
                rws = slice(c * sq + r, c * sq + r + step)
                scores = [s_s[buf, rws, sp] for sp in spans]
                m = None
                for s in scores:
                    sm = jnp.max(s, axis=-1, keepdims=True)
                    m = sm if m is None else jnp.maximum(m, sm)
                es = [jnp.exp2(s - m) for s in scores]
                denom = None
                for e in es:
                    se = jnp.sum(e, axis=-1, keepdims=True)
                    denom = se if denom is None else denom + se
                exps.append(es)
                denoms.append(denom)
            ratio = lam * denoms[0] / denoms[1]
            for e1, e2, sp in zip(exps[0], exps[1], spans):
                a_s[buf, r:r + step, sp] = (e1 - ratio * e2).astype(BF16)
            inv.append(1.0 / denoms[0])
        return jnp.concatenate(inv, axis=0)

    def pv_stage(h, inv):
        buf = h % 2
        cols = slice(h * LANES, (h + 1) * LANES)
        o = None
        for (_, vx, _), sp in zip(pieces[h], spans):
            po = jnp.dot(a_s[buf, :, sp], vx, preferred_element_type=F32)
            o = po if o is None else o + po
        o = _rms(o * inv, subln_ref[...]) * (1.0 - lam_init)
        o_ref[:, cols] = (o * _silu(z_ref[:, cols].astype(F32))).astype(BF16)

    scores_stage(0)
    for h in range(n_heads):
        if h + 1 < n_heads:
            scores_stage(h + 1)
        pv_stage(h, softmax_stage(h))


def _diff_attn_scratch(sq, sk):
    return [pltpu.VMEM((2, 2 * sq, LANES), BF16), pltpu.VMEM((2, 2 * sq, sk), F32),
            pltpu.VMEM((2, sq, sk), BF16)]


def _attn_b_ctx_kernel(lq1, lk1, lq2, lk2, subln_ref, q_ref, k_ref, v_ref, z_ref, o_ref, *scratch, lam_init):
    def kv_pieces(h):
        cols = slice(h * LANES, (h + 1) * LANES)
        return [(k_ref[cols, :].astype(BF16), v_ref[:, cols].astype(BF16), True)]
    _diff_attn_block((lq1, lk1, lq2, lk2), subln_ref, q_ref, z_ref, o_ref, kv_pieces, lam_init, *scratch)


def _attn_b_lat_kernel(lq1, lk1, lq2, lk2, subln_ref, q_ref, k_ref, v_ref, kc_ref, vc_ref, z_ref, o_ref,
                       *scratch, lam_init):
    def kv_pieces(h):
        cols = slice(h * LANES, (h + 1) * LANES)
        return [(k_ref[:, cols].astype(BF16), v_ref[:, cols].astype(BF16), False),
                (kc_ref[cols, :].astype(BF16), vc_ref[:, cols].astype(BF16), True)]
    _diff_attn_block((lq1, lk1, lq2, lk2), subln_ref, q_ref, z_ref, o_ref, kv_pieces, lam_init, *scratch)


def _attn_b(lams, subln, q, k, v, z, lam_init, cache=None):
    small = [pl.BlockSpec((1, DH), lambda *_: (0, 0))] * 4 + [pl.BlockSpec((1, 2 * DH), lambda *_: (0, 0))]
    small_args = [l.reshape(1, DH) for l in lams] + [subln.reshape(1, 2 * DH)]
    if cache is None:
        blk = pl.BlockSpec((N_CTX, D), lambda b: (b, 0))
        return pl.pallas_call(
            functools.partial(_attn_b_ctx_kernel, lam_init=lam_init),
            grid=(N_CTX_B,),
            in_specs=small + [blk, pl.BlockSpec((None, D, N_CTX), lambda b: (b, 0, 0)), blk, blk],
            out_specs=blk,
            out_shape=jax.ShapeDtypeStruct((N_CTX_B * N_CTX, D), BF16),
            scratch_shapes=_diff_attn_scratch(N_CTX, N_CTX),
            compiler_params=_params(("parallel",)),
            name="attn_b_ctx",
        )(*small_args, q, k, v, z)
    kc, vc = cache
    tq = 256
    nq = N_LAT // tq
    qblk = pl.BlockSpec((tq, D), lambda b, n: (b * nq + n, 0))
    return pl.pallas_call(
        functools.partial(_attn_b_lat_kernel, lam_init=lam_init),
        grid=(N_LAT_B, nq),
        in_specs=small + [qblk,
                          pl.BlockSpec((N_LAT, D), lambda b, n: (b, 0)),
                          pl.BlockSpec((N_LAT, D), lambda b, n: (b, 0)),
                          pl.BlockSpec((None, D, PAST), lambda b, n: (b, 0, 0)),
                          pl.BlockSpec((PAST, D), lambda b, n: (b, 0)),
                          qblk],
        out_specs=qblk,
        out_shape=jax.ShapeDtypeStruct((N_LAT_B * N_LAT, D), BF16),
        scratch_shapes=_diff_attn_scratch(tq, N_LAT + PAST),
        compiler_params=_params(("parallel", "arbitrary")),
        name="attn_b_lat",
    )(*small_args, q, k, v, kc, vc, z)


def _mm(a, b):
    return jnp.dot(a.astype(BF16), b.astype(BF16), preferred_element_type=F32)


def _each(fn, *lists):
    return [fn(*xs) for xs in zip(*lists)]


def _unit_tri_inverse_residuals(ls, same_blk, _mm):
    rd = [jnp.where(same_blk, -l, 0.0) for l in ls]
    lo = [jnp.where(same_blk, 0.0, l) for l in ls]
    pk = _each(_mm, rd, rd)
    for it in range(3):
        t = _each(_mm, rd, pk)
        nxt = _each(_mm, pk, pk) if it < 2 else None
        rd = _each(lambda r, p, x: r + p + x, rd, pk, t)
        pk = nxt
    m = _each(lambda x, r: x + _mm(r, x), lo, rd)
    m2 = _each(_mm, m, m)
    q = _each(lambda a, a2: a2 - a - _mm(a, a2), m, m2)
    return _each(lambda a, r: a + r + _mm(a, r), q, rd)


def _gdn_kernel(*refs, n, hp, has_s0, want_state):
    it = iter(refs)
    qp_ref, kp_ref, vp_ref, cwq_ref, cwk_ref, cwv_ref, g_ref, z_ref, onw_ref = [next(it) for _ in range(9)]
    s0_ref = next(it) if has_s0 else None
    og_ref = next(it)
    sf_ref = next(it) if want_state else None
    q_s, k_s, v_s, beta_b, gc_b, gtot_b, u_s, wq_s, intra_s, bb_s, ab_s, st_s = it
    nc = n // CHUNK
    head0 = pl.program_id(1) * hp

    row = lax.broadcasted_iota(jnp.int32, (n, LANES), 0)
    lane = lax.broadcasted_iota(jnp.int32, (n, LANES), 1)

    def conv_silu(p_ref, cw_ref, cols):
        x = p_ref[:, cols].astype(F32)
        xm1 = jnp.where(row == 0, 0.0, pltpu.roll(x, 1, 0))
        xp1 = jnp.where(row == n - 1, 0.0, pltpu.roll(x, n - 1, 0))
        return _silu(cw_ref[0:1, cols] * xm1 + cw_ref[1:2, cols] * x + cw_ref[2:3, cols] * xp1)

    def l2n(x):
        return x * lax.rsqrt(jnp.sum(x * x, axis=-1, keepdims=True) + 1e-6)

    for p in range(hp):
        cols = slice(p * LANES, (p + 1) * LANES)
        q_s[p] = l2n(conv_silu(qp_ref, cwq_ref, cols)) * (DK_C ** -0.5)
        k_s[p] = l2n(conv_silu(kp_ref, cwk_ref, cols))
        v_s[p] = conv_silu(vp_ref, cwv_ref, cols)

    gates = g_ref[...]
    local = row & (CHUNK - 1)
    pre = gates
    suf = gates
    s = 1
    while s < CHUNK:
        pre = pre + jnp.where(local >= s, pltpu.roll(pre, s, 0), 0.0)
        suf = suf + jnp.where(local < CHUNK - s, pltpu.roll(suf, n - s, 0), 0.0)
        s *= 2
    tot = pre + suf - gates

    def col(x, idx):
        picked = jnp.sum(jnp.where(lane == idx, x, 0.0), axis=1, keepdims=True)
        return jnp.broadcast_to(picked, (n, LANES))

    for p in range(hp):
        for d in range(2):
            beta_b[p, d] = col(gates, d * H_C + head0 + p)
            gc_b[p, d] = col(pre if d == 0 else suf, 2 * H_C + d * H_C + head0 + p)
            gtot_b[p, d] = col(tot, 2 * H_C + d * H_C + head0 + p)

    ii = lax.broadcasted_iota(jnp.int32, (CHUNK, LANES), 0)
    lane2 = lax.broadcasted_iota(jnp.int32, (CHUNK, LANES), 1)
    fwd = lane2 < CHUNK
    jj = lane2 & (CHUNK - 1)
    same_blk = (ii >> 4) == (jj >> 4)
    diag = ii == jj
    ahead = jnp.where(fwd, ii - jj, jj - ii)
    incl = ahead >= 0
    strict = ahead > 0
    heads = range(hp)
    pairs = [(p, c) for p in heads for c in range(nc)]
    chains = [(p, c, d) for p, c in pairs for d in range(2)]
    rows = lambda c: pl.ds(c * CHUNK, CHUNK)
    rows2 = lambda c: pl.ds(c * LANES, LANES)
    zeros_b = jnp.zeros((CHUNK, LANES), BF16)

    def block_diag(x2):
        return jnp.concatenate([jnp.where(fwd, x2, 0.0), jnp.where(fwd, 0.0, x2)], axis=0).astype(BF16)

    def packed_mm(x2, y2):
        return jnp.dot(x2.astype(BF16), block_diag(y2), preferred_element_type=F32)

    kc = [k_s[p, rows(c), :] for p, c in pairs]
    qc = [q_s[p, rows(c), :] for p, c in pairs]
    beta = [beta_b[p, d, rows(c), :] for p, c, d in chains]
    gc = [gc_b[p, d, rows(c), :] for p, c, d in chains]
    kb = [kc[i // 2] * b for i, b in enumerate(beta)]
    eg = [jnp.exp(g) for g in gc]
    a = []
    for j in range(len(pairs)):
        k_b = kc[j].astype(BF16)
        lhs = jnp.concatenate([jnp.concatenate([kb[2 * j], kb[2 * j + 1]], axis=1),
                               jnp.concatenate([qc[j], qc[j]], axis=1)], axis=0).astype(BF16)
        rhs_nt = jnp.concatenate([jnp.concatenate([k_b, zeros_b], axis=1),
                                  jnp.concatenate([zeros_b, k_b], axis=1)], axis=0)
        a.append(lax.dot_general(lhs, rhs_nt, _NT, preferred_element_type=F32))
    kd = [(kc[i // 2] * jnp.exp(gtot_b[p, d, rows(c), :] - gc[i])).astype(BF16)
          for i, (p, c, d) in enumerate(chains)]
    ls = []
    for j, (p, c) in enumerate(pairs):
        gci = jnp.where(fwd, gc[2 * j], gc[2 * j + 1])
        gcj = jnp.sum(jnp.where(diag, gci, 0.0), axis=0, keepdims=True)
        decay = jnp.where(incl, jnp.exp(jnp.where(incl, gci - gcj, 0.0)), 0.0)
        ls.append(jnp.where(strict, a[j][:CHUNK] * decay, 0.0))
        intra_s[p, rows(c), :] = jnp.where(incl, a[j][CHUNK:] * decay, 0.0)
        for d in range(2):
            wq_s[p, d, pl.ds(c * LANES + CHUNK, CHUNK), :] = (qc[j] * eg[2 * j + d]).astype(BF16)
    r = _unit_tri_inverse_residuals(ls, same_blk, packed_mm)
    rhs = [jnp.concatenate([v_s[p, rows(c), :] * b, x * e], axis=1)
           for (p, c, d), b, x, e in zip(chains, beta, kb, eg)]
    zeros_w = jnp.zeros((CHUNK, 2 * LANES), BF16)
    sol = []
    for j in range(len(pairs)):
        rf, rb = rhs[2 * j], rhs[2 * j + 1]
        both = jnp.concatenate([jnp.concatenate([rf.astype(BF16), zeros_w], axis=1),
                                jnp.concatenate([zeros_w, rb.astype(BF16)], axis=1)], axis=0)
        x = jnp.dot(r[j].astype(BF16), both, preferred_element_type=F32)
        sol.extend([rf + x[:, :2 * LANES], rb + x[:, 2 * LANES:]])
    for (p, c, d), x in zip(chains, sol):
        u_s[p, d, rows(c), :] = x[:, :LANES]
        wq_s[p, d, pl.ds(c * LANES, CHUNK), :] = x[:, LANES:].astype(BF16)
    ba = _each(lambda t, x: lax.dot_general(t, x.astype(BF16), _TN, preferred_element_type=F32),
               kd, sol)
    for (p, c, d), x in zip(chains, ba):
        bb_s[p, d, rows2(c), :] = x[:, :LANES]
        ab_s[p, d, rows2(c), :] = x[:, LANES:].astype(BF16)

    scans = [(p, d) for p in heads for d in range(2)]
    if has_s0:
        state = [s0_ref[d, p] for p, d in scans]
    else:
        state = [jnp.zeros((DK_C, LANES), F32) for _ in scans]
    for i in range(nc):
        for j, (p, d) in enumerate(scans):
            c = i if d == 0 else nc - 1 - i
            st_b = state[j].astype(BF16)
            st_s[p, d, rows2(c), :] = st_b
            eg_tot = jnp.exp(gtot_b[p, d, pl.ds(c * CHUNK, 1), :])
            state[j] = (state[j] * eg_tot + bb_s[p, d, rows2(c), :]
                        - jnp.dot(ab_s[p, d, rows2(c), :], st_b, preferred_element_type=F32))

    wo = [jnp.dot(wq_s[p, d, rows2(c), :], st_s[p, d, rows2(c), :], preferred_element_type=F32)
          for p, c, d in chains]
    vnew = [u_s[p, d, rows(c), :] - x[:CHUNK] for (p, c, d), x in zip(chains, wo)]
    for j, (p, c) in enumerate(pairs):
        cols = slice(p * LANES, (p + 1) * LANES)
        oc = (wo[2 * j][CHUNK:] + wo[2 * j + 1][CHUNK:]
              + _mm(intra_s[p, rows(c), :], jnp.concatenate([vnew[2 * j], vnew[2 * j + 1]], axis=0)))
        og_ref[rows(c), cols] = (_rms(oc, onw_ref[...]) * _silu(z_ref[rows(c), cols].astype(F32))).astype(BF16)
    if want_state:
        for j, (p, d) in enumerate(scans):
            sf_ref[d, p] = state[j]


def _gdn(qkv, conv_w, gates, z, onorm_w, latent, s0=None):
    n = N_LAT if latent else N_CTX
    nb = N_LAT_B if latent else N_CTX_B
    nc = n // CHUNK
    want_state = not latent
    hp = 1 if latent else 4
    groups = H_C // hp
    col = lambda off: pl.BlockSpec((n, hp * LANES), lambda b, h, off=off: (b, off * groups + h))
    cw = lambda off: pl.BlockSpec((3, hp * LANES), lambda b, h, off=off: (0, off * groups + h))
    state_spec = pl.BlockSpec((None, 2, hp, DK_C, LANES), lambda b, h: (b, 0, h, 0, 0))
    in_specs = [col(0), col(1), col(2), cw(0), cw(1), cw(2),
                pl.BlockSpec((n, LANES), lambda b, h: (b, 0)),
                pl.BlockSpec((n, hp * LANES), lambda b, h: (b, h)),
                pl.BlockSpec((1, LANES), lambda b, h: (0, 0))]
    args = [qkv, qkv, qkv, conv_w, conv_w, conv_w, gates, z, onorm_w.reshape(1, LANES)]
    if s0 is not None:
        in_specs.append(state_spec)
        args.append(s0)
    out_specs = [pl.BlockSpec((n, hp * LANES), lambda b, h: (b, h))]
    out_shape = [jax.ShapeDtypeStruct((nb * n, D), BF16)]
    if want_state:
        out_specs.append(state_spec)
        out_shape.append(jax.ShapeDtypeStruct((nb, 2, H_C, DK_C, LANES), F32))
    seq = lambda: pltpu.VMEM((hp, n, LANES), F32)
    both = lambda: pltpu.VMEM((hp, 2, n, LANES), F32)
    per_chunk = lambda dt: pltpu.VMEM((hp, 2, nc * LANES, LANES), dt)
    scratch = [seq(), seq(), seq(), both(), both(), both(), both(),
               per_chunk(BF16), seq(), per_chunk(F32), per_chunk(BF16),
               per_chunk(BF16)]
    return pl.pallas_call(
        functools.partial(_gdn_kernel, n=n, hp=hp, has_s0=s0 is not None, want_state=want_state),
        grid=(nb, groups),
        in_specs=in_specs,
        out_specs=out_specs,
        out_shape=out_shape,
        scratch_shapes=scratch,
        compiler_params=_params(("parallel", "arbitrary")),
        name=f"gdn_{'lat' if latent else 'ctx'}",
    )(*args)


def _rope_tables():
    rows = N_LAT // GRID_W
    row = jnp.repeat(jnp.arange(rows), GRID_W).astype(F32)
    colp = jnp.tile(jnp.arange(GRID_W), rows).astype(F32)
    quarter = DH // 4
    inv = ROPE_BASE ** (-jnp.arange(quarter, dtype=F32) / quarter)
    ar, ac = row[:, None] * inv, colp[:, None] * inv
    cos = jnp.concatenate([jnp.cos(ar)] * 2 + [jnp.cos(ac)] * 2, axis=-1)
    sin = jnp.concatenate([-jnp.sin(ar), jnp.sin(ar), -jnp.sin(ac), jnp.sin(ac)], axis=-1)
    return jnp.tile(cos, (1, 2)), jnp.tile(sin, (1, 2))


def kernel(x_prompt, x_sample, cache_l0_k, cache_l0_v, cache_l1_k, cache_l1_v, state_l2, cache_l3_k, cache_l3_v, c, c_ctx, l0_norm_w, l0_mod_w, l0_mod_b, l0_in_w, l0_out_w, l0_sink, l1_norm_w, l1_mod_w, l1_mod_b, l1_in_w, l1_out_w, l1_lambda_q1, l1_lambda_k1, l1_lambda_q2, l1_lambda_k2, l1_subln_w, l2_norm_w, l2_mod_w, l2_mod_b, l2_in_w, l2_out_w, l2_conv_w, l2_a_log, l2_dt_bias, l2_onorm_w, l3_norm_w, l3_mod_w, l3_mod_b, l3_in_w, l3_out_w, l3_sink, final_norm_w):
    xc = x_prompt.reshape(N_CTX_B * N_CTX, D)
    xl = x_sample.reshape(N_LAT_B * N_LAT, D)
    cond = jnp.concatenate([c, c_ctx[None, :], jnp.zeros((3, D), F32)], axis=0)
    mods = _modulation(cond, (l0_mod_w, l1_mod_w, l2_mod_w, l3_mod_w), (l0_mod_b, l1_mod_b, l2_mod_b, l3_mod_b))
    rope_tabs = _rope_tables()
    kvw = HKV_A * DH
    new_state = []

    def feature_major(t):
        nd = t.ndim
        return jnp.transpose(t, (0, *range(2, nd), 1)).reshape(t.shape[0], -1, t.shape[1])

    def token_major(t, feature_dims):
        t = t.reshape(t.shape[0], *feature_dims, t.shape[-1])
        return jnp.transpose(t, (0, t.ndim - 1, *range(1, t.ndim - 1)))

    def project(kind, xs, prev, norm_w, mod, w, extra, outs_ctx, outs_lat, rope):
        res = []
        for g, (x, outs) in enumerate(zip(xs, (outs_ctx, outs_lat))):
            fused = None if prev is None else (prev[g], prev[2], prev[3])
            res.append(_inproj(kind, x, norm_w, mod, w, extra, outs, latent=bool(g),
                               rope_tabs=rope_tabs if (rope and g) else None, prev=fused))
        if prev is not None:
            xs = (res[0][0], res[1][0])
            res = [r[1:] for r in res]
        return xs, res[0], res[1]

    def layer_a(xs, prev, mod, norm_w, in_w, out_w, sink, cache_k, cache_v):
        head = lambda h: slice(h * DH, (h + 1) * DH)
        by_head = lambda cols: [cols[:, head(h)] for h in _KV_ALIGNED_HEADS]
        in_w, out_w = in_w.astype(BF16), out_w.astype(BF16)
        w = jnp.concatenate(by_head(in_w[:, :D]) + [in_w[:, D:D + 2 * kvw]] + by_head(in_w[:, D + 2 * kvw:]),
                            axis=1)
        wo = jnp.concatenate([out_w[head(h)] for h in _KV_ALIGNED_HEADS], axis=0)
        sink = jnp.concatenate([sink[h:h + 1] for h in _KV_ALIGNED_HEADS])
        outs = lambda kv_dt: [(D, BF16), (kvw, kv_dt), (kvw, kv_dt), (D, BF16)]
        xs, (qc, kc, vc, zc), (ql, kl, vl, zl) = project("a", xs, prev, norm_w, mod, w, [], outs(_FEATURE_MAJOR),
                                                         outs(BF16), rope=True)
        ogc = _attn_a(sink, qc, kc, vc, zc)
        ogl = _attn_a(sink, ql, kl, vl, zl, cache=(feature_major(cache_k), feature_major(cache_v)))
        new_state.extend([token_major(kc, (HKV_A, DH)), token_major(vc, (HKV_A, DH))])
        return xs, (ogc, ogl, wo, mod)

    xs, prev = layer_a((xc, xl), None, mods[0], l0_norm_w, l0_in_w, l0_out_w, l0_sink, cache_l0_k, cache_l0_v)

    lam_init = 0.8 - 0.6 * math.exp(-0.3 * 1)
    outs = lambda k_dt, v_dt: [(D, BF16), (D, k_dt), (D, v_dt), (D, BF16)]
    xs, (qc, kc, vc, zc), (ql, kl, vl, zl) = project("b", xs, prev, l1_norm_w, mods[1], l1_in_w.astype(BF16), [],
                                                     outs(_FEATURE_MAJOR, F32), outs(BF16, BF16), rope=True)
    lams = (l1_lambda_q1, l1_lambda_k1, l1_lambda_q2, l1_lambda_k2)
    ogc = _attn_b(lams, l1_subln_w, qc, kc, vc, zc, lam_init)
    ogl = _attn_b(lams, l1_subln_w, ql, kl, vl, zl, lam_init,
                  cache=(feature_major(cache_l1_k), cache_l1_v.reshape(N_LAT_B * PAST, D)))
    prev = (ogc, ogl, l1_out_w.astype(BF16), mods[1])
    new_state.extend([token_major(kc, (H_C, 2, DH)), vc.reshape(N_CTX_B, N_CTX, H_C, 2 * DH)])

    w = l2_in_w.astype(BF16)
    wg = jnp.pad(l2_in_w[:, 4 * D:], ((0, 0), (0, LANES - 4 * H_C))).astype(BF16)
    lane_pad = lambda p: jnp.pad(p.reshape(1, 2 * H_C), ((0, 0), (2 * H_C, LANES - 4 * H_C)))
    extra = [wg, lane_pad(l2_a_log), lane_pad(l2_dt_bias)]
    outs = [(3 * D, BF16), (D, BF16), (LANES, F32)]
    xs, (qkvc, zc, gc), (qkvl, zl, gl) = project("c", xs, prev, l2_norm_w, mods[2], w, extra, outs, outs,
                                                 rope=False)
    ogc, st_new = _gdn(qkvc, l2_conv_w, gc, zc, l2_onorm_w, latent=False)
    (ogl,) = _gdn(qkvl, l2_conv_w, gl, zl, l2_onorm_w, latent=True, s0=state_l2)
    prev = (ogc, ogl, l2_out_w.astype(BF16), mods[2])
    new_state.append(st_new)

    xs, (ogc, ogl, wo, mod) = layer_a(xs, prev, mods[3], l3_norm_w, l3_in_w, l3_out_w, l3_sink,
                                      cache_l3_k, cache_l3_v)
    yc = _outproj(ogc, wo, xs[0], mod, latent=False, final_w=final_norm_w)
    yl = _outproj(ogl, wo, xs[1], mod, latent=True, final_w=final_norm_w)
    return (yc.reshape(N_CTX_B, N_CTX, D), yl.reshape(N_LAT_B, N_LAT, D), *new_state)
```

```python
import functools
import math

import jax
import jax.numpy as jnp
from jax import lax
from jax.experimental import pallas as pl
from jax.experimental.pallas import tpu as pltpu

F32 = jnp.float32
BF16 = jnp.bfloat16

D = 1024
N_CTX_B, N_CTX = 16, 256
N_LAT_B, N_LAT = 4, 1024
PAST = 512
GRID_W = 64
ROPE_BASE = 10000.0
NORM_EPS = 1e-6
L2_EPS = 1e-6
DH = 64
HKV_A = 4
KVW_A = HKV_A * DH
WINDOW = 128
A_LAT_QROWS = 256
H_C = 8
DK_C = 128
CHUNK = 64
INV_BLOCK_LOG2 = 4
LANES = 128
TM = 512
PROJ_COLS = 512
LOG2E = math.log2(math.e)
Q_SCALE = DH ** -0.5 * LOG2E
VMEM_LIMIT = 48 * 1024 * 1024

_NT = (((1,), (1,)), ((), ()))
_TN = (((0,), (0,)), ((), ()))


def _sigmoid(x):
    return 1.0 / (1.0 + jnp.exp(-x))


def _silu(x):
    return x * _sigmoid(x)


def _softplus(x):
    return jnp.maximum(x, 0.0) + jnp.log1p(jnp.exp(-jnp.abs(x)))


def _rms(x, w):
    return x * lax.rsqrt(jnp.mean(x * x, axis=-1, keepdims=True) + NORM_EPS) * w


def _params(sem):
    return pltpu.CompilerParams(dimension_semantics=sem, vmem_limit_bytes=VMEM_LIMIT)


def _mod_kernel(cond_ref, w0, w1, w2, w3, b0, b1, b2, b3, o0, o1, o2, o3):
    a = _silu(cond_ref[...]).astype(BF16)
    for w, b, o in ((w0, b0, o0), (w1, b1, o1), (w2, b2, o2), (w3, b3, o3)):
        o[...] = jnp.dot(a, w[...].astype(BF16), preferred_element_type=F32) + b[...]


def _modulation(cond, mod_ws, mod_bs):
    tn = 512
    wspec = pl.BlockSpec((D, tn), lambda j: (0, j))
    bspec = pl.BlockSpec((1, tn), lambda j: (0, j))
    ospec = pl.BlockSpec((8, tn), lambda j: (0, j))
    outs = pl.pallas_call(
        _mod_kernel,
        grid=(3 * D // tn,),
        in_specs=[pl.BlockSpec((8, D), lambda j: (0, 0))] + [wspec] * 4 + [bspec] * 4,
        out_specs=[ospec] * 4,
        out_shape=[jax.ShapeDtypeStruct((8, 3 * D), F32)] * 4,
        compiler_params=_params(("arbitrary",)),
        name="adaln_mod",
    )(cond, *mod_ws, *[b.reshape(1, 3 * D) for b in mod_bs])
    return [o.reshape(8, 1, 3 * D) for o in outs]


def _adaln_h(x_ref, nw_ref, mod_ref):
    y = _rms(x_ref[...], nw_ref[...])
    return (y * (1.0 + mod_ref[:, D:2 * D]) + mod_ref[:, 0:D]).astype(BF16)


def _proj(h, w_ref, lo, hi):
    return jnp.dot(h, w_ref[:, lo:hi], preferred_element_type=F32)


def _rope(x, cos_ref, sin_ref):
    width = x.shape[1]
    lane = lax.broadcasted_iota(jnp.int32, x.shape, 1)
    partner = jnp.where((lane & 31) < 16, pltpu.roll(x, width - 16, 1), pltpu.roll(x, 16, 1))
    reps = width // LANES
    return x * jnp.tile(cos_ref[...], (1, reps)) + partner * jnp.tile(sin_ref[...], (1, reps))


def _store_cols(ref, cols, val):
    if len(ref.shape) == 2:
        ref[:, cols] = val.astype(ref.dtype)
    else:
        for s in range(ref.shape[0]):
            ref[s, cols, :] = val[s * N_CTX:(s + 1) * N_CTX, :].T.astype(ref.dtype)


def _inproj_a_kernel(*refs, rope):
    if rope:
        x_ref, nw_ref, mod_ref, w_ref, cos_ref, sin_ref, q_ref, k_ref, v_ref, z_ref = refs
    else:
        x_ref, nw_ref, mod_ref, w_ref, q_ref, k_ref, v_ref, z_ref = refs
    h = _adaln_h(x_ref, nw_ref, mod_ref)
    z0 = D + 2 * KVW_A
    for lo in range(0, D, PROJ_COLS):
        acc = _proj(h, w_ref, lo, lo + PROJ_COLS)
        if rope:
            acc = _rope(acc, cos_ref, sin_ref)
        q_ref[:, lo:lo + PROJ_COLS] = (acc * Q_SCALE).astype(BF16)
    kv = _proj(h, w_ref, D, z0)
    k = kv[:, :KVW_A]
    if rope:
        k = _rope(k, cos_ref, sin_ref)
    _store_cols(k_ref, slice(0, KVW_A), k)
    _store_cols(v_ref, slice(0, KVW_A), kv[:, KVW_A:])
    for lo in range(0, D, PROJ_COLS):
        z_ref[:, lo:lo + PROJ_COLS] = _proj(h, w_ref, z0 + lo, z0 + lo + PROJ_COLS).astype(z_ref.dtype)


def _inproj_b_kernel(*refs, rope):
    if rope:
        x_ref, nw_ref, mod_ref, w_ref, cos_ref, sin_ref, q_ref, k_ref, v_ref, z_ref = refs
    else:
        x_ref, nw_ref, mod_ref, w_ref, q_ref, k_ref, v_ref, z_ref = refs
    h = _adaln_h(x_ref, nw_ref, mod_ref)
    for lo in range(0, D, PROJ_COLS):
        sl = slice(lo, lo + PROJ_COLS)
        q = _proj(h, w_ref, lo, lo + PROJ_COLS)
        k = _proj(h, w_ref, D + lo, D + lo + PROJ_COLS)
        if rope:
            q = _rope(q, cos_ref, sin_ref)
            k = _rope(k, cos_ref, sin_ref)
        q_ref[:, sl] = (q * Q_SCALE).astype(BF16)
        _store_cols(k_ref, sl, k)
        _store_cols(v_ref, sl, _proj(h, w_ref, 2 * D + lo, 2 * D + lo + PROJ_COLS))
        z_ref[:, sl] = _proj(h, w_ref, 3 * D + lo, 3 * D + lo + PROJ_COLS).astype(z_ref.dtype)


def _inproj_c_kernel(x_ref, nw_ref, mod_ref, w_ref, wg_ref, alog_ref, dtb_ref, qkv_ref, z_ref, g_ref):
    h = _adaln_h(x_ref, nw_ref, mod_ref)
    for lo in range(0, 3 * D, PROJ_COLS):
        qkv_ref[:, lo:lo + PROJ_COLS] = _proj(h, w_ref, lo, lo + PROJ_COLS).astype(qkv_ref.dtype)
    for lo in range(0, D, PROJ_COLS):
        z_ref[:, lo:lo + PROJ_COLS] = _proj(h, w_ref, 3 * D + lo, 3 * D + lo + PROJ_COLS).astype(z_ref.dtype)
    acc = jnp.dot(h, wg_ref[...], preferred_element_type=F32)
    lane = lax.broadcasted_iota(jnp.int32, acc.shape, 1)
    g = -jnp.exp(alog_ref[...]) * _softplus(acc + dtb_ref[...])
    g_ref[...] = jnp.where(lane < 2 * H_C, _sigmoid(acc), g)


class _FeatureMajor:
    @staticmethod
    def block(width):
        return pl.BlockSpec((TM // N_CTX, width, N_CTX), lambda i: (i, 0, 0))

    @staticmethod
    def shape(rows, width):
        return jax.ShapeDtypeStruct((rows // N_CTX, width, N_CTX), F32)


class _Loaded:
    def __init__(self, value):
        self._value = value

    def __getitem__(self, idx):
        return self._value


def _residual_then(inner, n_inner_inputs, og_ref, ow_ref, prev_mod_ref, x_ref, *refs):
    inner_inputs, (xn_ref, *inner_outputs) = refs[:n_inner_inputs - 1], refs[n_inner_inputs - 1:]
    out = jnp.dot(og_ref[...], ow_ref[...], preferred_element_type=F32)
    xn = x_ref[...] + prev_mod_ref[:, 2 * D:3 * D] * out
    xn_ref[...] = xn
    inner(_Loaded(xn), *inner_inputs, *inner_outputs)


def _inproj(kind, x, norm_w, mod, w, extra_in, outs, latent, rope_tabs=None, prev=None):
    rows = x.shape[0]
    per_seq = (N_LAT if latent else N_CTX) // TM
    mod_row = (lambda i: (i // per_seq, 0, 0)) if latent else (lambda i: (4, 0, 0))
    in_specs = [
        pl.BlockSpec((TM, D), lambda i: (i, 0)),
        pl.BlockSpec((1, D), lambda i: (0, 0)),
        pl.BlockSpec((None, 1, 3 * D), mod_row),
        pl.BlockSpec(w.shape, lambda i: (0, 0), pipeline_mode=pl.Buffered(1)),
    ]
    args = [x, norm_w.reshape(1, D), mod, w]
    for e in extra_in:
        in_specs.append(pl.BlockSpec(e.shape, lambda i: (0, 0)))
        args.append(e)
    rope = rope_tabs is not None
    if rope:
        for t in rope_tabs:
            in_specs.append(pl.BlockSpec((TM, LANES), lambda i: (i % per_seq, 0)))
            args.append(t)
    if kind == "a":
        body = functools.partial(_inproj_a_kernel, rope=rope)
    elif kind == "b":
        body = functools.partial(_inproj_b_kernel, rope=rope)
    else:
        body = _inproj_c_kernel
    out_specs = [_FeatureMajor.block(wd) if dt is _FeatureMajor else pl.BlockSpec((TM, wd), lambda i: (i, 0))
                 for wd, dt in outs]
    out_shape = [_FeatureMajor.shape(rows, wd) if dt is _FeatureMajor else jax.ShapeDtypeStruct((rows, wd), dt)
                 for wd, dt in outs]
    if prev is not None:
        og, out_w, prev_mod = prev
        body = functools.partial(_residual_then, body, len(in_specs))
        in_specs = [pl.BlockSpec((TM, D), lambda i: (i, 0)),
                    pl.BlockSpec((D, D), lambda i: (0, 0), pipeline_mode=pl.Buffered(1)),
                    pl.BlockSpec((None, 1, 3 * D), mod_row)] + in_specs
        args = [og, out_w, prev_mod] + args
        out_specs = [pl.BlockSpec((TM, D), lambda i: (i, 0))] + out_specs
        out_shape = [jax.ShapeDtypeStruct((rows, D), F32)] + out_shape
    return pl.pallas_call(
        body,
        grid=(rows // TM,),
        in_specs=in_specs,
        out_specs=out_specs,
        out_shape=out_shape,
        compiler_params=_params(("parallel",)),
        name=f"inproj_{kind}_{'lat' if latent else 'ctx'}",
    )(*args)


def _outproj_kernel(og_ref, w_ref, x_ref, mod_ref, *rest, final):
    out = jnp.dot(og_ref[...], w_ref[...], preferred_element_type=F32)
    xn = x_ref[...] + mod_ref[:, 2 * D:3 * D] * out
    if final:
        fw_ref, y_ref = rest
        y_ref[...] = _rms(xn, fw_ref[...])
    else:
        (o_ref,) = rest
        o_ref[...] = xn


def _outproj(og, w, x, mod, latent, final_w=None):
    rows = x.shape[0]
    per_seq = (N_LAT if latent else N_CTX) // TM
    mod_row = (lambda i: (i // per_seq, 0, 0)) if latent else (lambda i: (4, 0, 0))
    in_specs = [
        pl.BlockSpec((TM, D), lambda i: (i, 0)),
        pl.BlockSpec((D, D), lambda i: (0, 0)),
        pl.BlockSpec((TM, D), lambda i: (i, 0)),
        pl.BlockSpec((None, 1, 3 * D), mod_row),
    ]
    args = [og, w, x, mod]
    if final_w is not None:
        in_specs.append(pl.BlockSpec((1, D), lambda i: (0, 0)))
        args.append(final_w.reshape(1, D))
    return pl.pallas_call(
        functools.partial(_outproj_kernel, final=final_w is not None),
        grid=(rows // TM,),
        in_specs=in_specs,
        out_specs=pl.BlockSpec((TM, D), lambda i: (i, 0)),
        out_shape=jax.ShapeDtypeStruct((rows, D), F32),
        compiler_params=_params(("parallel",)),
        name=f"outproj_{'lat' if latent else 'ctx'}",
    )(*args)


class _PerUse:
    def __init__(self, make):
        self._make = make

    def __getitem__(self, i):
        return self._make(i)


def _lane_lo(shape):
    return lax.broadcasted_iota(jnp.int32, shape, 1) < DH


SOFTMAX_BLOCK_ELEMS = 32 * 1024
SOFTMAX_WHOLE_KEYS = 512


def _softmax_rows(sq, sk):
    if sk <= SOFTMAX_WHOLE_KEYS:
        return sq
    rows = 16
    while rows * 2 <= sq and sq % (rows * 2) == 0 and rows * 2 * sk <= SOFTMAX_BLOCK_ELEMS:
        rows *= 2
    return rows


Q_TILES_PER_KV_TILE = 4
_KV_ALIGNED_HEADS = (0, 4, 1, 5, 2, 6, 3, 7, 8, 12, 9, 13, 10, 14, 11, 15)


def _n_keys(x, feature_major):
    return x.shape[1] if feature_major else x.shape[0]


def _scores(lhs, k, feature_major):
    if feature_major:
        return jnp.dot(lhs, k, preferred_element_type=F32)
    return lax.dot_general(lhs, k, _NT, preferred_element_type=F32)


def _weighted_values(p, v, feature_major):
    if feature_major:
        return lax.dot_general(p, v, _NT, preferred_element_type=F32)
    return jnp.dot(p, v, preferred_element_type=F32)


def _gqa_block(sink_ref, q_ref, z_ref, o_ref, kv_pieces, lhs_s, s_s, e_s):
    sq = q_ref.shape[0]
    lo = _lane_lo((1, LANES))
    halves = (lo, jnp.logical_not(lo))
    n_kv_tiles = HKV_A // 2
    pieces = _PerUse(lambda j: [(k2.astype(BF16), v2.astype(BF16), valid, fm)
                                for k2, v2, valid, fm in kv_pieces(j)])
    spans, off = [], 0
    for k2, _, _, fm in pieces[0]:
        spans.append(slice(off, off + _n_keys(k2, fm)))
        off += _n_keys(k2, fm)
    step = _softmax_rows(sq, off)
    blocks = lambda j: [(t, a) for t in range(Q_TILES_PER_KV_TILE * j, Q_TILES_PER_KV_TILE * (j + 1))
                        for a in range(2)]

    def scores_stage(j):
        for i, (t, a) in enumerate(blocks(j)):
            q2 = q_ref[:, t * LANES:(t + 1) * LANES]
            lhs_s[j, i * sq:(i + 1) * sq, :] = jnp.where(halves[a], q2, jnp.zeros_like(q2))
        for (k2, _, _, fm), sp in zip(pieces[j], spans):
            s_s[j, :, sp] = _scores(lhs_s[j], k2, fm)

    def softmax_stage(j):
        inv = []
        for i, (t, a) in enumerate(blocks(j)):
            sink = sink_ref[2 * t + a] * LOG2E
            parts = []
            for r in range(0, sq, step):
                rws = slice(i * sq + r, i * sq + r + step)
                scores = []
                for (_, _, valid, _), sp in zip(pieces[j], spans):
                    s = s_s[j, rws, sp]
                    scores.append(s if valid is None
                                  else jnp.where(valid[r:r + step], s, -jnp.inf))
                m = sink
                for s in scores:
                    m = jnp.maximum(m, jnp.max(s, axis=-1, keepdims=True))
                denom = jnp.exp2(sink - m)
                for s, sp in zip(scores, spans):
                    e = jnp.exp2(s - m)
                    denom = denom + jnp.sum(e, axis=-1, keepdims=True)
                    e_s[j, rws, sp] = e.astype(BF16)
                parts.append(1.0 / denom)
            inv.append(jnp.concatenate(parts, axis=0))
        return inv

    def pv_stage(j, inv):
        pv = None
        for (_, v2, _, fm), sp in zip(pieces[j], spans):
            part = _weighted_values(e_s[j, :, sp], v2, fm)
            pv = part if pv is None else pv + part
        for n_t in range(Q_TILES_PER_KV_TILE):
            t = Q_TILES_PER_KV_TILE * j + n_t
            i0, i1 = 2 * n_t, 2 * n_t + 1
            o2 = jnp.where(lo, pv[i0 * sq:(i0 + 1) * sq] * inv[i0], pv[i1 * sq:(i1 + 1) * sq] * inv[i1])
            cols = slice(t * LANES, (t + 1) * LANES)
            o_ref[:, cols] = (o2 * _silu(z_ref[:, cols].astype(F32))).astype(BF16)

    scores_stage(0)
    for j in range(n_kv_tiles):
        if j + 1 < n_kv_tiles:
            scores_stage(j + 1)
        pv_stage(j, softmax_stage(j))


def _gqa_scratch(sq, sk):
    stacked = 2 * Q_TILES_PER_KV_TILE * sq
    tiles = HKV_A // 2
    return [pltpu.VMEM((tiles, stacked, LANES), BF16), pltpu.VMEM((tiles, stacked, sk), F32),
            pltpu.VMEM((tiles, stacked, sk), BF16)]


def _attn_a_ctx_kernel(sink_ref, q_ref, k_ref, v_ref, z_ref, o_ref, *scratch):
    def kv_pieces(j):
        rws = slice(j * LANES, (j + 1) * LANES)
        return [(k_ref[rws, :], v_ref[rws, :], None, True)]
    _gqa_block(sink_ref, q_ref, z_ref, o_ref, kv_pieces, *scratch)


def _attn_a_lat_kernel(sink_ref, q_ref, k_ref, v_ref, kc_ref, vc_ref, z_ref, o_ref, *scratch):
    n = pl.program_id(1)
    span = A_LAT_QROWS + 2 * WINDOW
    start = pl.multiple_of(jnp.clip(n * A_LAT_QROWS - WINDOW, 0, N_LAT - span), WINDOW)
    qi = n * A_LAT_QROWS + lax.broadcasted_iota(jnp.int32, (A_LAT_QROWS, span), 0)
    kj = start + lax.broadcasted_iota(jnp.int32, (A_LAT_QROWS, span), 1)
    valid = jnp.abs(kj - qi) <= WINDOW

    def kv_pieces(j):
        cols = slice(j * LANES, (j + 1) * LANES)
        return [(k_ref[pl.ds(start, span), cols], v_ref[pl.ds(start, span), cols], valid, False),
                (kc_ref[cols, :], vc_ref[cols, :], None, True)]
    _gqa_block(sink_ref, q_ref, z_ref, o_ref, kv_pieces, *scratch)


def _attn_a(sink, q, k, v, z, cache=None):
    smem = pl.BlockSpec(memory_space=pltpu.SMEM)
    if cache is None:
        return pl.pallas_call(
            _attn_a_ctx_kernel,
            grid=(N_CTX_B,),
            in_specs=[smem,
                      pl.BlockSpec((N_CTX, D), lambda b: (b, 0)),
                      pl.BlockSpec((None, KVW_A, N_CTX), lambda b: (b, 0, 0)),
                      pl.BlockSpec((None, KVW_A, N_CTX), lambda b: (b, 0, 0)),
                      pl.BlockSpec((N_CTX, D), lambda b: (b, 0))],
            out_specs=pl.BlockSpec((N_CTX, D), lambda b: (b, 0)),
            out_shape=jax.ShapeDtypeStruct((N_CTX_B * N_CTX, D), BF16),
            scratch_shapes=_gqa_scratch(N_CTX, N_CTX),
            compiler_params=_params(("parallel",)),
            name="attn_a_ctx",
        )(sink, q, k, v, z)
    kc, vc = cache
    nq = N_LAT // A_LAT_QROWS
    return pl.pallas_call(
        _attn_a_lat_kernel,
        grid=(N_LAT_B, nq),
        in_specs=[smem,
                  pl.BlockSpec((A_LAT_QROWS, D), lambda b, n: (b * nq + n, 0)),
                  pl.BlockSpec((N_LAT, KVW_A), lambda b, n: (b, 0)),
                  pl.BlockSpec((N_LAT, KVW_A), lambda b, n: (b, 0)),
                  pl.BlockSpec((None, KVW_A, PAST), lambda b, n: (b, 0, 0)),
                  pl.BlockSpec((None, KVW_A, PAST), lambda b, n: (b, 0, 0)),
                  pl.BlockSpec((A_LAT_QROWS, D), lambda b, n: (b * nq + n, 0))],
        out_specs=pl.BlockSpec((A_LAT_QROWS, D), lambda b, n: (b * nq + n, 0)),
        out_shape=jax.ShapeDtypeStruct((N_LAT_B * N_LAT, D), BF16),
        scratch_shapes=_gqa_scratch(A_LAT_QROWS, A_LAT_QROWS + 2 * WINDOW + PAST),
        compiler_params=_params(("parallel", "arbitrary")),
        name="attn_a_lat",
    )(sink, q, k, v, kc, vc, z)


def _diff_attn_block(lam_refs, subln_ref, q_ref, z_ref, o_ref, kv_pieces, lam_init, lhs_s, s_s, a_s):
    lq1, lk1, lq2, lk2 = lam_refs
    dot_exp = lambda a, c: jnp.exp(jnp.sum(a[...] * c[...], axis=-1, keepdims=True))
    lam = dot_exp(lq1, lk1) - dot_exp(lq2, lk2) + lam_init
    sq = q_ref.shape[0]
    lo = _lane_lo((1, LANES))
    n_heads = D // LANES
    pieces = _PerUse(kv_pieces)
    spans, off = [], 0
    for k2, _, fm in pieces[0]:
        spans.append(slice(off, off + _n_keys(k2, fm)))
        off += _n_keys(k2, fm)
    step = _softmax_rows(sq, 2 * off)

    def scores_stage(h):
        buf = h % 2
        q2 = q_ref[:, h * LANES:(h + 1) * LANES]
        lhs_s[buf, 0:sq, :] = jnp.where(lo, q2, jnp.zeros_like(q2))
        lhs_s[buf, sq:2 * sq, :] = jnp.where(lo, jnp.zeros_like(q2), q2)
        for (k2, _, fm), sp in zip(pieces[h], spans):
            s_s[buf, :, sp] = _scores(lhs_s[buf], k2, fm)

    def softmax_stage(h):
        buf = h % 2
        inv = []
        for r in range(0, sq, step):
            exps, denoms = [], []
            for c in range(2):
                rws = slice(c * sq + r, c * sq + r + step)
                scores = [s_s[buf, rws, sp] for sp in spans]
                m = None
                for s in scores:
                    sm = jnp.max(s, axis=-1, keepdims=True)
                    m = sm if m is None else jnp.maximum(m, sm)
                es = [jnp.exp2(s - m) for s in scores]
                denom = None
                for e in es:
                    se = jnp.sum(e, axis=-1, keepdims=True)
                    denom = se if denom is None else denom + se
                exps.append(es)
                denoms.append(denom)
            ratio = lam * denoms[0] / denoms[1]
            for e1, e2, sp in zip(exps[0], exps[1], spans):
                a_s[buf, r:r + step, sp] = (e1 - ratio * e2).astype(BF16)
            inv.append(1.0 / denoms[0])
        return jnp.concatenate(inv, axis=0)

    def pv_stage(h, inv):
        buf = h % 2
        cols = slice(h * LANES, (h + 1) * LANES)
        o = None
        for (_, vx, _), sp in zip(pieces[h], spans):
            po = jnp.dot(a_s[buf, :, sp], vx, preferred_element_type=F32)
            o = po if o is None else o + po
        o = _rms(o * inv, subln_ref[...]) * (1.0 - lam_init)
        o_ref[:, cols] = (o * _silu(z_ref[:, cols].astype(F32))).astype(BF16)

    scores_stage(0)
    for h in range(n_heads):
        if h + 1 < n_heads:
            scores_stage(h + 1)
        pv_stage(h, softmax_stage(h))


def _diff_attn_scratch(sq, sk):
    return [pltpu.VMEM((2, 2 * sq, LANES), BF16), pltpu.VMEM((2, 2 * sq, sk), F32),
            pltpu.VMEM((2, sq, sk), BF16)]


def _attn_b_ctx_kernel(lq1, lk1, lq2, lk2, subln_ref, q_ref, k_ref, v_ref, z_ref, o_ref, *scratch, lam_init):
    def kv_pieces(h):
        cols = slice(h * LANES, (h + 1) * LANES)
        return [(k_ref[cols, :].astype(BF16), v_ref[:, cols].astype(BF16), True)]
    _diff_attn_block((lq1, lk1, lq2, lk2), subln_ref, q_ref, z_ref, o_ref, kv_pieces, lam_init, *scratch)


def _attn_b_lat_kernel(lq1, lk1, lq2, lk2, subln_ref, q_ref, k_ref, v_ref, kc_ref, vc_ref, z_ref, o_ref,
                       *scratch, lam_init):
    def kv_pieces(h):
        cols = slice(h * LANES, (h + 1) * LANES)
        return [(k_ref[:, cols].astype(BF16), v_ref[:, cols].astype(BF16), False),
                (kc_ref[cols, :].astype(BF16), vc_ref[:, cols].astype(BF16), True)]
    _diff_attn_block((lq1, lk1, lq2, lk2), subln_ref, q_ref, z_ref, o_ref, kv_pieces, lam_init, *scratch)


def _attn_b(lams, subln, q, k, v, z, lam_init, cache=None):
    small = [pl.BlockSpec((1, DH), lambda *_: (0, 0))] * 4 + [pl.BlockSpec((1, 2 * DH), lambda *_: (0, 0))]
    small_args = [l.reshape(1, DH) for l in lams] + [subln.reshape(1, 2 * DH)]
    if cache is None:
        blk = pl.BlockSpec((N_CTX, D), lambda b: (b, 0))
        return pl.pallas_call(
            functools.partial(_attn_b_ctx_kernel, lam_init=lam_init),
            grid=(N_CTX_B,),
            in_specs=small + [blk, pl.BlockSpec((None, D, N_CTX), lambda b: (b, 0, 0)), blk, blk],
            out_specs=blk,
            out_shape=jax.ShapeDtypeStruct((N_CTX_B * N_CTX, D), BF16),
            scratch_shapes=_diff_attn_scratch(N_CTX, N_CTX),
            compiler_params=_params(("parallel",)),
            name="attn_b_ctx",
        )(*small_args, q, k, v, z)
    kc, vc = cache
    tq = 512
    nq = N_LAT // tq
    qblk = pl.BlockSpec((tq, D), lambda b, n: (b * nq + n, 0))
    return pl.pallas_call(
        functools.partial(_attn_b_lat_kernel, lam_init=lam_init),
        grid=(N_LAT_B, nq),
        in_specs=small + [qblk,
                          pl.BlockSpec((N_LAT, D), lambda b, n: (b, 0)),
                          pl.BlockSpec((N_LAT, D), lambda b, n: (b, 0)),
                          pl.BlockSpec((None, D, PAST), lambda b, n: (b, 0, 0)),
                          pl.BlockSpec((PAST, D), lambda b, n: (b, 0)),
                          qblk],
        out_specs=qblk,
        out_shape=jax.ShapeDtypeStruct((N_LAT_B * N_LAT, D), BF16),
        scratch_shapes=_diff_attn_scratch(tq, N_LAT + PAST),
        compiler_params=_params(("parallel", "arbitrary")),
        name="attn_b_lat",
    )(*small_args, q, k, v, kc, vc, z)


def _mm(a, b):
    return jnp.dot(a.astype(BF16), b.astype(BF16), preferred_element_type=F32)


def _each(fn, *lists):
    return [fn(*xs) for xs in zip(*lists)]


def _unit_tri_inverse_residuals(ls, same_blk, mm):
    squarings = INV_BLOCK_LOG2 - 1
    rd = [jnp.where(same_blk, -l, 0.0) for l in ls]
    lo = [jnp.where(same_blk, 0.0, l) for l in ls]
    pk = _each(mm, rd, rd)
    for it in range(squarings):
        t = _each(mm, rd, pk)
        nxt = _each(mm, pk, pk) if it < squarings - 1 else None
        rd = _each(lambda r, p, x: r + p + x, rd, pk, t)
        pk = nxt
    m = _each(lambda x, r: x + mm(r, x), lo, rd)
    m2 = _each(mm, m, m)
    q = _each(lambda a, a2: a2 - a - mm(a, a2), m, m2)
    return _each(lambda a, r: a + r + mm(a, r), q, rd)


def _gdn_kernel(*refs, n, hp, has_s0, want_state):
    it = iter(refs)
    qp_ref, kp_ref, vp_ref, cwq_ref, cwk_ref, cwv_ref, g_ref, z_ref, onw_ref = [next(it) for _ in range(9)]
    s0_ref = next(it) if has_s0 else None
    og_ref = next(it)
    sf_ref = next(it) if want_state else None
    q_s, k_s, v_s, beta_b, gc_b, gtot_b, u_s, wq_s, intra_s, bb_s, ab_s, st_s = it
    nc = n // CHUNK
    head0 = pl.program_id(1) * hp

    row = lax.broadcasted_iota(jnp.int32, (n, LANES), 0)
    lane = lax.broadcasted_iota(jnp.int32, (n, LANES), 1)

    def conv_silu(p_ref, cw_ref, cols):
        x = p_ref[:, cols].astype(F32)
        xm1 = jnp.where(row == 0, 0.0, pltpu.roll(x, 1, 0))
        xp1 = jnp.where(row == n - 1, 0.0, pltpu.roll(x, n - 1, 0))
        return _silu(cw_ref[0:1, cols] * xm1 + cw_ref[1:2, cols] * x + cw_ref[2:3, cols] * xp1)

    def l2n(x):
        return x * lax.rsqrt(jnp.sum(x * x, axis=-1, keepdims=True) + L2_EPS)

    for p in range(hp):
        cols = slice(p * LANES, (p + 1) * LANES)
        q_s[p] = l2n(conv_silu(qp_ref, cwq_ref, cols)) * (DK_C ** -0.5)
        k_s[p] = l2n(conv_silu(kp_ref, cwk_ref, cols))
        v_s[p] = conv_silu(vp_ref, cwv_ref, cols)

    gates = g_ref[...]
    local = row & (CHUNK - 1)
    pre = gates
    suf = gates
    s = 1
    while s < CHUNK:
        pre = pre + jnp.where(local >= s, pltpu.roll(pre, s, 0), 0.0)
        suf = suf + jnp.where(local < CHUNK - s, pltpu.roll(suf, n - s, 0), 0.0)
        s *= 2
    tot = pre + suf - gates

    def col(x, idx):
        picked = jnp.sum(jnp.where(lane == idx, x, 0.0), axis=1, keepdims=True)
        return jnp.broadcast_to(picked, (n, LANES))

    for p in range(hp):
        for d in range(2):
            beta_b[p, d] = col(gates, d * H_C + head0 + p)
            gc_b[p, d] = col(pre if d == 0 else suf, 2 * H_C + d * H_C + head0 + p)
            gtot_b[p, d] = col(tot, 2 * H_C + d * H_C + head0 + p)

    ii = lax.broadcasted_iota(jnp.int32, (CHUNK, LANES), 0)
    lane2 = lax.broadcasted_iota(jnp.int32, (CHUNK, LANES), 1)
    fwd = lane2 < CHUNK
    jj = lane2 & (CHUNK - 1)
    same_blk = (ii >> INV_BLOCK_LOG2) == (jj >> INV_BLOCK_LOG2)
    assert CHUNK >> INV_BLOCK_LOG2 == 4
    diag = ii == jj
    ahead = jnp.where(fwd, ii - jj, jj - ii)
    incl = ahead >= 0
    strict = ahead > 0
    heads = range(hp)
    pairs = [(p, c) for p in heads for c in range(nc)]
    chains = [(p, c, d) for p, c in pairs for d in range(2)]
    rows = lambda c: pl.ds(c * CHUNK, CHUNK)
    rows2 = lambda c: pl.ds(c * LANES, LANES)
    zeros_b = jnp.zeros((CHUNK, LANES), BF16)

    def block_diag(x2):
        return jnp.concatenate([jnp.where(fwd, x2, 0.0), jnp.where(fwd, 0.0, x2)], axis=0).astype(BF16)

    def packed_mm(x2, y2):
        return jnp.dot(x2.astype(BF16), block_diag(y2), preferred_element_type=F32)

    kc = [k_s[p, rows(c), :] for p, c in pairs]
    qc = [q_s[p, rows(c), :] for p, c in pairs]
    beta = [beta_b[p, d, rows(c), :] for p, c, d in chains]
    gc = [gc_b[p, d, rows(c), :] for p, c, d in chains]
    kb = [kc[i // 2] * b for i, b in enumerate(beta)]
    eg = [jnp.exp(g) for g in gc]
    a = []
    for j in range(len(pairs)):
        k_b = kc[j].astype(BF16)
        lhs = jnp.concatenate([jnp.concatenate([kb[2 * j], kb[2 * j + 1]], axis=1),
                               jnp.concatenate([qc[j], qc[j]], axis=1)], axis=0).astype(BF16)
        rhs_nt = jnp.concatenate([jnp.concatenate([k_b, zeros_b], axis=1),
                                  jnp.concatenate([zeros_b, k_b], axis=1)], axis=0)
        a.append(lax.dot_general(lhs, rhs_nt, _NT, preferred_element_type=F32))
    kd = [(kc[i // 2] * jnp.exp(gtot_b[p, d, rows(c), :] - gc[i])).astype(BF16)
          for i, (p, c, d) in enumerate(chains)]
    ls = []
    for j, (p, c) in enumerate(pairs):
        gci = jnp.where(fwd, gc[2 * j], gc[2 * j + 1])
        gcj = jnp.sum(jnp.where(diag, gci, 0.0), axis=0, keepdims=True)
        decay = jnp.where(incl, jnp.exp(jnp.where(incl, gci - gcj, 0.0)), 0.0)
        ls.append(jnp.where(strict, a[j][:CHUNK] * decay, 0.0))
        intra_s[p, rows(c), :] = jnp.where(incl, a[j][CHUNK:] * decay, 0.0)
        for d in range(2):
            wq_s[p, d, pl.ds(c * LANES + CHUNK, CHUNK), :] = (qc[j] * eg[2 * j + d]).astype(BF16)
    r = _unit_tri_inverse_residuals(ls, same_blk, packed_mm)
    rhs = [jnp.concatenate([v_s[p, rows(c), :] * b, x * e], axis=1)
           for (p, c, d), b, x, e in zip(chains, beta, kb, eg)]
    zeros_w = jnp.zeros((CHUNK, 2 * LANES), BF16)
    sol = []
    for j in range(len(pairs)):
        rf, rb = rhs[2 * j], rhs[2 * j + 1]
        both = jnp.concatenate([jnp.concatenate([rf.astype(BF16), zeros_w], axis=1),
                                jnp.concatenate([zeros_w, rb.astype(BF16)], axis=1)], axis=0)
        x = jnp.dot(r[j].astype(BF16), both, preferred_element_type=F32)
        sol.extend([rf + x[:, :2 * LANES], rb + x[:, 2 * LANES:]])
    for (p, c, d), x in zip(chains, sol):
        u_s[p, d, rows(c), :] = x[:, :LANES]
        wq_s[p, d, pl.ds(c * LANES, CHUNK), :] = x[:, LANES:].astype(BF16)
    ba = _each(lambda t, x: lax.dot_general(t, x.astype(BF16), _TN, preferred_element_type=F32),
               kd, sol)
    for (p, c, d), x in zip(chains, ba):
        bb_s[p, d, rows2(c), :] = x[:, :LANES]
        ab_s[p, d, rows2(c), :] = x[:, LANES:].astype(BF16)

    scans = [(p, d) for p in heads for d in range(2)]
    if has_s0:
        state = [s0_ref[d, p] for p, d in scans]
    else:
        state = [jnp.zeros((DK_C, LANES), F32) for _ in scans]
    for i in range(nc):
        for j, (p, d) in enumerate(scans):
            c = i if d == 0 else nc - 1 - i
            st_b = state[j].astype(BF16)
            st_s[p, d, rows2(c), :] = st_b
            eg_tot = jnp.exp(gtot_b[p, d, pl.ds(c * CHUNK, 1), :])
            state[j] = (state[j] * eg_tot + bb_s[p, d, rows2(c), :]
                        - jnp.dot(ab_s[p, d, rows2(c), :], st_b, preferred_element_type=F32))

    wo = [jnp.dot(wq_s[p, d, rows2(c), :], st_s[p, d, rows2(c), :], preferred_element_type=F32)
          for p, c, d in chains]
    vnew = [u_s[p, d, rows(c), :] - x[:CHUNK] for (p, c, d), x in zip(chains, wo)]
    for j, (p, c) in enumerate(pairs):
        cols = slice(p * LANES, (p + 1) * LANES)
        oc = (wo[2 * j][CHUNK:] + wo[2 * j + 1][CHUNK:]
              + _mm(intra_s[p, rows(c), :], jnp.concatenate([vnew[2 * j], vnew[2 * j + 1]], axis=0)))
        og_ref[rows(c), cols] = (_rms(oc, onw_ref[...]) * _silu(z_ref[rows(c), cols].astype(F32))).astype(BF16)
    if want_state:
        for j, (p, d) in enumerate(scans):
            sf_ref[d, p] = state[j]


def _gdn(qkv, conv_w, gates, z, onorm_w, latent, s0=None):
    n = N_LAT if latent else N_CTX
    nb = N_LAT_B if latent else N_CTX_B
    nc = n // CHUNK
    want_state = not latent
    hp = 2 if latent else 8
    groups = H_C // hp
    col = lambda off: pl.BlockSpec((n, hp * LANES), lambda b, h, off=off: (b, off * groups + h))
    cw = lambda off: pl.BlockSpec((3, hp * LANES), lambda b, h, off=off: (0, off * groups + h))
    state_spec = pl.BlockSpec((None, 2, hp, DK_C, LANES), lambda b, h: (b, 0, h, 0, 0))
    in_specs = [col(0), col(1), col(2), cw(0), cw(1), cw(2),
                pl.BlockSpec((n, LANES), lambda b, h: (b, 0)),
                pl.BlockSpec((n, hp * LANES), lambda b, h: (b, h)),
                pl.BlockSpec((1, LANES), lambda b, h: (0, 0))]
    args = [qkv, qkv, qkv, conv_w, conv_w, conv_w, gates, z, onorm_w.reshape(1, LANES)]
    if s0 is not None:
        in_specs.append(state_spec)
        args.append(s0)
    out_specs = [pl.BlockSpec((n, hp * LANES), lambda b, h: (b, h))]
    out_shape = [jax.ShapeDtypeStruct((nb * n, D), BF16)]
    if want_state:
        out_specs.append(state_spec)
        out_shape.append(jax.ShapeDtypeStruct((nb, 2, H_C, DK_C, LANES), F32))
    seq = lambda: pltpu.VMEM((hp, n, LANES), F32)
    both = lambda: pltpu.VMEM((hp, 2, n, LANES), F32)
    per_chunk = lambda dt: pltpu.VMEM((hp, 2, nc * LANES, LANES), dt)
    scratch = [seq(), seq(), seq(), both(), both(), both(), both(),
               per_chunk(BF16), seq(), per_chunk(F32), per_chunk(BF16),
               per_chunk(BF16)]
    return pl.pallas_call(
        functools.partial(_gdn_kernel, n=n, hp=hp, has_s0=s0 is not None, want_state=want_state),
        grid=(nb, groups),
        in_specs=in_specs,
        out_specs=out_specs,
        out_shape=out_shape,
        scratch_shapes=scratch,
        compiler_params=_params(("parallel", "arbitrary")),
        name=f"gdn_{'lat' if latent else 'ctx'}",
    )(*args)


def _rope_tables():
    rows = N_LAT // GRID_W
    row = jnp.repeat(jnp.arange(rows), GRID_W).astype(F32)
    colp = jnp.tile(jnp.arange(GRID_W), rows).astype(F32)
    quarter = DH // 4
    inv = ROPE_BASE ** (-jnp.arange(quarter, dtype=F32) / quarter)
    ar, ac = row[:, None] * inv, colp[:, None] * inv
    cos = jnp.concatenate([jnp.cos(ar)] * 2 + [jnp.cos(ac)] * 2, axis=-1)
    sin = jnp.concatenate([-jnp.sin(ar), jnp.sin(ar), -jnp.sin(ac), jnp.sin(ac)], axis=-1)
    return jnp.tile(cos, (1, 2)), jnp.tile(sin, (1, 2))


def kernel(x_prompt, x_sample, cache_l0_k, cache_l0_v, cache_l1_k, cache_l1_v, state_l2, cache_l3_k, cache_l3_v, c, c_ctx, l0_norm_w, l0_mod_w, l0_mod_b, l0_in_w, l0_out_w, l0_sink, l1_norm_w, l1_mod_w, l1_mod_b, l1_in_w, l1_out_w, l1_lambda_q1, l1_lambda_k1, l1_lambda_q2, l1_lambda_k2, l1_subln_w, l2_norm_w, l2_mod_w, l2_mod_b, l2_in_w, l2_out_w, l2_conv_w, l2_a_log, l2_dt_bias, l2_onorm_w, l3_norm_w, l3_mod_w, l3_mod_b, l3_in_w, l3_out_w, l3_sink, final_norm_w):
    xc = x_prompt.reshape(N_CTX_B * N_CTX, D)
    xl = x_sample.reshape(N_LAT_B * N_LAT, D)
    cond = jnp.concatenate([c, c_ctx[None, :], jnp.zeros((3, D), F32)], axis=0)
    mods = _modulation(cond, (l0_mod_w, l1_mod_w, l2_mod_w, l3_mod_w), (l0_mod_b, l1_mod_b, l2_mod_b, l3_mod_b))
    rope_tabs = _rope_tables()
    new_state = []

    def feature_major(t):
        nd = t.ndim
        return jnp.transpose(t, (0, *range(2, nd), 1)).reshape(t.shape[0], -1, t.shape[1])

    def token_major(t, feature_dims):
        t = t.reshape(t.shape[0], *feature_dims, t.shape[-1])
        return jnp.transpose(t, (0, t.ndim - 1, *range(1, t.ndim - 1)))

    def project(kind, xs, prev, norm_w, mod, w, extra, outs_ctx, outs_lat, rope):
        res = []
        for g, (x, outs) in enumerate(zip(xs, (outs_ctx, outs_lat))):
            fused = None if prev is None else (prev[g], prev[2], prev[3])
            res.append(_inproj(kind, x, norm_w, mod, w, extra, outs, latent=bool(g),
                               rope_tabs=rope_tabs if (rope and g) else None, prev=fused))
        if prev is not None:
            xs = (res[0][0], res[1][0])
            res = [r[1:] for r in res]
        return xs, res[0], res[1]

    def layer_a(xs, prev, mod, norm_w, in_w, out_w, sink, cache_k, cache_v):
        head = lambda h: slice(h * DH, (h + 1) * DH)
        by_head = lambda cols: [cols[:, head(h)] for h in _KV_ALIGNED_HEADS]
        in_w, out_w = in_w.astype(BF16), out_w.astype(BF16)
        w = jnp.concatenate(by_head(in_w[:, :D]) + [in_w[:, D:D + 2 * KVW_A]] + by_head(in_w[:, D + 2 * KVW_A:]),
                            axis=1)
        wo = jnp.concatenate([out_w[head(h)] for h in _KV_ALIGNED_HEADS], axis=0)
        sink = jnp.concatenate([sink[h:h + 1] for h in _KV_ALIGNED_HEADS])
        outs = lambda kv_dt: [(D, BF16), (KVW_A, kv_dt), (KVW_A, kv_dt), (D, BF16)]
        xs, (qc, kc, vc, zc), (ql, kl, vl, zl) = project("a", xs, prev, norm_w, mod, w, [], outs(_FeatureMajor),
                                                         outs(BF16), rope=True)
        ogc = _attn_a(sink, qc, kc, vc, zc)
        ogl = _attn_a(sink, ql, kl, vl, zl, cache=(feature_major(cache_k), feature_major(cache_v)))
        new_state.extend([token_major(kc, (HKV_A, DH)), token_major(vc, (HKV_A, DH))])
        return xs, (ogc, ogl, wo, mod)

    xs, prev = layer_a((xc, xl), None, mods[0], l0_norm_w, l0_in_w, l0_out_w, l0_sink, cache_l0_k, cache_l0_v)

    lam_init = 0.8 - 0.6 * math.exp(-0.3 * 1)
    outs = lambda k_dt, v_dt: [(D, BF16), (D, k_dt), (D, v_dt), (D, BF16)]
    xs, (qc, kc, vc, zc), (ql, kl, vl, zl) = project("b", xs, prev, l1_norm_w, mods[1], l1_in_w.astype(BF16), [],
                                                     outs(_FeatureMajor, F32), outs(BF16, BF16), rope=True)
    lams = (l1_lambda_q1, l1_lambda_k1, l1_lambda_q2, l1_lambda_k2)
    ogc = _attn_b(lams, l1_subln_w, qc, kc, vc, zc, lam_init)
    ogl = _attn_b(lams, l1_subln_w, ql, kl, vl, zl, lam_init,
                  cache=(feature_major(cache_l1_k), cache_l1_v.reshape(N_LAT_B * PAST, D)))
    prev = (ogc, ogl, l1_out_w.astype(BF16), mods[1])
    new_state.extend([token_major(kc, (H_C, 2, DH)), vc.reshape(N_CTX_B, N_CTX, H_C, 2 * DH)])

    w = l2_in_w.astype(BF16)
    wg = jnp.pad(l2_in_w[:, 4 * D:], ((0, 0), (0, LANES - 4 * H_C))).astype(BF16)
    lane_pad = lambda p: jnp.pad(p.reshape(1, 2 * H_C), ((0, 0), (2 * H_C, LANES - 4 * H_C)))
    extra = [wg, lane_pad(l2_a_log), lane_pad(l2_dt_bias)]
    outs = [(3 * D, BF16), (D, BF16), (LANES, F32)]
    xs, (qkvc, zc, gc), (qkvl, zl, gl) = project("c", xs, prev, l2_norm_w, mods[2], w, extra, outs, outs,
                                                 rope=False)
    ogc, st_new = _gdn(qkvc, l2_conv_w, gc, zc, l2_onorm_w, latent=False)
    (ogl,) = _gdn(qkvl, l2_conv_w, gl, zl, l2_onorm_w, latent=True, s0=state_l2)
    prev = (ogc, ogl, l2_out_w.astype(BF16), mods[2])
    new_state.append(st_new)

    xs, (ogc, ogl, wo, mod) = layer_a(xs, prev, mods[3], l3_norm_w, l3_in_w, l3_out_w, l3_sink,
                                      cache_l3_k, cache_l3_v)
    yc = _outproj(ogc, wo, xs[0], mod, latent=False, final_w=final_norm_w)
    yl = _outproj(ogl, wo, xs[1], mod, latent=True, final_w=final_norm_w)
    return (yc.reshape(N_CTX_B, N_CTX, D), yl.reshape(N_LAT_B, N_LAT, D), *new_state)
```

```python
import functools
import math

import jax
import jax.numpy as jnp
from jax import lax
from jax.experimental import pallas as pl
from jax.experimental.pallas import tpu as pltpu

F32 = jnp.float32
BF16 = jnp.bfloat16

D = 1024
N_CTX_B, N_CTX = 16, 256
N_LAT_B, N_LAT = 4, 1024
PAST = 512
GRID_W = 64
ROPE_BASE = 10000.0
NORM_EPS = 1e-6
L2_EPS = 1e-6
DH = 64
HKV_A = 4
KVW_A = HKV_A * DH
WINDOW = 128
A_LAT_QROWS = 128
H_C = 8
DK_C = 128
CHUNK = 64
INV_BLOCK_LOG2 = 4
LANES = 128
TM = 512
PROJ_COLS = 512
LOG2E = math.log2(math.e)
Q_SCALE = DH ** -0.5 * LOG2E
VMEM_LIMIT = 48 * 1024 * 1024

_NT = (((1,), (1,)), ((), ()))
_TN = (((0,), (0,)), ((), ()))


def _sigmoid(x):
    return 1.0 / (1.0 + jnp.exp(-x))


def _silu(x):
    return x * _sigmoid(x)


def _softplus(x):
    return jnp.maximum(x, 0.0) + jnp.log1p(jnp.exp(-jnp.abs(x)))


def _rms(x, w):
    return x * lax.rsqrt(jnp.mean(x * x, axis=-1, keepdims=True) + NORM_EPS) * w


def _params(sem):
    return pltpu.CompilerParams(dimension_semantics=sem, vmem_limit_bytes=VMEM_LIMIT)


def _mod_kernel(cond_ref, w0, w1, w2, w3, b0, b1, b2, b3, o0, o1, o2, o3):
    a = _silu(cond_ref[...]).astype(BF16)
    for w, b, o in ((w0, b0, o0), (w1, b1, o1), (w2, b2, o2), (w3, b3, o3)):
        o[...] = jnp.dot(a, w[...].astype(BF16), preferred_element_type=F32) + b[...]


def _modulation(cond, mod_ws, mod_bs):
    tn = 512
    wspec = pl.BlockSpec((D, tn), lambda j: (0, j))
    bspec = pl.BlockSpec((1, tn), lambda j: (0, j))
    ospec = pl.BlockSpec((8, tn), lambda j: (0, j))
    outs = pl.pallas_call(
        _mod_kernel,
        grid=(3 * D // tn,),
        in_specs=[pl.BlockSpec((8, D), lambda j: (0, 0))] + [wspec] * 4 + [bspec] * 4,
        out_specs=[ospec] * 4,
        out_shape=[jax.ShapeDtypeStruct((8, 3 * D), F32)] * 4,
        compiler_params=_params(("arbitrary",)),
        name="adaln_mod",
    )(cond, *mod_ws, *[b.reshape(1, 3 * D) for b in mod_bs])
    return [o.reshape(8, 1, 3 * D) for o in outs]


def _adaln_h(x_ref, nw_ref, mod_ref):
    y = _rms(x_ref[...], nw_ref[...])
    return (y * (1.0 + mod_ref[:, D:2 * D]) + mod_ref[:, 0:D]).astype(BF16)


def _proj(h, w_ref, lo, hi):
    return jnp.dot(h, w_ref[:, lo:hi], preferred_element_type=F32)


def _rope(x, cos_ref, sin_ref):
    width = x.shape[1]
    lane = lax.broadcasted_iota(jnp.int32, x.shape, 1)
    partner = jnp.where((lane & 31) < 16, pltpu.roll(x, width - 16, 1), pltpu.roll(x, 16, 1))
    reps = width // LANES
    return x * jnp.tile(cos_ref[...], (1, reps)) + partner * jnp.tile(sin_ref[...], (1, reps))


def _store_cols(ref, cols, val):
    if len(ref.shape) == 2:
        ref[:, cols] = val.astype(ref.dtype)
    else:
        for s in range(ref.shape[0]):
            ref[s, cols, :] = val[s * N_CTX:(s + 1) * N_CTX, :].T.astype(ref.dtype)


def _inproj_a_kernel(*refs, rope):
    if rope:
        x_ref, nw_ref, mod_ref, w_ref, cos_ref, sin_ref, q_ref, k_ref, v_ref, z_ref = refs
    else:
        x_ref, nw_ref, mod_ref, w_ref, q_ref, k_ref, v_ref, z_ref = refs
    h = _adaln_h(x_ref, nw_ref, mod_ref)
    z0 = D + 2 * KVW_A
    for lo in range(0, D, PROJ_COLS):
        acc = _proj(h, w_ref, lo, lo + PROJ_COLS)
        if rope:
            acc = _rope(acc, cos_ref, sin_ref)
        q_ref[:, lo:lo + PROJ_COLS] = (acc * Q_SCALE).astype(BF16)
    kv = _proj(h, w_ref, D, z0)
    k = kv[:, :KVW_A]
    if rope:
        k = _rope(k, cos_ref, sin_ref)
    _store_cols(k_ref, slice(0, KVW_A), k)
    _store_cols(v_ref, slice(0, KVW_A), kv[:, KVW_A:])
    for lo in range(0, D, PROJ_COLS):
        z_ref[:, lo:lo + PROJ_COLS] = _proj(h, w_ref, z0 + lo, z0 + lo + PROJ_COLS).astype(z_ref.dtype)


def _inproj_b_kernel(*refs, rope):
    if rope:
        x_ref, nw_ref, mod_ref, w_ref, cos_ref, sin_ref, q_ref, k_ref, v_ref, z_ref = refs
    else:
        x_ref, nw_ref, mod_ref, w_ref, q_ref, k_ref, v_ref, z_ref = refs
    h = _adaln_h(x_ref, nw_ref, mod_ref)
    for lo in range(0, D, PROJ_COLS):
        sl = slice(lo, lo + PROJ_COLS)
        q = _proj(h, w_ref, lo, lo + PROJ_COLS)
        k = _proj(h, w_ref, D + lo, D + lo + PROJ_COLS)
        if rope:
            q = _rope(q, cos_ref, sin_ref)
            k = _rope(k, cos_ref, sin_ref)
        q_ref[:, sl] = (q * Q_SCALE).astype(BF16)
        _store_cols(k_ref, sl, k)
        _store_cols(v_ref, sl, _proj(h, w_ref, 2 * D + lo, 2 * D + lo + PROJ_COLS))
        z_ref[:, sl] = _proj(h, w_ref, 3 * D + lo, 3 * D + lo + PROJ_COLS).astype(z_ref.dtype)


def _inproj_c_kernel(x_ref, nw_ref, mod_ref, w_ref, wg_ref, alog_ref, dtb_ref, qkv_ref, z_ref, g_ref):
    h = _adaln_h(x_ref, nw_ref, mod_ref)
    for lo in range(0, 3 * D, PROJ_COLS):
        qkv_ref[:, lo:lo + PROJ_COLS] = _proj(h, w_ref, lo, lo + PROJ_COLS).astype(qkv_ref.dtype)
    for lo in range(0, D, PROJ_COLS):
        z_ref[:, lo:lo + PROJ_COLS] = _proj(h, w_ref, 3 * D + lo, 3 * D + lo + PROJ_COLS).astype(z_ref.dtype)
    acc = jnp.dot(h, wg_ref[...], preferred_element_type=F32)
    lane = lax.broadcasted_iota(jnp.int32, acc.shape, 1)
    g = -jnp.exp(alog_ref[...]) * _softplus(acc + dtb_ref[...])
    g_ref[...] = jnp.where(lane < 2 * H_C, _sigmoid(acc), g)


class _FeatureMajor:
    @staticmethod
    def block(width):
        return pl.BlockSpec((TM // N_CTX, width, N_CTX), lambda i: (i, 0, 0))

    @staticmethod
    def shape(rows, width):
        return jax.ShapeDtypeStruct((rows // N_CTX, width, N_CTX), F32)


class _Loaded:
    def __init__(self, value):
        self._value = value

    def __getitem__(self, idx):
        return self._value


def _residual_then(inner, n_inner_inputs, og_ref, ow_ref, prev_mod_ref, x_ref, *refs):
    inner_inputs, (xn_ref, *inner_outputs) = refs[:n_inner_inputs - 1], refs[n_inner_inputs - 1:]
    out = jnp.dot(og_ref[...], ow_ref[...], preferred_element_type=F32)
    xn = x_ref[...] + prev_mod_ref[:, 2 * D:3 * D] * out
    xn_ref[...] = xn
    inner(_Loaded(xn), *inner_inputs, *inner_outputs)


def _inproj(kind, x, norm_w, mod, w, extra_in, outs, latent, rope_tabs=None, prev=None):
    rows = x.shape[0]
    per_seq = (N_LAT if latent else N_CTX) // TM
    mod_row = (lambda i: (i // per_seq, 0, 0)) if latent else (lambda i: (4, 0, 0))
    in_specs = [
        pl.BlockSpec((TM, D), lambda i: (i, 0)),
        pl.BlockSpec((1, D), lambda i: (0, 0)),
        pl.BlockSpec((None, 1, 3 * D), mod_row),
        pl.BlockSpec(w.shape, lambda i: (0, 0), pipeline_mode=pl.Buffered(1)),
    ]
    args = [x, norm_w.reshape(1, D), mod, w]
    for e in extra_in:
        in_specs.append(pl.BlockSpec(e.shape, lambda i: (0, 0)))
        args.append(e)
    rope = rope_tabs is not None
    if rope:
        for t in rope_tabs:
            in_specs.append(pl.BlockSpec((TM, LANES), lambda i: (i % per_seq, 0)))
            args.append(t)
    if kind == "a":
        body = functools.partial(_inproj_a_kernel, rope=rope)
    elif kind == "b":
        body = functools.partial(_inproj_b_kernel, rope=rope)
    else:
        body = _inproj_c_kernel
    out_specs = [_FeatureMajor.block(wd) if dt is _FeatureMajor else pl.BlockSpec((TM, wd), lambda i: (i, 0))
                 for wd, dt in outs]
    out_shape = [_FeatureMajor.shape(rows, wd) if dt is _FeatureMajor else jax.ShapeDtypeStruct((rows, wd), dt)
                 for wd, dt in outs]
    if prev is not None:
        og, out_w, prev_mod = prev
        body = functools.partial(_residual_then, body, len(in_specs))
        in_specs = [pl.BlockSpec((TM, D), lambda i: (i, 0)),
                    pl.BlockSpec((D, D), lambda i: (0, 0), pipeline_mode=pl.Buffered(1)),
                    pl.BlockSpec((None, 1, 3 * D), mod_row)] + in_specs
        args = [og, out_w, prev_mod] + args
        out_specs = [pl.BlockSpec((TM, D), lambda i: (i, 0))] + out_specs
        out_shape = [jax.ShapeDtypeStruct((rows, D), F32)] + out_shape
    return pl.pallas_call(
        body,
        grid=(rows // TM,),
        in_specs=in_specs,
        out_specs=out_specs,
        out_shape=out_shape,
        compiler_params=_params(("parallel",)),
        name=f"inproj_{kind}_{'lat' if latent else 'ctx'}",
    )(*args)


def _outproj_kernel(og_ref, w_ref, x_ref, mod_ref, *rest, final):
    out = jnp.dot(og_ref[...], w_ref[...], preferred_element_type=F32)
    xn = x_ref[...] + mod_ref[:, 2 * D:3 * D] * out
    if final:
        fw_ref, y_ref = rest
        y_ref[...] = _rms(xn, fw_ref[...])
    else:
        (o_ref,) = rest
        o_ref[...] = xn


def _outproj(og, w, x, mod, latent, final_w=None):
    rows = x.shape[0]
    per_seq = (N_LAT if latent else N_CTX) // TM
    mod_row = (lambda i: (i // per_seq, 0, 0)) if latent else (lambda i: (4, 0, 0))
    in_specs = [
        pl.BlockSpec((TM, D), lambda i: (i, 0)),
        pl.BlockSpec((D, D), lambda i: (0, 0)),
        pl.BlockSpec((TM, D), lambda i: (i, 0)),
        pl.BlockSpec((None, 1, 3 * D), mod_row),
    ]
    args = [og, w, x, mod]
    if final_w is not None:
        in_specs.append(pl.BlockSpec((1, D), lambda i: (0, 0)))
        args.append(final_w.reshape(1, D))
    return pl.pallas_call(
        functools.partial(_outproj_kernel, final=final_w is not None),
        grid=(rows // TM,),
        in_specs=in_specs,
        out_specs=pl.BlockSpec((TM, D), lambda i: (i, 0)),
        out_shape=jax.ShapeDtypeStruct((rows, D), F32),
        compiler_params=_params(("parallel",)),
        name=f"outproj_{'lat' if latent else 'ctx'}",
    )(*args)


class _PerUse:
    def __init__(self, make):
        self._make = make

    def __getitem__(self, i):
        return self._make(i)


def _lane_lo(shape):
    return lax.broadcasted_iota(jnp.int32, shape, 1) < DH


SOFTMAX_BLOCK_ELEMS = 32 * 1024
SOFTMAX_WHOLE_KEYS = 512


def _softmax_rows(sq, sk):
    if sk <= SOFTMAX_WHOLE_KEYS:
        return sq
    rows = 16
    while rows * 2 <= sq and sq % (rows * 2) == 0 and rows * 2 * sk <= SOFTMAX_BLOCK_ELEMS:
        rows *= 2
    return rows


Q_TILES_PER_KV_TILE = 4
_KV_ALIGNED_HEADS = (0, 4, 1, 5, 2, 6, 3, 7, 8, 12, 9, 13, 10, 14, 11, 15)


def _n_keys(x, feature_major):
    return x.shape[1] if feature_major else x.shape[0]


def _scores(lhs, k, feature_major):
    if feature_major:
        return jnp.dot(lhs, k, preferred_element_type=F32)
    return lax.dot_general(lhs, k, _NT, preferred_element_type=F32)


def _weighted_values(p, v, feature_major):
    if feature_major:
        return lax.dot_general(p, v, _NT, preferred_element_type=F32)
    return jnp.dot(p, v, preferred_element_type=F32)


def _gqa_block(sink_ref, q_ref, z_ref, o_ref, kv_pieces, lhs_s, s_s, e_s):
    sq = q_ref.shape[0]
    lo = _lane_lo((1, LANES))
    halves = (lo, jnp.logical_not(lo))
    n_kv_tiles = HKV_A // 2
    pieces = _PerUse(lambda j: [(k2.astype(BF16), v2.astype(BF16), valid, fm)
                                for k2, v2, valid, fm in kv_pieces(j)])
    spans, off = [], 0
    for k2, _, _, fm in pieces[0]:
        spans.append(slice(off, off + _n_keys(k2, fm)))
        off += _n_keys(k2, fm)
    step = _softmax_rows(sq, off)
    blocks = lambda j: [(t, a) for t in range(Q_TILES_PER_KV_TILE * j, Q_TILES_PER_KV_TILE * (j + 1))
                        for a in range(2)]

    def scores_stage(j):
        for i, (t, a) in enumerate(blocks(j)):
            q2 = q_ref[:, t * LANES:(t + 1) * LANES]
            lhs_s[j, i * sq:(i + 1) * sq, :] = jnp.where(halves[a], q2, jnp.zeros_like(q2))
        for (k2, _, _, fm), sp in zip(pieces[j], spans):
            s_s[j, :, sp] = _scores(lhs_s[j], k2, fm)

    def softmax_stage(j):
        inv = []
        for i, (t, a) in enumerate(blocks(j)):
            sink = sink_ref[2 * t + a] * LOG2E
            parts = []
            for r in range(0, sq, step):
                rws = slice(i * sq + r, i * sq + r + step)
                scores = []
                for (_, _, valid, _), sp in zip(pieces[j], spans):
                    s = s_s[j, rws, sp]
                    scores.append(s if valid is None
                                  else jnp.where(valid[r:r + step], s, -jnp.inf))
                m = sink
                for s in scores:
                    m = jnp.maximum(m, jnp.max(s, axis=-1, keepdims=True))
                denom = jnp.exp2(sink - m)
                for s, sp in zip(scores, spans):
                    e = jnp.exp2(s - m)
                    denom = denom + jnp.sum(e, axis=-1, keepdims=True)
                    e_s[j, rws, sp] = e.astype(BF16)
                parts.append(1.0 / denom)
            inv.append(jnp.concatenate(parts, axis=0))
        return inv

    def pv_stage(j, inv):
        pv = None
        for (_, v2, _, fm), sp in zip(pieces[j], spans):
            part = _weighted_values(e_s[j, :, sp], v2, fm)
            pv = part if pv is None else pv + part
        for n_t in range(Q_TILES_PER_KV_TILE):
            t = Q_TILES_PER_KV_TILE * j + n_t
            i0, i1 = 2 * n_t, 2 * n_t + 1
            o2 = jnp.where(lo, pv[i0 * sq:(i0 + 1) * sq] * inv[i0], pv[i1 * sq:(i1 + 1) * sq] * inv[i1])
            cols = slice(t * LANES, (t + 1) * LANES)
            o_ref[:, cols] = (o2 * _silu(z_ref[:, cols].astype(F32))).astype(BF16)

    scores_stage(0)
    for j in range(n_kv_tiles):
        if j + 1 < n_kv_tiles:
            scores_stage(j + 1)
        pv_stage(j, softmax_stage(j))


def _gqa_scratch(sq, sk):
    stacked = 2 * Q_TILES_PER_KV_TILE * sq
    tiles = HKV_A // 2
    return [pltpu.VMEM((tiles, stacked, LANES), BF16), pltpu.VMEM((tiles, stacked, sk), F32),
            pltpu.VMEM((tiles, stacked, sk), BF16)]


def _attn_a_ctx_kernel(sink_ref, q_ref, k_ref, v_ref, z_ref, o_ref, *scratch):
    def kv_pieces(j):
        rws = slice(j * LANES, (j + 1) * LANES)
        return [(k_ref[rws, :], v_ref[rws, :], None, True)]
    _gqa_block(sink_ref, q_ref, z_ref, o_ref, kv_pieces, *scratch)


def _attn_a_lat_kernel(sink_ref, q_ref, k_ref, v_ref, kc_ref, vc_ref, z_ref, o_ref, *scratch):
    n = pl.program_id(1)
    span = A_LAT_QROWS + 2 * WINDOW
    start = pl.multiple_of(jnp.clip(n * A_LAT_QROWS - WINDOW, 0, N_LAT - span), WINDOW)
    qi = n * A_LAT_QROWS + lax.broadcasted_iota(jnp.int32, (A_LAT_QROWS, span), 0)
    kj = start + lax.broadcasted_iota(jnp.int32, (A_LAT_QROWS, span), 1)
    valid = jnp.abs(kj - qi) <= WINDOW

    def kv_pieces(j):
        cols = slice(j * LANES, (j + 1) * LANES)
        return [(k_ref[pl.ds(start, span), cols], v_ref[pl.ds(start, span), cols], valid, False),
                (kc_ref[cols, :], vc_ref[cols, :], None, True)]
    _gqa_block(sink_ref, q_ref, z_ref, o_ref, kv_pieces, *scratch)


def _attn_a(sink, q, k, v, z, cache=None):
    smem = pl.BlockSpec(memory_space=pltpu.SMEM)
    if cache is None:
        return pl.pallas_call(
            _attn_a_ctx_kernel,
            grid=(N_CTX_B,),
            in_specs=[smem,
                      pl.BlockSpec((N_CTX, D), lambda b: (b, 0)),
                      pl.BlockSpec((None, KVW_A, N_CTX), lambda b: (b, 0, 0)),
                      pl.BlockSpec((None, KVW_A, N_CTX), lambda b: (b, 0, 0)),
                      pl.BlockSpec((N_CTX, D), lambda b: (b, 0))],
            out_specs=pl.BlockSpec((N_CTX, D), lambda b: (b, 0)),
            out_shape=jax.ShapeDtypeStruct((N_CTX_B * N_CTX, D), BF16),
            scratch_shapes=_gqa_scratch(N_CTX, N_CTX),
            compiler_params=_params(("parallel",)),
            name="attn_a_ctx",
        )(sink, q, k, v, z)
    kc, vc = cache
    nq = N_LAT // A_LAT_QROWS
    return pl.pallas_call(
        _attn_a_lat_kernel,
        grid=(N_LAT_B, nq),
        in_specs=[smem,
                  pl.BlockSpec((A_LAT_QROWS, D), lambda b, n: (b * nq + n, 0)),
                  pl.BlockSpec((N_LAT, KVW_A), lambda b, n: (b, 0)),
                  pl.BlockSpec((N_LAT, KVW_A), lambda b, n: (b, 0)),
                  pl.BlockSpec((None, KVW_A, PAST), lambda b, n: (b, 0, 0)),
                  pl.BlockSpec((None, KVW_A, PAST), lambda b, n: (b, 0, 0)),
                  pl.BlockSpec((A_LAT_QROWS, D), lambda b, n: (b * nq + n, 0))],
        out_specs=pl.BlockSpec((A_LAT_QROWS, D), lambda b, n: (b * nq + n, 0)),
        out_shape=jax.ShapeDtypeStruct((N_LAT_B * N_LAT, D), BF16),
        scratch_shapes=_gqa_scratch(A_LAT_QROWS, A_LAT_QROWS + 2 * WINDOW + PAST),
        compiler_params=_params(("parallel", "arbitrary")),
        name="attn_a_lat",
    )(sink, q, k, v, kc, vc, z)


def _diff_attn_block(lam_refs, subln_ref, q_ref, z_ref, o_ref, kv_pieces, lam_init, lhs_s, s_s, a_s):
    lq1, lk1, lq2, lk2 = lam_refs
    dot_exp = lambda a, c: jnp.exp(jnp.sum(a[...] * c[...], axis=-1, keepdims=True))
    lam = dot_exp(lq1, lk1) - dot_exp(lq2, lk2) + lam_init
    sq = q_ref.shape[0]
    lo = _lane_lo((1, LANES))
    n_heads = D // LANES
    pieces = _PerUse(kv_pieces)
    spans, off = [], 0
    for k2, _, fm in pieces[0]:
        spans.append(slice(off, off + _n_keys(k2, fm)))
        off += _n_keys(k2, fm)
    step = _softmax_rows(sq, 2 * off)

    def scores_stage(h):
        buf = h % 2
        q2 = q_ref[:, h * LANES:(h + 1) * LANES]
        lhs_s[buf, 0:sq, :] = jnp.where(lo, q2, jnp.zeros_like(q2))
        lhs_s[buf, sq:2 * sq, :] = jnp.where(lo, jnp.zeros_like(q2), q2)
        for (k2, _, fm), sp in zip(pieces[h], spans):
            s_s[buf, :, sp] = _scores(lhs_s[buf], k2, fm)

    def softmax_stage(h):
        buf = h % 2
        inv = []
        for r in range(0, sq, step):
            exps, denoms = [], []
            for c in range(2):
                rws = slice(c * sq + r, c * sq + r + step)
                scores = [s_s[buf, rws, sp] for sp in spans]
                m = None
                for s in scores:
                    sm = jnp.max(s, axis=-1, keepdims=True)
                    m = sm if m is None else jnp.maximum(m, sm)
                es = [jnp.exp2(s - m) for s in scores]
                denom = None
                for e in es:
                    se = jnp.sum(e, axis=-1, keepdims=True)
                    denom = se if denom is None else denom + se
                exps.append(es)
                denoms.append(denom)
            ratio = lam * denoms[0] / denoms[1]
            for e1, e2, sp in zip(exps[0], exps[1], spans):
                a_s[buf, r:r + step, sp] = (e1 - ratio * e2).astype(BF16)
            inv.append(1.0 / denoms[0])
        return jnp.concatenate(inv, axis=0)

    def pv_stage(h, inv):
        buf = h % 2
        cols = slice(h * LANES, (h + 1) * LANES)
        o = None
        for (_, vx, _), sp in zip(pieces[h], spans):
            po = jnp.dot(a_s[buf, :, sp], vx, preferred_element_type=F32)
            o = po if o is None else o + po
        o = _rms(o * inv, subln_ref[...]) * (1.0 - lam_init)
        o_ref[:, cols] = (o * _silu(z_ref[:, cols].astype(F32))).astype(BF16)

    scores_stage(0)
    for h in range(n_heads):
        if h + 1 < n_heads:
            scores_stage(h + 1)
        pv_stage(h, softmax_stage(h))


def _diff_attn_scratch(sq, sk):
    return [pltpu.VMEM((2, 2 * sq, LANES), BF16), pltpu.VMEM((2, 2 * sq, sk), F32),
            pltpu.VMEM((2, sq, sk), BF16)]


def _attn_b_ctx_kernel(lq1, lk1, lq2, lk2, subln_ref, q_ref, k_ref, v_ref, z_ref, o_ref, *scratch, lam_init):
    def kv_pieces(h):
        cols = slice(h * LANES, (h + 1) * LANES)
        return [(k_ref[cols, :].astype(BF16), v_ref[:, cols].astype(BF16), True)]
    _diff_attn_block((lq1, lk1, lq2, lk2), subln_ref, q_ref, z_ref, o_ref, kv_pieces, lam_init, *scratch)


def _attn_b_lat_kernel(lq1, lk1, lq2, lk2, subln_ref, q_ref, k_ref, v_ref, kc_ref, vc_ref, z_ref, o_ref,
                       *scratch, lam_init):
    def kv_pieces(h):
        cols = slice(h * LANES, (h + 1) * LANES)
        return [(k_ref[:, cols].astype(BF16), v_ref[:, cols].astype(BF16), False),
                (kc_ref[cols, :].astype(BF16), vc_ref[:, cols].astype(BF16), True)]
    _diff_attn_block((lq1, lk1, lq2, lk2), subln_ref, q_ref, z_ref, o_ref, kv_pieces, lam_init, *scratch)


def _attn_b(lams, subln, q, k, v, z, lam_init, cache=None):
    small = [pl.BlockSpec((1, DH), lambda *_: (0, 0))] * 4 + [pl.BlockSpec((1, 2 * DH), lambda *_: (0, 0))]
    small_args = [l.reshape(1, DH) for l in lams] + [subln.reshape(1, 2 * DH)]
    if cache is None:
        blk = pl.BlockSpec((N_CTX, D), lambda b: (b, 0))
        return pl.pallas_call(
            functools.partial(_attn_b_ctx_kernel, lam_init=lam_init),
            grid=(N_CTX_B,),
            in_specs=small + [blk, pl.BlockSpec((None, D, N_CTX), lambda b: (b, 0, 0)), blk, blk],
            out_specs=blk,
            out_shape=jax.ShapeDtypeStruct((N_CTX_B * N_CTX, D), BF16),
            scratch_shapes=_diff_attn_scratch(N_CTX, N_CTX),
            compiler_params=_params(("parallel",)),
            name="attn_b_ctx",
        )(*small_args, q, k, v, z)
    kc, vc = cache
    tq = 512
    nq = N_LAT // tq
    qblk = pl.BlockSpec((tq, D), lambda b, n: (b * nq + n, 0))
    return pl.pallas_call(
        functools.partial(_attn_b_lat_kernel, lam_init=lam_init),
        grid=(N_LAT_B, nq),
        in_specs=small + [qblk,
                          pl.BlockSpec((N_LAT, D), lambda b, n: (b, 0)),
                          pl.BlockSpec((N_LAT, D), lambda b, n: (b, 0)),
                          pl.BlockSpec((None, D, PAST), lambda b, n: (b, 0, 0)),
                          pl.BlockSpec((PAST, D), lambda b, n: (b, 0)),
                          qblk],
        out_specs=qblk,
        out_shape=jax.ShapeDtypeStruct((N_LAT_B * N_LAT, D), BF16),
        scratch_shapes=_diff_attn_scratch(tq, N_LAT + PAST),
        compiler_params=_params(("parallel", "arbitrary")),
        name="attn_b_lat",
    )(*small_args, q, k, v, kc, vc, z)


def _mm(a, b):
    return jnp.dot(a.astype(BF16), b.astype(BF16), preferred_element_type=F32)


def _each(fn, *lists):
    return [fn(*xs) for xs in zip(*lists)]


def _unit_tri_inverse_residuals(ls, same_blk, mm):
    squarings = INV_BLOCK_LOG2 - 1
    rd = [jnp.where(same_blk, -l, 0.0) for l in ls]
    lo = [jnp.where(same_blk, 0.0, l) for l in ls]
    pk = _each(mm, rd, rd)
    for it in range(squarings):
        t = _each(mm, rd, pk)
        nxt = _each(mm, pk, pk) if it < squarings - 1 else None
        rd = _each(lambda r, p, x: r + p + x, rd, pk, t)
        pk = nxt
    m = _each(lambda x, r: x + mm(r, x), lo, rd)
    m2 = _each(mm, m, m)
    q = _each(lambda a, a2: a2 - a - mm(a, a2), m, m2)
    return _each(lambda a, r: a + r + mm(a, r), q, rd)


def _gdn_kernel(*refs, n, hp, has_s0, want_state):
    it = iter(refs)
    qp_ref, kp_ref, vp_ref, cwq_ref, cwk_ref, cwv_ref, g_ref, z_ref, onw_ref = [next(it) for _ in range(9)]
    s0_ref = next(it) if has_s0 else None
    og_ref = next(it)
    sf_ref = next(it) if want_state else None
    q_s, k_s, v_s, beta_b, gc_b, gtot_b, u_s, wq_s, intra_s, bb_s, ab_s, st_s = it
    nc = n // CHUNK
    head0 = pl.program_id(1) * hp

    row = lax.broadcasted_iota(jnp.int32, (n, LANES), 0)
    lane = lax.broadcasted_iota(jnp.int32, (n, LANES), 1)

    def conv_silu(p_ref, cw_ref, cols):
        x = p_ref[:, cols].astype(F32)
        xm1 = jnp.where(row == 0, 0.0, pltpu.roll(x, 1, 0))
        xp1 = jnp.where(row == n - 1, 0.0, pltpu.roll(x, n - 1, 0))
        return _silu(cw_ref[0:1, cols] * xm1 + cw_ref[1:2, cols] * x + cw_ref[2:3, cols] * xp1)

    def l2n(x):
        return x * lax.rsqrt(jnp.sum(x * x, axis=-1, keepdims=True) + L2_EPS)

    for p in range(hp):
        cols = slice(p * LANES, (p + 1) * LANES)
        q_s[p] = l2n(conv_silu(qp_ref, cwq_ref, cols)) * (DK_C ** -0.5)
        k_s[p] = l2n(conv_silu(kp_ref, cwk_ref, cols))
        v_s[p] = conv_silu(vp_ref, cwv_ref, cols)

    gates = g_ref[...]
    local = row & (CHUNK - 1)
    pre = gates
    suf = gates
    s = 1
    while s < CHUNK:
        pre = pre + jnp.where(local >= s, pltpu.roll(pre, s, 0), 0.0)
        suf = suf + jnp.where(local < CHUNK - s, pltpu.roll(suf, n - s, 0), 0.0)
        s *= 2
    tot = pre + suf - gates

    def col(x, idx):
        picked = jnp.sum(jnp.where(lane == idx, x, 0.0), axis=1, keepdims=True)
        return jnp.broadcast_to(picked, (n, LANES))

    for p in range(hp):
        for d in range(2):
            beta_b[p, d] = col(gates, d * H_C + head0 + p)
            gc_b[p, d] = col(pre if d == 0 else suf, 2 * H_C + d * H_C + head0 + p)
            gtot_b[p, d] = col(tot, 2 * H_C + d * H_C + head0 + p)

    ii = lax.broadcasted_iota(jnp.int32, (CHUNK, LANES), 0)
    lane2 = lax.broadcasted_iota(jnp.int32, (CHUNK, LANES), 1)
    fwd = lane2 < CHUNK
    jj = lane2 & (CHUNK - 1)
    same_blk = (ii >> INV_BLOCK_LOG2) == (jj >> INV_BLOCK_LOG2)
    assert CHUNK >> INV_BLOCK_LOG2 == 4
    diag = ii == jj
    ahead = jnp.where(fwd, ii - jj, jj - ii)
    incl = ahead >= 0
    strict = ahead > 0
    heads = range(hp)
    pairs = [(p, c) for p in heads for c in range(nc)]
    chains = [(p, c, d) for p, c in pairs for d in range(2)]
    rows = lambda c: pl.ds(c * CHUNK, CHUNK)
    rows2 = lambda c: pl.ds(c * LANES, LANES)
    zeros_b = jnp.zeros((CHUNK, LANES), BF16)

    def block_diag(x2):
        return jnp.concatenate([jnp.where(fwd, x2, 0.0), jnp.where(fwd, 0.0, x2)], axis=0).astype(BF16)

    def packed_mm(x2, y2):
        return jnp.dot(x2.astype(BF16), block_diag(y2), preferred_element_type=F32)

    kc = [k_s[p, rows(c), :] for p, c in pairs]
    qc = [q_s[p, rows(c), :] for p, c in pairs]
    beta = [beta_b[p, d, rows(c), :] for p, c, d in chains]
    gc = [gc_b[p, d, rows(c), :] for p, c, d in chains]
    kb = [kc[i // 2] * b for i, b in enumerate(beta)]
    eg = [jnp.exp(g) for g in gc]
    a = []
    for j in range(len(pairs)):
        k_b = kc[j].astype(BF16)
        lhs = jnp.concatenate([jnp.concatenate([kb[2 * j], kb[2 * j + 1]], axis=1),
                               jnp.concatenate([qc[j], qc[j]], axis=1)], axis=0).astype(BF16)
        rhs_nt = jnp.concatenate([jnp.concatenate([k_b, zeros_b], axis=1),
                                  jnp.concatenate([zeros_b, k_b], axis=1)], axis=0)
        a.append(lax.dot_general(lhs, rhs_nt, _NT, preferred_element_type=F32))
    kd = [(kc[i // 2] * jnp.exp(gtot_b[p, d, rows(c), :] - gc[i])).astype(BF16)
          for i, (p, c, d) in enumerate(chains)]
    ls = []
    for j, (p, c) in enumerate(pairs):
        gci = jnp.where(fwd, gc[2 * j], gc[2 * j + 1])
        gcj = jnp.sum(jnp.where(diag, gci, 0.0), axis=0, keepdims=True)
        decay = jnp.where(incl, jnp.exp(jnp.where(incl, gci - gcj, 0.0)), 0.0)
        ls.append(jnp.where(strict, a[j][:CHUNK] * decay, 0.0))
        intra_s[p, rows(c), :] = jnp.where(incl, a[j][CHUNK:] * decay, 0.0)
        for d in range(2):
            wq_s[p, d, pl.ds(c * LANES + CHUNK, CHUNK), :] = (qc[j] * eg[2 * j + d]).astype(BF16)
    r = _unit_tri_inverse_residuals(ls, same_blk, packed_mm)
    rhs = [jnp.concatenate([v_s[p, rows(c), :] * b, x * e], axis=1)
           for (p, c, d), b, x, e in zip(chains, beta, kb, eg)]
    zeros_w = jnp.zeros((CHUNK, 2 * LANES), BF16)
    sol = []
    for j in range(len(pairs)):
        rf, rb = rhs[2 * j], rhs[2 * j + 1]
        both = jnp.concatenate([jnp.concatenate([rf.astype(BF16), zeros_w], axis=1),
                                jnp.concatenate([zeros_w, rb.astype(BF16)], axis=1)], axis=0)
        x = jnp.dot(r[j].astype(BF16), both, preferred_element_type=F32)
        sol.extend([rf + x[:, :2 * LANES], rb + x[:, 2 * LANES:]])
    for (p, c, d), x in zip(chains, sol):
        u_s[p, d, rows(c), :] = x[:, :LANES]
        wq_s[p, d, pl.ds(c * LANES, CHUNK), :] = x[:, LANES:].astype(BF16)
    ba = _each(lambda t, x: lax.dot_general(t, x.astype(BF16), _TN, preferred_element_type=F32),
               kd, sol)
    for (p, c, d), x in zip(chains, ba):
        bb_s[p, d, rows2(c), :] = x[:, :LANES]
        ab_s[p, d, rows2(c), :] = x[:, LANES:].astype(BF16)

    scans = [(p, d) for p in heads for d in range(2)]
    if has_s0:
        state = [s0_ref[d, p] for p, d in scans]
    else:
        state = [jnp.zeros((DK_C, LANES), F32) for _ in scans]
    for i in range(nc):
        for j, (p, d) in enumerate(scans):
            c = i if d == 0 else nc - 1 - i
            st_b = state[j].astype(BF16)
            st_s[p, d, rows2(c), :] = st_b
            eg_tot = jnp.exp(gtot_b[p, d, pl.ds(c * CHUNK, 1), :])
            state[j] = (state[j] * eg_tot + bb_s[p, d, rows2(c), :]
                        - jnp.dot(ab_s[p, d, rows2(c), :], st_b, preferred_element_type=F32))

    wo = [jnp.dot(wq_s[p, d, rows2(c), :], st_s[p, d, rows2(c), :], preferred_element_type=F32)
          for p, c, d in chains]
    vnew = [u_s[p, d, rows(c), :] - x[:CHUNK] for (p, c, d), x in zip(chains, wo)]
    for j, (p, c) in enumerate(pairs):
        cols = slice(p * LANES, (p + 1) * LANES)
        oc = (wo[2 * j][CHUNK:] + wo[2 * j + 1][CHUNK:]
              + _mm(intra_s[p, rows(c), :], jnp.concatenate([vnew[2 * j], vnew[2 * j + 1]], axis=0)))
        og_ref[rows(c), cols] = (_rms(oc, onw_ref[...]) * _silu(z_ref[rows(c), cols].astype(F32))).astype(BF16)
    if want_state:
        for j, (p, d) in enumerate(scans):
            sf_ref[d, p] = state[j]


def _gdn(qkv, conv_w, gates, z, onorm_w, latent, s0=None):
    n = N_LAT if latent else N_CTX
    nb = N_LAT_B if latent else N_CTX_B
    nc = n // CHUNK
    want_state = not latent
    hp = 2 if latent else 8
    groups = H_C // hp
    col = lambda off: pl.BlockSpec((n, hp * LANES), lambda b, h, off=off: (b, off * groups + h))
    cw = lambda off: pl.BlockSpec((3, hp * LANES), lambda b, h, off=off: (0, off * groups + h))
    state_spec = pl.BlockSpec((None, 2, hp, DK_C, LANES), lambda b, h: (b, 0, h, 0, 0))
    in_specs = [col(0), col(1), col(2), cw(0), cw(1), cw(2),
                pl.BlockSpec((n, LANES), lambda b, h: (b, 0)),
                pl.BlockSpec((n, hp * LANES), lambda b, h: (b, h)),
                pl.BlockSpec((1, LANES), lambda b, h: (0, 0))]
    args = [qkv, qkv, qkv, conv_w, conv_w, conv_w, gates, z, onorm_w.reshape(1, LANES)]
    if s0 is not None:
        in_specs.append(state_spec)
        args.append(s0)
    out_specs = [pl.BlockSpec((n, hp * LANES), lambda b, h: (b, h))]
    out_shape = [jax.ShapeDtypeStruct((nb * n, D), BF16)]
    if want_state:
        out_specs.append(state_spec)
        out_shape.append(jax.ShapeDtypeStruct((nb, 2, H_C, DK_C, LANES), F32))
    seq = lambda: pltpu.VMEM((hp, n, LANES), F32)
    both = lambda: pltpu.VMEM((hp, 2, n, LANES), F32)
    per_chunk = lambda dt: pltpu.VMEM((hp, 2, nc * LANES, LANES), dt)
    scratch = [seq(), seq(), seq(), both(), both(), both(), both(),
               per_chunk(BF16), seq(), per_chunk(F32), per_chunk(BF16),
               per_chunk(BF16)]
    return pl.pallas_call(
        functools.partial(_gdn_kernel, n=n, hp=hp, has_s0=s0 is not None, want_state=want_state),
        grid=(nb, groups),
        in_specs=in_specs,
        out_specs=out_specs,
        out_shape=out_shape,
        scratch_shapes=scratch,
        compiler_params=_params(("parallel", "arbitrary")),
        name=f"gdn_{'lat' if latent else 'ctx'}",
    )(*args)


def _rope_tables():
    rows = N_LAT // GRID_W
    row = jnp.repeat(jnp.arange(rows), GRID_W).astype(F32)
    colp = jnp.tile(jnp.arange(GRID_W), rows).astype(F32)
    quarter = DH // 4
    inv = ROPE_BASE ** (-jnp.arange(quarter, dtype=F32) / quarter)
    ar, ac = row[:, None] * inv, colp[:, None] * inv
    cos = jnp.concatenate([jnp.cos(ar)] * 2 + [jnp.cos(ac)] * 2, axis=-1)
    sin = jnp.concatenate([-jnp.sin(ar), jnp.sin(ar), -jnp.sin(ac), jnp.sin(ac)], axis=-1)
    return jnp.tile(cos, (1, 2)), jnp.tile(sin, (1, 2))


def kernel(x_prompt, x_sample, cache_l0_k, cache_l0_v, cache_l1_k, cache_l1_v, state_l2, cache_l3_k, cache_l3_v, c, c_ctx, l0_norm_w, l0_mod_w, l0_mod_b, l0_in_w, l0_out_w, l0_sink, l1_norm_w, l1_mod_w, l1_mod_b, l1_in_w, l1_out_w, l1_lambda_q1, l1_lambda_k1, l1_lambda_q2, l1_lambda_k2, l1_subln_w, l2_norm_w, l2_mod_w, l2_mod_b, l2_in_w, l2_out_w, l2_conv_w, l2_a_log, l2_dt_bias, l2_onorm_w, l3_norm_w, l3_mod_w, l3_mod_b, l3_in_w, l3_out_w, l3_sink, final_norm_w):
    xc = x_prompt.reshape(N_CTX_B * N_CTX, D)
    xl = x_sample.reshape(N_LAT_B * N_LAT, D)
    cond = jnp.concatenate([c, c_ctx[None, :], jnp.zeros((3, D), F32)], axis=0)
    mods = _modulation(cond, (l0_mod_w, l1_mod_w, l2_mod_w, l3_mod_w), (l0_mod_b, l1_mod_b, l2_mod_b, l3_mod_b))
    rope_tabs = _rope_tables()
    new_state = []

    def feature_major(t):
        nd = t.ndim
        return jnp.transpose(t, (0, *range(2, nd), 1)).reshape(t.shape[0], -1, t.shape[1])

    def token_major(t, feature_dims):
        t = t.reshape(t.shape[0], *feature_dims, t.shape[-1])
        return jnp.transpose(t, (0, t.ndim - 1, *range(1, t.ndim - 1)))

    def project(kind, xs, prev, norm_w, mod, w, extra, outs_ctx, outs_lat, rope):
        res = []
        for g, (x, outs) in enumerate(zip(xs, (outs_ctx, outs_lat))):
            fused = None if prev is None else (prev[g], prev[2], prev[3])
            res.append(_inproj(kind, x, norm_w, mod, w, extra, outs, latent=bool(g),
                               rope_tabs=rope_tabs if (rope and g) else None, prev=fused))
        if prev is not None:
            xs = (res[0][0], res[1][0])
            res = [r[1:] for r in res]
        return xs, res[0], res[1]

    def layer_a(xs, prev, mod, norm_w, in_w, out_w, sink, cache_k, cache_v):
        head = lambda h: slice(h * DH, (h + 1) * DH)
        by_head = lambda cols: [cols[:, head(h)] for h in _KV_ALIGNED_HEADS]
        in_w, out_w = in_w.astype(BF16), out_w.astype(BF16)
        w = jnp.concatenate(by_head(in_w[:, :D]) + [in_w[:, D:D + 2 * KVW_A]] + by_head(in_w[:, D + 2 * KVW_A:]),
                            axis=1)
        wo = jnp.concatenate([out_w[head(h)] for h in _KV_ALIGNED_HEADS], axis=0)
        sink = jnp.concatenate([sink[h:h + 1] for h in _KV_ALIGNED_HEADS])
        outs = lambda kv_dt: [(D, BF16), (KVW_A, kv_dt), (KVW_A, kv_dt), (D, BF16)]
        xs, (qc, kc, vc, zc), (ql, kl, vl, zl) = project("a", xs, prev, norm_w, mod, w, [], outs(_FeatureMajor),
                                                         outs(BF16), rope=True)
        ogc = _attn_a(sink, qc, kc, vc, zc)
        ogl = _attn_a(sink, ql, kl, vl, zl, cache=(feature_major(cache_k), feature_major(cache_v)))
        new_state.extend([token_major(kc, (HKV_A, DH)), token_major(vc, (HKV_A, DH))])
        return xs, (ogc, ogl, wo, mod)

    xs, prev = layer_a((xc, xl), None, mods[0], l0_norm_w, l0_in_w, l0_out_w, l0_sink, cache_l0_k, cache_l0_v)

    lam_init = 0.8 - 0.6 * math.exp(-0.3 * 1)
    outs = lambda k_dt, v_dt: [(D, BF16), (D, k_dt), (D, v_dt), (D, BF16)]
    xs, (qc, kc, vc, zc), (ql, kl, vl, zl) = project("b", xs, prev, l1_norm_w, mods[1], l1_in_w.astype(BF16), [],
                                                     outs(_FeatureMajor, F32), outs(BF16, BF16), rope=True)
    lams = (l1_lambda_q1, l1_lambda_k1, l1_lambda_q2, l1_lambda_k2)
    ogc = _attn_b(lams, l1_subln_w, qc, kc, vc, zc, lam_init)
    ogl = _attn_b(lams, l1_subln_w, ql, kl, vl, zl, lam_init,
                  cache=(feature_major(cache_l1_k), cache_l1_v.reshape(N_LAT_B * PAST, D)))
    prev = (ogc, ogl, l1_out_w.astype(BF16), mods[1])
    new_state.extend([token_major(kc, (H_C, 2, DH)), vc.reshape(N_CTX_B, N_CTX, H_C, 2 * DH)])

    w = l2_in_w.astype(BF16)
    wg = jnp.pad(l2_in_w[:, 4 * D:], ((0, 0), (0, LANES - 4 * H_C))).astype(BF16)
    lane_pad = lambda p: jnp.pad(p.reshape(1, 2 * H_C), ((0, 0), (2 * H_C, LANES - 4 * H_C)))
    extra = [wg, lane_pad(l2_a_log), lane_pad(l2_dt_bias)]
    outs = [(3 * D, BF16), (D, BF16), (LANES, F32)]
    xs, (qkvc, zc, gc), (qkvl, zl, gl) = project("c", xs, prev, l2_norm_w, mods[2], w, extra, outs, outs,
                                                 rope=False)
    ogc, st_new = _gdn(qkvc, l2_conv_w, gc, zc, l2_onorm_w, latent=False)
    (ogl,) = _gdn(qkvl, l2_conv_w, gl, zl, l2_onorm_w, latent=True, s0=state_l2)
    prev = (ogc, ogl, l2_out_w.astype(BF16), mods[2])
    new_state.append(st_new)

    xs, (ogc, ogl, wo, mod) = layer_a(xs, prev, mods[3], l3_norm_w, l3_in_w, l3_out_w, l3_sink,
                                      cache_l3_k, cache_l3_v)
    yc = _outproj(ogc, wo, xs[0], mod, latent=False, final_w=final_norm_w)
    yl = _outproj(ogl, wo, xs[1], mod, latent=True, final_w=final_norm_w)
    return (yc.reshape(N_CTX_B, N_CTX, D), yl.reshape(N_LAT_B, N_LAT, D), *new_state)
```

```python
import functools
import math

import jax
import jax.numpy as jnp
from jax import lax
from jax.experimental import pallas as pl
from jax.experimental.pallas import tpu as pltpu

F32 = jnp.float32
BF16 = jnp.bfloat16

D = 1024
N_CTX_B, N_CTX = 16, 256
N_LAT_B, N_LAT = 4, 1024
PAST = 512
GRID_W = 64
ROPE_BASE = 10000.0
NORM_EPS = 1e-6
L2_EPS = 1e-6
DH = 64
HKV_A = 4
KVW_A = HKV_A * DH
WINDOW = 128
A_LAT_QROWS = 128
H_C = 8
DK_C = 128
CHUNK = 64
INV_BLOCK_LOG2 = 4
LANES = 128
TM = 512
PROJ_COLS = 512
LOG2E = math.log2(math.e)
Q_SCALE = DH ** -0.5 * LOG2E
VMEM_LIMIT = 48 * 1024 * 1024

_NT = (((1,), (1,)), ((), ()))
_TN = (((0,), (0,)), ((), ()))


def _sigmoid(x):
    return 1.0 / (1.0 + jnp.exp(-x))


def _silu(x):
    return x * _sigmoid(x)


def _softplus(x):
    return jnp.maximum(x, 0.0) + jnp.log1p(jnp.exp(-jnp.abs(x)))


def _rms(x, w):
    return x * lax.rsqrt(jnp.mean(x * x, axis=-1, keepdims=True) + NORM_EPS) * w


def _params(sem):
    return pltpu.CompilerParams(dimension_semantics=sem, vmem_limit_bytes=VMEM_LIMIT)


def _mod_kernel(cond_ref, w0, w1, w2, w3, b0, b1, b2, b3, o0, o1, o2, o3):
    a = _silu(cond_ref[...]).astype(BF16)
    for w, b, o in ((w0, b0, o0), (w1, b1, o1), (w2, b2, o2), (w3, b3, o3)):
        o[...] = jnp.dot(a, w[...].astype(BF16), preferred_element_type=F32) + b[...]


def _modulation(cond, mod_ws, mod_bs):
    tn = 512
    wspec = pl.BlockSpec((D, tn), lambda j: (0, j))
    bspec = pl.BlockSpec((1, tn), lambda j: (0, j))
    ospec = pl.BlockSpec((8, tn), lambda j: (0, j))
    outs = pl.pallas_call(
        _mod_kernel,
        grid=(3 * D // tn,),
        in_specs=[pl.BlockSpec((8, D), lambda j: (0, 0))] + [wspec] * 4 + [bspec] * 4,
        out_specs=[ospec] * 4,
        out_shape=[jax.ShapeDtypeStruct((8, 3 * D), F32)] * 4,
        compiler_params=_params(("arbitrary",)),
        name="adaln_mod",
    )(cond, *mod_ws, *[b.reshape(1, 3 * D) for b in mod_bs])
    return [o.reshape(8, 1, 3 * D) for o in outs]


def _adaln_h(x_ref, nw_ref, mod_ref):
    y = _rms(x_ref[...], nw_ref[...])
    return (y * (1.0 + mod_ref[:, D:2 * D]) + mod_ref[:, 0:D]).astype(BF16)


def _proj(h, w_ref, lo, hi):
    return jnp.dot(h, w_ref[:, lo:hi], preferred_element_type=F32)


def _rope(x, cos_ref, sin_ref):
    width = x.shape[1]
    lane = lax.broadcasted_iota(jnp.int32, x.shape, 1)
    partner = jnp.where((lane & 31) < 16, pltpu.roll(x, width - 16, 1), pltpu.roll(x, 16, 1))
    reps = width // LANES
    return x * jnp.tile(cos_ref[...], (1, reps)) + partner * jnp.tile(sin_ref[...], (1, reps))


def _store_cols(ref, cols, val):
    if len(ref.shape) == 2:
        ref[:, cols] = val.astype(ref.dtype)
    else:
        for s in range(ref.shape[0]):
            ref[s, cols, :] = val[s * N_CTX:(s + 1) * N_CTX, :].T.astype(ref.dtype)


def _inproj_a_kernel(*refs, rope):
    if rope:
        x_ref, nw_ref, mod_ref, w_ref, cos_ref, sin_ref, q_ref, k_ref, v_ref, z_ref = refs
    else:
        x_ref, nw_ref, mod_ref, w_ref, q_ref, k_ref, v_ref, z_ref = refs
    h = _adaln_h(x_ref, nw_ref, mod_ref)
    z0 = D + 2 * KVW_A
    for lo in range(0, D, PROJ_COLS):
        acc = _proj(h, w_ref, lo, lo + PROJ_COLS)
        if rope:
            acc = _rope(acc, cos_ref, sin_ref)
        q_ref[:, lo:lo + PROJ_COLS] = (acc * Q_SCALE).astype(BF16)
    kv = _proj(h, w_ref, D, z0)
    k = kv[:, :KVW_A]
    if rope:
        k = _rope(k, cos_ref, sin_ref)
    _store_cols(k_ref, slice(0, KVW_A), k)
    _store_cols(v_ref, slice(0, KVW_A), kv[:, KVW_A:])
    for lo in range(0, D, PROJ_COLS):
        z_ref[:, lo:lo + PROJ_COLS] = _proj(h, w_ref, z0 + lo, z0 + lo + PROJ_COLS).astype(z_ref.dtype)


def _inproj_b_kernel(*refs, rope):
    if rope:
        x_ref, nw_ref, mod_ref, w_ref, cos_ref, sin_ref, q_ref, k_ref, v_ref, z_ref = refs
    else:
        x_ref, nw_ref, mod_ref, w_ref, q_ref, k_ref, v_ref, z_ref = refs
    h = _adaln_h(x_ref, nw_ref, mod_ref)
    for lo in range(0, D, PROJ_COLS):
        sl = slice(lo, lo + PROJ_COLS)
        q = _proj(h, w_ref, lo, lo + PROJ_COLS)
        k = _proj(h, w_ref, D + lo, D + lo + PROJ_COLS)
        if rope:
            q = _rope(q, cos_ref, sin_ref)
            k = _rope(k, cos_ref, sin_ref)
        q_ref[:, sl] = (q * Q_SCALE).astype(BF16)
        _store_cols(k_ref, sl, k)
        _store_cols(v_ref, sl, _proj(h, w_ref, 2 * D + lo, 2 * D + lo + PROJ_COLS))
        z_ref[:, sl] = _proj(h, w_ref, 3 * D + lo, 3 * D + lo + PROJ_COLS).astype(z_ref.dtype)


def _inproj_c_kernel(x_ref, nw_ref, mod_ref, w_ref, wg_ref, alog_ref, dtb_ref, qkv_ref, z_ref, g_ref):
    h = _adaln_h(x_ref, nw_ref, mod_ref)
    for lo in range(0, 3 * D, PROJ_COLS):
        qkv_ref[:, lo:lo + PROJ_COLS] = _proj(h, w_ref, lo, lo + PROJ_COLS).astype(qkv_ref.dtype)
    for lo in range(0, D, PROJ_COLS):
        z_ref[:, lo:lo + PROJ_COLS] = _proj(h, w_ref, 3 * D + lo, 3 * D + lo + PROJ_COLS).astype(z_ref.dtype)
    acc = jnp.dot(h, wg_ref[...], preferred_element_type=F32)
    lane = lax.broadcasted_iota(jnp.int32, acc.shape, 1)
    g = -jnp.exp(alog_ref[...]) * _softplus(acc + dtb_ref[...])
    g_ref[...] = jnp.where(lane < 2 * H_C, _sigmoid(acc), g)


class _FeatureMajor:
    @staticmethod
    def block(width):
        return pl.BlockSpec((TM // N_CTX, width, N_CTX), lambda i: (i, 0, 0))

    @staticmethod
    def shape(rows, width):
        return jax.ShapeDtypeStruct((rows // N_CTX, width, N_CTX), F32)


class _Loaded:
    def __init__(self, value):
        self._value = value

    def __getitem__(self, idx):
        return self._value


def _residual_then(inner, n_inner_inputs, og_ref, ow_ref, prev_mod_ref, x_ref, *refs):
    inner_inputs, (xn_ref, *inner_outputs) = refs[:n_inner_inputs - 1], refs[n_inner_inputs - 1:]
    out = jnp.dot(og_ref[...], ow_ref[...], preferred_element_type=F32)
    xn = x_ref[...] + prev_mod_ref[:, 2 * D:3 * D] * out
    xn_ref[...] = xn
    inner(_Loaded(xn), *inner_inputs, *inner_outputs)


def _inproj(kind, x, norm_w, mod, w, extra_in, outs, latent, rope_tabs=None, prev=None):
    rows = x.shape[0]
    per_seq = (N_LAT if latent else N_CTX) // TM
    mod_row = (lambda i: (i // per_seq, 0, 0)) if latent else (lambda i: (4, 0, 0))
    in_specs = [
        pl.BlockSpec((TM, D), lambda i: (i, 0)),
        pl.BlockSpec((1, D), lambda i: (0, 0)),
        pl.BlockSpec((None, 1, 3 * D), mod_row),
        pl.BlockSpec(w.shape, lambda i: (0, 0), pipeline_mode=pl.Buffered(1)),
    ]
    args = [x, norm_w.reshape(1, D), mod, w]
    for e in extra_in:
        in_specs.append(pl.BlockSpec(e.shape, lambda i: (0, 0)))
        args.append(e)
    rope = rope_tabs is not None
    if rope:
        for t in rope_tabs:
            in_specs.append(pl.BlockSpec((TM, LANES), lambda i: (i % per_seq, 0)))
            args.append(t)
    if kind == "a":
        body = functools.partial(_inproj_a_kernel, rope=rope)
    elif kind == "b":
        body = functools.partial(_inproj_b_kernel, rope=rope)
    else:
        body = _inproj_c_kernel
    out_specs = [_FeatureMajor.block(wd) if dt is _FeatureMajor else pl.BlockSpec((TM, wd), lambda i: (i, 0))
                 for wd, dt in outs]
    out_shape = [_FeatureMajor.shape(rows, wd) if dt is _FeatureMajor else jax.ShapeDtypeStruct((rows, wd), dt)
                 for wd, dt in outs]
    if prev is not None:
        og, out_w, prev_mod = prev
        body = functools.partial(_residual_then, body, len(in_specs))
        in_specs = [pl.BlockSpec((TM, D), lambda i: (i, 0)),
                    pl.BlockSpec((D, D), lambda i: (0, 0), pipeline_mode=pl.Buffered(1)),
                    pl.BlockSpec((None, 1, 3 * D), mod_row)] + in_specs
        args = [og, out_w, prev_mod] + args
        out_specs = [pl.BlockSpec((TM, D), lambda i: (i, 0))] + out_specs
        out_shape = [jax.ShapeDtypeStruct((rows, D), F32)] + out_shape
    return pl.pallas_call(
        body,
        grid=(rows // TM,),
        in_specs=in_specs,
        out_specs=out_specs,
        out_shape=out_shape,
        compiler_params=_params(("parallel",)),
        name=f"inproj_{kind}_{'lat' if latent else 'ctx'}",
    )(*args)


def _final_kernel(ogc_ref, ogl_ref, w_ref, xc_ref, xl_ref, mod_ref, fw_ref, yc_ref, yl_ref, *, ctx_steps):
    def finish(og_ref, x_ref, y_ref):
        out = jnp.dot(og_ref[...], w_ref[...], preferred_element_type=F32)
        y_ref[...] = _rms(x_ref[...] + mod_ref[:, 2 * D:3 * D] * out, fw_ref[...])

    @pl.when(pl.program_id(0) < ctx_steps)
    def _():
        finish(ogc_ref, xc_ref, yc_ref)

    @pl.when(pl.program_id(0) >= ctx_steps)
    def _():
        finish(ogl_ref, xl_ref, yl_ref)


def _final(ogc, ogl, w, xc, xl, mod, final_w):
    nc, nl = xc.shape[0] // TM, xl.shape[0] // TM
    per_seq = N_LAT // TM
    ctx_blk = pl.BlockSpec((TM, D), lambda i: (jnp.minimum(i, nc - 1), 0))
    lat_blk = pl.BlockSpec((TM, D), lambda i: (jnp.maximum(i - nc, 0), 0))
    mod_row = lambda i: (jnp.where(i < nc, 4, (i - nc) // per_seq), 0, 0)
    return pl.pallas_call(
        functools.partial(_final_kernel, ctx_steps=nc),
        grid=(nc + nl,),
        in_specs=[ctx_blk, lat_blk, pl.BlockSpec((D, D), lambda i: (0, 0)), ctx_blk, lat_blk,
                  pl.BlockSpec((None, 1, 3 * D), mod_row), pl.BlockSpec((1, D), lambda i: (0, 0))],
        out_specs=[ctx_blk, lat_blk],
        out_shape=[jax.ShapeDtypeStruct(xc.shape, F32), jax.ShapeDtypeStruct(xl.shape, F32)],
        compiler_params=_params(("arbitrary",)),
        name="final_outproj",
    )(ogc, ogl, w, xc, xl, mod, final_w.reshape(1, D))


class _PerUse:
    def __init__(self, make):
        self._make = make

    def __getitem__(self, i):
        return self._make(i)


def _lane_lo(shape):
    return lax.broadcasted_iota(jnp.int32, shape, 1) < DH


SOFTMAX_BLOCK_ELEMS = 32 * 1024
SOFTMAX_WHOLE_KEYS = 512


def _softmax_rows(sq, sk):
    if sk <= SOFTMAX_WHOLE_KEYS:
        return sq
    rows = 16
    while rows * 2 <= sq and sq % (rows * 2) == 0 and rows * 2 * sk <= SOFTMAX_BLOCK_ELEMS:
        rows *= 2
    return rows


Q_TILES_PER_KV_TILE = 4
_KV_ALIGNED_HEADS = (0, 4, 1, 5, 2, 6, 3, 7, 8, 12, 9, 13, 10, 14, 11, 15)


def _n_keys(x, feature_major):
    return x.shape[1] if feature_major else x.shape[0]


def _scores(lhs, k, feature_major):
    if feature_major:
        return jnp.dot(lhs, k, preferred_element_type=F32)
    return lax.dot_general(lhs, k, _NT, preferred_element_type=F32)


def _weighted_values(p, v, feature_major):
    if feature_major:
        return lax.dot_general(p, v, _NT, preferred_element_type=F32)
    return jnp.dot(p, v, preferred_element_type=F32)


def _gqa_block(sink_ref, q_ref, z_ref, o_ref, kv_pieces, lhs_s, s_s, e_s):
    sq = q_ref.shape[0]
    lo = _lane_lo((1, LANES))
    halves = (lo, jnp.logical_not(lo))
    n_kv_tiles = HKV_A // 2
    pieces = _PerUse(lambda j: [(k2.astype(BF16), v2.astype(BF16), valid, fm)
                                for k2, v2, valid, fm in kv_pieces(j)])
    spans, off = [], 0
    for k2, _, _, fm in pieces[0]:
        spans.append(slice(off, off + _n_keys(k2, fm)))
        off += _n_keys(k2, fm)
    step = _softmax_rows(sq, off)
    blocks = lambda j: [(t, a) for t in range(Q_TILES_PER_KV_TILE * j, Q_TILES_PER_KV_TILE * (j + 1))
                        for a in range(2)]

    def scores_stage(j):
        for i, (t, a) in enumerate(blocks(j)):
            q2 = q_ref[:, t * LANES:(t + 1) * LANES]
            lhs_s[j, i * sq:(i + 1) * sq, :] = jnp.where(halves[a], q2, jnp.zeros_like(q2))
        for (k2, _, _, fm), sp in zip(pieces[j], spans):
            s_s[j, :, sp] = _scores(lhs_s[j], k2, fm)

    def softmax_stage(j):
        inv = []
        for i, (t, a) in enumerate(blocks(j)):
            sink = sink_ref[2 * t + a] * LOG2E
            parts = []
            for r in range(0, sq, step):
                rws = slice(i * sq + r, i * sq + r + step)
                scores = []
                for (_, _, valid, _), sp in zip(pieces[j], spans):
                    s = s_s[j, rws, sp]
                    scores.append(s if valid is None
                                  else jnp.where(valid[r:r + step], s, -jnp.inf))
                m = sink
                for s in scores:
                    m = jnp.maximum(m, jnp.max(s, axis=-1, keepdims=True))
                denom = jnp.exp2(sink - m)
                for s, sp in zip(scores, spans):
                    e = jnp.exp2(s - m)
                    denom = denom + jnp.sum(e, axis=-1, keepdims=True)
                    e_s[j, rws, sp] = e.astype(BF16)
                parts.append(1.0 / denom)
            inv.append(jnp.concatenate(parts, axis=0))
        return inv

    def pv_stage(j, inv):
        pv = None
        for (_, v2, _, fm), sp in zip(pieces[j], spans):
            part = _weighted_values(e_s[j, :, sp], v2, fm)
            pv = part if pv is None else pv + part
        for n_t in range(Q_TILES_PER_KV_TILE):
            t = Q_TILES_PER_KV_TILE * j + n_t
            i0, i1 = 2 * n_t, 2 * n_t + 1
            o2 = jnp.where(lo, pv[i0 * sq:(i0 + 1) * sq] * inv[i0], pv[i1 * sq:(i1 + 1) * sq] * inv[i1])
            cols = slice(t * LANES, (t + 1) * LANES)
            o_ref[:, cols] = (o2 * _silu(z_ref[:, cols].astype(F32))).astype(BF16)

    scores_stage(0)
    for j in range(n_kv_tiles):
        if j + 1 < n_kv_tiles:
            scores_stage(j + 1)
        pv_stage(j, softmax_stage(j))


def _gqa_scratch(sq, sk):
    stacked = 2 * Q_TILES_PER_KV_TILE * sq
    tiles = HKV_A // 2
    return [pltpu.VMEM((tiles, stacked, LANES), BF16), pltpu.VMEM((tiles, stacked, sk), F32),
            pltpu.VMEM((tiles, stacked, sk), BF16)]


def _attn_a_ctx_kernel(sink_ref, q_ref, k_ref, v_ref, z_ref, o_ref, *scratch):
    def kv_pieces(j):
        rws = slice(j * LANES, (j + 1) * LANES)
        return [(k_ref[rws, :], v_ref[rws, :], None, True)]
    _gqa_block(sink_ref, q_ref, z_ref, o_ref, kv_pieces, *scratch)


def _attn_a_lat_kernel(sink_ref, q_ref, k_ref, v_ref, kc_ref, vc_ref, z_ref, o_ref, *scratch):
    n = pl.program_id(1)
    span = A_LAT_QROWS + 2 * WINDOW
    start = pl.multiple_of(jnp.clip(n * A_LAT_QROWS - WINDOW, 0, N_LAT - span), WINDOW)
    qi = n * A_LAT_QROWS + lax.broadcasted_iota(jnp.int32, (A_LAT_QROWS, span), 0)
    kj = start + lax.broadcasted_iota(jnp.int32, (A_LAT_QROWS, span), 1)
    valid = jnp.abs(kj - qi) <= WINDOW

    def kv_pieces(j):
        cols = slice(j * LANES, (j + 1) * LANES)
        return [(k_ref[pl.ds(start, span), cols], v_ref[pl.ds(start, span), cols], valid, False),
                (kc_ref[cols, :], vc_ref[cols, :], None, True)]
    _gqa_block(sink_ref, q_ref, z_ref, o_ref, kv_pieces, *scratch)


def _attn_a(sink, q, k, v, z, cache=None):
    smem = pl.BlockSpec(memory_space=pltpu.SMEM)
    if cache is None:
        return pl.pallas_call(
            _attn_a_ctx_kernel,
            grid=(N_CTX_B,),
            in_specs=[smem,
                      pl.BlockSpec((N_CTX, D), lambda b: (b, 0)),
                      pl.BlockSpec((None, KVW_A, N_CTX), lambda b: (b, 0, 0)),
                      pl.BlockSpec((None, KVW_A, N_CTX), lambda b: (b, 0, 0)),
                      pl.BlockSpec((N_CTX, D), lambda b: (b, 0))],
            out_specs=pl.BlockSpec((N_CTX, D), lambda b: (b, 0)),
            out_shape=jax.ShapeDtypeStruct((N_CTX_B * N_CTX, D), BF16),
            scratch_shapes=_gqa_scratch(N_CTX, N_CTX),
            compiler_params=_params(("parallel",)),
            name="attn_a_ctx",
        )(sink, q, k, v, z)
    kc, vc = cache
    nq = N_LAT // A_LAT_QROWS
    return pl.pallas_call(
        _attn_a_lat_kernel,
        grid=(N_LAT_B, nq),
        in_specs=[smem,
                  pl.BlockSpec((A_LAT_QROWS, D), lambda b, n: (b * nq + n, 0)),
                  pl.BlockSpec((N_LAT, KVW_A), lambda b, n: (b, 0)),
                  pl.BlockSpec((N_LAT, KVW_A), lambda b, n: (b, 0)),
                  pl.BlockSpec((None, KVW_A, PAST), lambda b, n: (b, 0, 0)),
                  pl.BlockSpec((None, KVW_A, PAST), lambda b, n: (b, 0, 0)),
                  pl.BlockSpec((A_LAT_QROWS, D), lambda b, n: (b * nq + n, 0))],
        out_specs=pl.BlockSpec((A_LAT_QROWS, D), lambda b, n: (b * nq + n, 0)),
        out_shape=jax.ShapeDtypeStruct((N_LAT_B * N_LAT, D), BF16),
        scratch_shapes=_gqa_scratch(A_LAT_QROWS, A_LAT_QROWS + 2 * WINDOW + PAST),
        compiler_params=_params(("parallel", "arbitrary")),
        name="attn_a_lat",
    )(sink, q, k, v, kc, vc, z)


def _diff_attn_block(lam_refs, subln_ref, q_ref, z_ref, o_ref, kv_pieces, lam_init, lhs_s, s_s, a_s):
    lq1, lk1, lq2, lk2 = lam_refs
    dot_exp = lambda a, c: jnp.exp(jnp.sum(a[...] * c[...], axis=-1, keepdims=True))
    lam = dot_exp(lq1, lk1) - dot_exp(lq2, lk2) + lam_init
    sq = q_ref.shape[0]
    lo = _lane_lo((1, LANES))
    n_heads = D // LANES
    pieces = _PerUse(kv_pieces)
    spans, off = [], 0
    for k2, _, fm in pieces[0]:
        spans.append(slice(off, off + _n_keys(k2, fm)))
        off += _n_keys(k2, fm)
    step = _softmax_rows(sq, 2 * off)

    def scores_stage(h):
        buf = h % 2
        q2 = q_ref[:, h * LANES:(h + 1) * LANES]
        lhs_s[buf, 0:sq, :] = jnp.where(lo, q2, jnp.zeros_like(q2))
        lhs_s[buf, sq:2 * sq, :] = jnp.where(lo, jnp.zeros_like(q2), q2)
        for (k2, _, fm), sp in zip(pieces[h], spans):
            s_s[buf, :, sp] = _scores(lhs_s[buf], k2, fm)

    def softmax_stage(h):
        buf = h % 2
        inv = []
        for r in range(0, sq, step):
            exps, denoms = [], []
            for c in range(2):
                rws = slice(c * sq + r, c * sq + r + step)
                scores = [s_s[buf, rws, sp] for sp in spans]
                m = None
                for s in scores:
                    sm = jnp.max(s, axis=-1, keepdims=True)
                    m = sm if m is None else jnp.maximum(m, sm)
                es = [jnp.exp2(s - m) for s in scores]
                denom = None
                for e in es:
                    se = jnp.sum(e, axis=-1, keepdims=True)
                    denom = se if denom is None else denom + se
                exps.append(es)
                denoms.append(denom)
            ratio = lam * denoms[0] / denoms[1]
            for e1, e2, sp in zip(exps[0], exps[1], spans):
                a_s[buf, r:r + step, sp] = (e1 - ratio * e2).astype(BF16)
            inv.append(1.0 / denoms[0])
        return jnp.concatenate(inv, axis=0)

    def pv_stage(h, inv):
        buf = h % 2
        cols = slice(h * LANES, (h + 1) * LANES)
        o = None
        for (_, vx, _), sp in zip(pieces[h], spans):
            po = jnp.dot(a_s[buf, :, sp], vx, preferred_element_type=F32)
            o = po if o is None else o + po
        o = _rms(o * inv, subln_ref[...]) * (1.0 - lam_init)
        o_ref[:, cols] = (o * _silu(z_ref[:, cols].astype(F32))).astype(BF16)

    scores_stage(0)
    for h in range(n_heads):
        if h + 1 < n_heads:
            scores_stage(h + 1)
        pv_stage(h, softmax_stage(h))


def _diff_attn_scratch(sq, sk):
    return [pltpu.VMEM((2, 2 * sq, LANES), BF16), pltpu.VMEM((2, 2 * sq, sk), F32),
            pltpu.VMEM((2, sq, sk), BF16)]


def _attn_b_ctx_kernel(lq1, lk1, lq2, lk2, subln_ref, q_ref, k_ref, v_ref, z_ref, o_ref, *scratch, lam_init):
    def kv_pieces(h):
        cols = slice(h * LANES, (h + 1) * LANES)
        return [(k_ref[cols, :].astype(BF16), v_ref[:, cols].astype(BF16), True)]
    _diff_attn_block((lq1, lk1, lq2, lk2), subln_ref, q_ref, z_ref, o_ref, kv_pieces, lam_init, *scratch)


def _attn_b_lat_kernel(lq1, lk1, lq2, lk2, subln_ref, q_ref, k_ref, v_ref, kc_ref, vc_ref, z_ref, o_ref,
                       *scratch, lam_init):
    def kv_pieces(h):
        cols = slice(h * LANES, (h + 1) * LANES)
        return [(k_ref[:, cols].astype(BF16), v_ref[:, cols].astype(BF16), False),
                (kc_ref[cols, :].astype(BF16), vc_ref[:, cols].astype(BF16), True)]
    _diff_attn_block((lq1, lk1, lq2, lk2), subln_ref, q_ref, z_ref, o_ref, kv_pieces, lam_init, *scratch)


def _attn_b(lams, subln, q, k, v, z, lam_init, cache=None):
    small = [pl.BlockSpec((1, DH), lambda *_: (0, 0))] * 4 + [pl.BlockSpec((1, 2 * DH), lambda *_: (0, 0))]
    small_args = [l.reshape(1, DH) for l in lams] + [subln.reshape(1, 2 * DH)]
    if cache is None:
        blk = pl.BlockSpec((N_CTX, D), lambda b: (b, 0))
        return pl.pallas_call(
            functools.partial(_attn_b_ctx_kernel, lam_init=lam_init),
            grid=(N_CTX_B,),
            in_specs=small + [blk, pl.BlockSpec((None, D, N_CTX), lambda b: (b, 0, 0)), blk, blk],
            out_specs=blk,
            out_shape=jax.ShapeDtypeStruct((N_CTX_B * N_CTX, D), BF16),
            scratch_shapes=_diff_attn_scratch(N_CTX, N_CTX),
            compiler_params=_params(("parallel",)),
            name="attn_b_ctx",
        )(*small_args, q, k, v, z)
    kc, vc = cache
    tq = 512
    nq = N_LAT // tq
    qblk = pl.BlockSpec((tq, D), lambda b, n: (b * nq + n, 0))
    return pl.pallas_call(
        functools.partial(_attn_b_lat_kernel, lam_init=lam_init),
        grid=(N_LAT_B, nq),
        in_specs=small + [qblk,
                          pl.BlockSpec((N_LAT, D), lambda b, n: (b, 0)),
                          pl.BlockSpec((N_LAT, D), lambda b, n: (b, 0)),
                          pl.BlockSpec((None, D, PAST), lambda b, n: (b, 0, 0)),
                          pl.BlockSpec((PAST, D), lambda b, n: (b, 0)),
                          qblk],
        out_specs=qblk,
        out_shape=jax.ShapeDtypeStruct((N_LAT_B * N_LAT, D), BF16),
        scratch_shapes=_diff_attn_scratch(tq, N_LAT + PAST),
        compiler_params=_params(("parallel", "arbitrary")),
        name="attn_b_lat",
    )(*small_args, q, k, v, kc, vc, z)


def _mm(a, b):
    return jnp.dot(a.astype(BF16), b.astype(BF16), preferred_element_type=F32)


def _each(fn, *lists):
    return [fn(*xs) for xs in zip(*lists)]


def _unit_tri_inverse_residuals(ls, same_blk, mm):
    squarings = INV_BLOCK_LOG2 - 1
    rd = [jnp.where(same_blk, -l, 0.0) for l in ls]
    lo = [jnp.where(same_blk, 0.0, l) for l in ls]
    pk = _each(mm, rd, rd)
    for it in range(squarings):
        t = _each(mm, rd, pk)
        nxt = _each(mm, pk, pk) if it < squarings - 1 else None
        rd = _each(lambda r, p, x: r + p + x, rd, pk, t)
        pk = nxt
    m = _each(lambda x, r: x + mm(r, x), lo, rd)
    m2 = _each(mm, m, m)
    q = _each(lambda a, a2: a2 - a - mm(a, a2), m, m2)
    return _each(lambda a, r: a + r + mm(a, r), q, rd)


def _gdn_kernel(*refs, n, hp, has_s0, want_state):
    it = iter(refs)
    qp_ref, kp_ref, vp_ref, cwq_ref, cwk_ref, cwv_ref, g_ref, z_ref, onw_ref = [next(it) for _ in range(9)]
    s0_ref = next(it) if has_s0 else None
    og_ref = next(it)
    sf_ref = next(it) if want_state else None
    q_s, k_s, v_s, beta_b, gc_b, gtot_b, u_s, wq_s, intra_s, bb_s, ab_s, st_s = it
    nc = n // CHUNK
    head0 = pl.program_id(1) * hp

    row = lax.broadcasted_iota(jnp.int32, (n, LANES), 0)
    lane = lax.broadcasted_iota(jnp.int32, (n, LANES), 1)

    def conv_silu(p_ref, cw_ref, cols):
        x = p_ref[:, cols].astype(F32)
        xm1 = jnp.where(row == 0, 0.0, pltpu.roll(x, 1, 0))
        xp1 = jnp.where(row == n - 1, 0.0, pltpu.roll(x, n - 1, 0))
        return _silu(cw_ref[0:1, cols] * xm1 + cw_ref[1:2, cols] * x + cw_ref[2:3, cols] * xp1)

    def l2n(x):
        return x * lax.rsqrt(jnp.sum(x * x, axis=-1, keepdims=True) + L2_EPS)

    for p in range(hp):
        cols = slice(p * LANES, (p + 1) * LANES)
        q_s[p] = l2n(conv_silu(qp_ref, cwq_ref, cols)) * (DK_C ** -0.5)
        k_s[p] = l2n(conv_silu(kp_ref, cwk_ref, cols))
        v_s[p] = conv_silu(vp_ref, cwv_ref, cols)

    gates = g_ref[...]
    local = row & (CHUNK - 1)
    pre = gates
    suf = gates
    s = 1
    while s < CHUNK:
        pre = pre + jnp.where(local >= s, pltpu.roll(pre, s, 0), 0.0)
        suf = suf + jnp.where(local < CHUNK - s, pltpu.roll(suf, n - s, 0), 0.0)
        s *= 2
    tot = pre + suf - gates

    def col(x, idx):
        picked = jnp.sum(jnp.where(lane == idx, x, 0.0), axis=1, keepdims=True)
        return jnp.broadcast_to(picked, (n, LANES))

    for p in range(hp):
        for d in range(2):
            beta_b[p, d] = col(gates, d * H_C + head0 + p)
            gc_b[p, d] = col(pre if d == 0 else suf, 2 * H_C + d * H_C + head0 + p)
            gtot_b[p, d] = col(tot, 2 * H_C + d * H_C + head0 + p)

    ii = lax.broadcasted_iota(jnp.int32, (CHUNK, LANES), 0)
    lane2 = lax.broadcasted_iota(jnp.int32, (CHUNK, LANES), 1)
    fwd = lane2 < CHUNK
    jj = lane2 & (CHUNK - 1)
    same_blk = (ii >> INV_BLOCK_LOG2) == (jj >> INV_BLOCK_LOG2)
    assert CHUNK >> INV_BLOCK_LOG2 == 4
    diag = ii == jj
    ahead = jnp.where(fwd, ii - jj, jj - ii)
    incl = ahead >= 0
    strict = ahead > 0
    heads = range(hp)
    pairs = [(p, c) for p in heads for c in range(nc)]
    chains = [(p, c, d) for p, c in pairs for d in range(2)]
    rows = lambda c: pl.ds(c * CHUNK, CHUNK)
    rows2 = lambda c: pl.ds(c * LANES, LANES)
    zeros_b = jnp.zeros((CHUNK, LANES), BF16)

    def block_diag(x2):
        return jnp.concatenate([jnp.where(fwd, x2, 0.0), jnp.where(fwd, 0.0, x2)], axis=0).astype(BF16)

    def packed_mm(x2, y2):
        return jnp.dot(x2.astype(BF16), block_diag(y2), preferred_element_type=F32)

    kc = [k_s[p, rows(c), :] for p, c in pairs]
    qc = [q_s[p, rows(c), :] for p, c in pairs]
    beta = [beta_b[p, d, rows(c), :] for p, c, d in chains]
    gc = [gc_b[p, d, rows(c), :] for p, c, d in chains]
    kb = [kc[i // 2] * b for i, b in enumerate(beta)]
    eg = [jnp.exp(g) for g in gc]
    a = []
    for j in range(len(pairs)):
        k_b = kc[j].astype(BF16)
        lhs = jnp.concatenate([jnp.concatenate([kb[2 * j], kb[2 * j + 1]], axis=1),
                               jnp.concatenate([qc[j], qc[j]], axis=1)], axis=0).astype(BF16)
        rhs_nt = jnp.concatenate([jnp.concatenate([k_b, zeros_b], axis=1),
                                  jnp.concatenate([zeros_b, k_b], axis=1)], axis=0)
        a.append(lax.dot_general(lhs, rhs_nt, _NT, preferred_element_type=F32))
    kd = [(kc[i // 2] * jnp.exp(gtot_b[p, d, rows(c), :] - gc[i])).astype(BF16)
          for i, (p, c, d) in enumerate(chains)]
    ls = []
    for j, (p, c) in enumerate(pairs):
        gci = jnp.where(fwd, gc[2 * j], gc[2 * j + 1])
        gcj = jnp.sum(jnp.where(diag, gci, 0.0), axis=0, keepdims=True)
        decay = jnp.where(incl, jnp.exp(jnp.where(incl, gci - gcj, 0.0)), 0.0)
        ls.append(jnp.where(strict, a[j][:CHUNK] * decay, 0.0))
        intra_s[p, rows(c), :] = jnp.where(incl, a[j][CHUNK:] * decay, 0.0)
        for d in range(2):
            wq_s[p, d, pl.ds(c * LANES + CHUNK, CHUNK), :] = (qc[j] * eg[2 * j + d]).astype(BF16)
    r = _unit_tri_inverse_residuals(ls, same_blk, packed_mm)
    rhs = [jnp.concatenate([v_s[p, rows(c), :] * b, x * e], axis=1)
           for (p, c, d), b, x, e in zip(chains, beta, kb, eg)]
    zeros_w = jnp.zeros((CHUNK, 2 * LANES), BF16)
    sol = []
    for j in range(len(pairs)):
        rf, rb = rhs[2 * j], rhs[2 * j + 1]
        both = jnp.concatenate([jnp.concatenate([rf.astype(BF16), zeros_w], axis=1),
                                jnp.concatenate([zeros_w, rb.astype(BF16)], axis=1)], axis=0)
        x = jnp.dot(r[j].astype(BF16), both, preferred_element_type=F32)
        sol.extend([rf + x[:, :2 * LANES], rb + x[:, 2 * LANES:]])
    for (p, c, d), x in zip(chains, sol):
        u_s[p, d, rows(c), :] = x[:, :LANES]
        wq_s[p, d, pl.ds(c * LANES, CHUNK), :] = x[:, LANES:].astype(BF16)
    ba = _each(lambda t, x: lax.dot_general(t, x.astype(BF16), _TN, preferred_element_type=F32),
               kd, sol)
    for (p, c, d), x in zip(chains, ba):
        bb_s[p, d, rows2(c), :] = x[:, :LANES]
        ab_s[p, d, rows2(c), :] = x[:, LANES:].astype(BF16)

    scans = [(p, d) for p in heads for d in range(2)]
    if has_s0:
        state = [s0_ref[d, p] for p, d in scans]
    else:
        state = [jnp.zeros((DK_C, LANES), F32) for _ in scans]
    for i in range(nc):
        for j, (p, d) in enumerate(scans):
            c = i if d == 0 else nc - 1 - i
            st_b = state[j].astype(BF16)
            st_s[p, d, rows2(c), :] = st_b
            eg_tot = jnp.exp(gtot_b[p, d, pl.ds(c * CHUNK, 1), :])
            state[j] = (state[j] * eg_tot + bb_s[p, d, rows2(c), :]
                        - jnp.dot(ab_s[p, d, rows2(c), :], st_b, preferred_element_type=F32))

    wo = [jnp.dot(wq_s[p, d, rows2(c), :], st_s[p, d, rows2(c), :], preferred_element_type=F32)
          for p, c, d in chains]
    vnew = [u_s[p, d, rows(c), :] - x[:CHUNK] for (p, c, d), x in zip(chains, wo)]
    for j, (p, c) in enumerate(pairs):
        cols = slice(p * LANES, (p + 1) * LANES)
        oc = (wo[2 * j][CHUNK:] + wo[2 * j + 1][CHUNK:]
              + _mm(intra_s[p, rows(c), :], jnp.concatenate([vnew[2 * j], vnew[2 * j + 1]], axis=0)))
        og_ref[rows(c), cols] = (_rms(oc, onw_ref[...]) * _silu(z_ref[rows(c), cols].astype(F32))).astype(BF16)
    if want_state:
        for j, (p, d) in enumerate(scans):
            sf_ref[d, p] = state[j]


def _gdn(qkv, conv_w, gates, z, onorm_w, latent, s0=None):
    n = N_LAT if latent else N_CTX
    nb = N_LAT_B if latent else N_CTX_B
    nc = n // CHUNK
    want_state = not latent
    hp = 2 if latent else 8
    groups = H_C // hp
    col = lambda off: pl.BlockSpec((n, hp * LANES), lambda b, h, off=off: (b, off * groups + h))
    cw = lambda off: pl.BlockSpec((3, hp * LANES), lambda b, h, off=off: (0, off * groups + h))
    state_spec = pl.BlockSpec((None, 2, hp, DK_C, LANES), lambda b, h: (b, 0, h, 0, 0))
    in_specs = [col(0), col(1), col(2), cw(0), cw(1), cw(2),
                pl.BlockSpec((n, LANES), lambda b, h: (b, 0)),
                pl.BlockSpec((n, hp * LANES), lambda b, h: (b, h)),
                pl.BlockSpec((1, LANES), lambda b, h: (0, 0))]
    args = [qkv, qkv, qkv, conv_w, conv_w, conv_w, gates, z, onorm_w.reshape(1, LANES)]
    if s0 is not None:
        in_specs.append(state_spec)
        args.append(s0)
    out_specs = [pl.BlockSpec((n, hp * LANES), lambda b, h: (b, h))]
    out_shape = [jax.ShapeDtypeStruct((nb * n, D), BF16)]
    if want_state:
        out_specs.append(state_spec)
        out_shape.append(jax.ShapeDtypeStruct((nb, 2, H_C, DK_C, LANES), F32))
    seq = lambda: pltpu.VMEM((hp, n, LANES), F32)
    both = lambda: pltpu.VMEM((hp, 2, n, LANES), F32)
    per_chunk = lambda dt: pltpu.VMEM((hp, 2, nc * LANES, LANES), dt)
    scratch = [seq(), seq(), seq(), both(), both(), both(), both(),
               per_chunk(BF16), seq(), per_chunk(F32), per_chunk(BF16),
               per_chunk(BF16)]
    return pl.pallas_call(
        functools.partial(_gdn_kernel, n=n, hp=hp, has_s0=s0 is not None, want_state=want_state),
        grid=(nb, groups),
        in_specs=in_specs,
        out_specs=out_specs,
        out_shape=out_shape,
        scratch_shapes=scratch,
        compiler_params=_params(("parallel", "arbitrary")),
        name=f"gdn_{'lat' if latent else 'ctx'}",
    )(*args)


def _rope_tables():
    rows = N_LAT // GRID_W
    row = jnp.repeat(jnp.arange(rows), GRID_W).astype(F32)
    colp = jnp.tile(jnp.arange(GRID_W), rows).astype(F32)
    quarter = DH // 4
    inv = ROPE_BASE ** (-jnp.arange(quarter, dtype=F32) / quarter)
    ar, ac = row[:, None] * inv, colp[:, None] * inv
    cos = jnp.concatenate([jnp.cos(ar)] * 2 + [jnp.cos(ac)] * 2, axis=-1)
    sin = jnp.concatenate([-jnp.sin(ar), jnp.sin(ar), -jnp.sin(ac), jnp.sin(ac)], axis=-1)
    return jnp.tile(cos, (1, 2)), jnp.tile(sin, (1, 2))


def kernel(x_prompt, x_sample, cache_l0_k, cache_l0_v, cache_l1_k, cache_l1_v, state_l2, cache_l3_k, cache_l3_v, c, c_ctx, l0_norm_w, l0_mod_w, l0_mod_b, l0_in_w, l0_out_w, l0_sink, l1_norm_w, l1_mod_w, l1_mod_b, l1_in_w, l1_out_w, l1_lambda_q1, l1_lambda_k1, l1_lambda_q2, l1_lambda_k2, l1_subln_w, l2_norm_w, l2_mod_w, l2_mod_b, l2_in_w, l2_out_w, l2_conv_w, l2_a_log, l2_dt_bias, l2_onorm_w, l3_norm_w, l3_mod_w, l3_mod_b, l3_in_w, l3_out_w, l3_sink, final_norm_w):
    xc = x_prompt.reshape(N_CTX_B * N_CTX, D)
    xl = x_sample.reshape(N_LAT_B * N_LAT, D)
    cond = jnp.concatenate([c, c_ctx[None, :], jnp.zeros((3, D), F32)], axis=0)
    mods = _modulation(cond, (l0_mod_w, l1_mod_w, l2_mod_w, l3_mod_w), (l0_mod_b, l1_mod_b, l2_mod_b, l3_mod_b))
    rope_tabs = _rope_tables()
    new_state = []

    def feature_major(t):
        nd = t.ndim
        return jnp.transpose(t, (0, *range(2, nd), 1)).reshape(t.shape[0], -1, t.shape[1])

    def token_major(t, feature_dims):
        t = t.reshape(t.shape[0], *feature_dims, t.shape[-1])
        return jnp.transpose(t, (0, t.ndim - 1, *range(1, t.ndim - 1)))

    def project(kind, xs, prev, norm_w, mod, w, extra, outs_ctx, outs_lat, rope):
        res = []
        for g, (x, outs) in enumerate(zip(xs, (outs_ctx, outs_lat))):
            fused = None if prev is None else (prev[g], prev[2], prev[3])
            res.append(_inproj(kind, x, norm_w, mod, w, extra, outs, latent=bool(g),
                               rope_tabs=rope_tabs if (rope and g) else None, prev=fused))
        if prev is not None:
            xs = (res[0][0], res[1][0])
            res = [r[1:] for r in res]
        return xs, res[0], res[1]

    def layer_a(xs, prev, mod, norm_w, in_w, out_w, sink, cache_k, cache_v):
        head = lambda h: slice(h * DH, (h + 1) * DH)
        by_head = lambda cols: [cols[:, head(h)] for h in _KV_ALIGNED_HEADS]
        in_w, out_w = in_w.astype(BF16), out_w.astype(BF16)
        w = jnp.concatenate(by_head(in_w[:, :D]) + [in_w[:, D:D + 2 * KVW_A]] + by_head(in_w[:, D + 2 * KVW_A:]),
                            axis=1)
        wo = jnp.concatenate([out_w[head(h)] for h in _KV_ALIGNED_HEADS], axis=0)
        sink = jnp.concatenate([sink[h:h + 1] for h in _KV_ALIGNED_HEADS])
        outs = lambda kv_dt: [(D, BF16), (KVW_A, kv_dt), (KVW_A, kv_dt), (D, BF16)]
        xs, (qc, kc, vc, zc), (ql, kl, vl, zl) = project("a", xs, prev, norm_w, mod, w, [], outs(_FeatureMajor),
                                                         outs(BF16), rope=True)
        ogc = _attn_a(sink, qc, kc, vc, zc)
        ogl = _attn_a(sink, ql, kl, vl, zl, cache=(feature_major(cache_k), feature_major(cache_v)))
        new_state.extend([token_major(kc, (HKV_A, DH)), token_major(vc, (HKV_A, DH))])
        return xs, (ogc, ogl, wo, mod)

    xs, prev = layer_a((xc, xl), None, mods[0], l0_norm_w, l0_in_w, l0_out_w, l0_sink, cache_l0_k, cache_l0_v)

    lam_init = 0.8 - 0.6 * math.exp(-0.3 * 1)
    outs = lambda k_dt, v_dt: [(D, BF16), (D, k_dt), (D, v_dt), (D, BF16)]
    xs, (qc, kc, vc, zc), (ql, kl, vl, zl) = project("b", xs, prev, l1_norm_w, mods[1], l1_in_w.astype(BF16), [],
                                                     outs(_FeatureMajor, F32), outs(BF16, BF16), rope=True)
    lams = (l1_lambda_q1, l1_lambda_k1, l1_lambda_q2, l1_lambda_k2)
    ogc = _attn_b(lams, l1_subln_w, qc, kc, vc, zc, lam_init)
    ogl = _attn_b(lams, l1_subln_w, ql, kl, vl, zl, lam_init,
                  cache=(feature_major(cache_l1_k), cache_l1_v.reshape(N_LAT_B * PAST, D)))
    prev = (ogc, ogl, l1_out_w.astype(BF16), mods[1])
    new_state.extend([token_major(kc, (H_C, 2, DH)), vc.reshape(N_CTX_B, N_CTX, H_C, 2 * DH)])

    w = l2_in_w.astype(BF16)
    wg = jnp.pad(l2_in_w[:, 4 * D:], ((0, 0), (0, LANES - 4 * H_C))).astype(BF16)
    lane_pad = lambda p: jnp.pad(p.reshape(1, 2 * H_C), ((0, 0), (2 * H_C, LANES - 4 * H_C)))
    extra = [wg, lane_pad(l2_a_log), lane_pad(l2_dt_bias)]
    outs = [(3 * D, BF16), (D, BF16), (LANES, F32)]
    xs, (qkvc, zc, gc), (qkvl, zl, gl) = project("c", xs, prev, l2_norm_w, mods[2], w, extra, outs, outs,
                                                 rope=False)
    ogc, st_new = _gdn(qkvc, l2_conv_w, gc, zc, l2_onorm_w, latent=False)
    (ogl,) = _gdn(qkvl, l2_conv_w, gl, zl, l2_onorm_w, latent=True, s0=state_l2)
    prev = (ogc, ogl, l2_out_w.astype(BF16), mods[2])
    new_state.append(st_new)

    xs, (ogc, ogl, wo, mod) = layer_a(xs, prev, mods[3], l3_norm_w, l3_in_w, l3_out_w, l3_sink,
                                      cache_l3_k, cache_l3_v)
    yc, yl = _final(ogc, ogl, wo, xs[0], xs[1], mod, final_norm_w)
    return (yc.reshape(N_CTX_B, N_CTX, D), yl.reshape(N_LAT_B, N_LAT, D), *new_state)
```
